```python
import jax, jax.numpy as jnp
from jax import lax
import numpy as np

D_MODEL = 2048
BATCH = 8
SEQ = 4096
DEPTH = 4

GRID_W = 64
CTX_LEN = 256
N_MIXERS = 2
N_ATT_LAYERS = (DEPTH + N_MIXERS - 1) // N_MIXERS
N_GLA_LAYERS = DEPTH // N_MIXERS
ATT_HEADS = 16
ATT_KV_HEADS = 4
HEAD_DIM = D_MODEL // ATT_HEADS
WINDOW = 128
ATT_BLOCK = 128
ROPE_BASE = 10000.0
ROPE_AXIS_DIM = HEAD_DIM // 2
GLA_HEADS = 4
GLA_DK = D_MODEL // 2
GLA_DV = D_MODEL
GLA_DK_HEAD = GLA_DK // GLA_HEADS
GLA_DV_HEAD = GLA_DV // GLA_HEADS
GLA_GATE_RANK = 16
GLA_GATE_NORM = 16.0
GLA_CHUNK = 16
D_FF = 5632
CONV_W = 3
EPS = 1e-6
N_MOD = 6

kernel_name = "hybrid_swa_gla_convffn_dit"


def rmsnorm(x, g):
    xf = x.astype(jnp.float32)
    y = xf * lax.rsqrt(jnp.mean(xf * xf, axis=-1, keepdims=True) + EPS)
    return (y * g.astype(jnp.float32)).astype(x.dtype)


def modulate(x, g, shift, scale):
    return rmsnorm(x, g) * (1 + scale) + shift


def axial_rope_tables(n):
    rows = n // GRID_W
    row = jnp.repeat(jnp.arange(rows, dtype=jnp.float32), GRID_W)
    col = jnp.tile(jnp.arange(GRID_W, dtype=jnp.float32), rows)
    inv = ROPE_BASE ** (-jnp.arange(0, ROPE_AXIS_DIM, 2, dtype=jnp.float32) / ROPE_AXIS_DIM)
    ang_r = row[:, None] * inv[None, :]
    ang_c = col[:, None] * inv[None, :]
    return (jnp.cos(ang_r), jnp.sin(ang_r), jnp.cos(ang_c), jnp.sin(ang_c))


def rope_1d(x, cos, sin):
    shp = (cos.shape[0],) + (1,) * (x.ndim - 3) + (cos.shape[1],)
    c = cos.reshape(shp)
    s = sin.reshape(shp)
    x1, x2 = jnp.split(x, 2, axis=-1)
    return jnp.concatenate([x1 * c - x2 * s, x2 * c + x1 * s], axis=-1)


def apply_axial_rope(x, rope):
    cos_r, sin_r, cos_c, sin_c = rope
    xr, xc = jnp.split(x.astype(jnp.float32), 2, axis=-1)
    out = jnp.concatenate([rope_1d(xr, cos_r, sin_r), rope_1d(xc, cos_c, sin_c)], axis=-1)
    return out.astype(x.dtype)


def sink_softmax(logits, sink_logit):
    m = jnp.maximum(jnp.max(logits, axis=-1, keepdims=True), sink_logit)
    e = jnp.exp(logits - m)
    return e / (jnp.sum(e, axis=-1, keepdims=True) + jnp.exp(sink_logit - m))


def attn_mixer(h_ctx, h_lat, w_qkv, sink, w_o, rope, ctx_out):
    B, N, _ = h_lat.shape
    G = ATT_HEADS // ATT_KV_HEADS
    nq = ATT_HEADS * HEAD_DIM
    nkv = ATT_KV_HEADS * HEAD_DIM
    scale = HEAD_DIM ** -0.5

    def proj(h):
        n = h.shape[1]
        q, k, v = jnp.split(h @ w_qkv, [nq, nq + nkv], axis=-1)
        return (q.reshape(B, n, ATT_KV_HEADS, G, HEAD_DIM),
                k.reshape(B, n, ATT_KV_HEADS, HEAD_DIM),
                v.reshape(B, n, ATT_KV_HEADS, HEAD_DIM))

    qc, kc, vc = proj(h_ctx)
    ql, kl, vl = proj(h_lat)
    ql = apply_axial_rope(ql, rope)
    kl = apply_axial_rope(kl, rope)
    sink_b = sink.astype(jnp.float32).reshape(ATT_KV_HEADS, G)[None, :, :, None, None]

    oc = None
    if ctx_out:
        s = jnp.einsum('blkgd,bmkd->bkglm', qc, kc, preferred_element_type=jnp.float32) * scale
        p = sink_softmax(s, sink_b).astype(vc.dtype)
        oc = jnp.einsum('bkglm,bmkd->blkgd', p, vc).reshape(B, -1, nq) @ w_o

    nb = N // ATT_BLOCK

    def windows(t):
        tp = jnp.pad(t, ((0, 0), (ATT_BLOCK, ATT_BLOCK), (0, 0), (0, 0)))
        tp = tp.reshape(B, nb + 2, ATT_BLOCK, ATT_KV_HEADS, HEAD_DIM)
        w = jnp.concatenate([tp[:, :-2], tp[:, 1:-1], tp[:, 2:]], axis=2)
        return jnp.moveaxis(w, 1, 0)

    qb = jnp.moveaxis(ql.reshape(B, nb, ATT_BLOCK, ATT_KV_HEADS, G, HEAD_DIM), 1, 0)
    kw = windows(kl)
    vw = windows(vl)
    qi = jnp.arange(ATT_BLOCK)[:, None]
    kj = jnp.arange(3 * ATT_BLOCK)[None, :]
    band = jnp.abs(qi - kj + ATT_BLOCK) <= WINDOW

    def block(args):
        q, k, v, blk = args
        kpos = blk * ATT_BLOCK - ATT_BLOCK + kj
        mask = band & (kpos >= 0) & (kpos < N)
        s_w = jnp.einsum('bqkgd,bjkd->bkgqj', q, k, preferred_element_type=jnp.float32) * scale
        s_w = jnp.where(mask, s_w, -jnp.inf)
        s_c = jnp.einsum('bqkgd,bjkd->bkgqj', q, kc, preferred_element_type=jnp.float32) * scale
        p = sink_softmax(jnp.concatenate([s_w, s_c], axis=-1), sink_b).astype(v.dtype)
        o = (jnp.einsum('bkgqj,bjkd->bqkgd', p[..., :3 * ATT_BLOCK], v)
             + jnp.einsum('bkgqj,bjkd->bqkgd', p[..., 3 * ATT_BLOCK:], vc))
        return o.reshape(B, ATT_BLOCK, nq)

    ol = lax.map(block, (qb, kw, vw, jnp.arange(nb)))
    ol = jnp.moveaxis(ol, 0, 1).reshape(B, N, nq) @ w_o
    return oc, ol


def gla_chunk_scan(q, k, v, g, s0):
    B, H, n, dk = q.shape
    dv = v.shape[-1]
    nc = n // GLA_CHUNK
    f32 = jnp.float32
    qf = q.astype(f32).reshape(B, H, nc, GLA_CHUNK, dk)
    kf = k.astype(f32).reshape(B, H, nc, GLA_CHUNK, dk)
    vf = v.astype(f32).reshape(B, H, nc, GLA_CHUNK, dv)
    b = jnp.cumsum(g.astype(f32).reshape(B, H, nc, GLA_CHUNK, dk), axis=3)
    b_last = b[:, :, :, -1]
    q_in = qf * jnp.exp(b)
    k_in = kf * jnp.exp(-b)
    k_st = kf * jnp.exp(b_last[:, :, :, None, :] - b)
    causal = jnp.tril(jnp.ones((GLA_CHUNK, GLA_CHUNK), dtype=bool))
    a = jnp.where(causal, jnp.einsum('bhcid,bhcjd->bhcij', q_in, k_in), 0.0)
    o_intra = jnp.einsum('bhcij,bhcjv->bhciv', a, vf)

    def step(S, xs):
        qi, ks, vi, dl = xs
        o = jnp.einsum('bhcd,bhdv->bhcv', qi, S)
        S = S * jnp.exp(dl)[..., None] + jnp.einsum('bhcd,bhcv->bhdv', ks, vi)
        return S, o

    xs = (jnp.moveaxis(q_in, 2, 0), jnp.moveaxis(k_st, 2, 0),
          jnp.moveaxis(vf, 2, 0), jnp.moveaxis(b_last, 2, 0))
    S, o_inter = lax.scan(step, s0.astype(f32), xs)
    o = o_intra + jnp.moveaxis(o_inter, 0, 2)
    return o.reshape(B, H, n, dv), S


def gla_mixer(h_ctx, h_lat, w_in, gf_w1, gf_w2, gf_b, gb_w1, gb_w2, gb_b, onorm_g, w_o, ctx_out):
    B = h_lat.shape[0]

    def heads(t, d):
        return t.reshape(B, t.shape[1], GLA_HEADS, d).transpose(0, 2, 1, 3)

    def gate(h, w1, w2, bias):
        z = ((h @ w1) @ w2 + bias).astype(jnp.float32)
        return heads(jax.nn.log_sigmoid(z) / GLA_GATE_NORM, GLA_DK_HEAD)

    def proj(h):
        q, k, v, r = jnp.split(h @ w_in, [GLA_DK, 2 * GLA_DK, 2 * GLA_DK + GLA_DV], axis=-1)
        q = heads(q, GLA_DK_HEAD) * (GLA_DK_HEAD ** -0.5)
        return (q, heads(k, GLA_DK_HEAD), heads(v, GLA_DV_HEAD), r,
                gate(h, gf_w1, gf_w2, gf_b), gate(h, gb_w1, gb_w2, gb_b))

    def flip(t):
        return t[:, :, ::-1]

    def bidir(q, k, v, gf, gb, sf0, sb0):
        of, sf = gla_chunk_scan(q, k, v, gf, sf0)
        ob, sb = gla_chunk_scan(flip(q), flip(k), flip(v), flip(gb), sb0)
        return of + flip(ob), sf, sb

    def out(o, r):
        o = rmsnorm(o, onorm_g)
        o = o.transpose(0, 2, 1, 3).reshape(B, o.shape[2], GLA_DV).astype(r.dtype)
        return (o * jax.nn.silu(r)) @ w_o

    s0 = jnp.zeros((B, GLA_HEADS, GLA_DK_HEAD, GLA_DV_HEAD), jnp.float32)
    qc, kc, vc, rc, gfc, gbc = proj(h_ctx)
    o_c, sf, sb = bidir(qc, kc, vc, gfc, gbc, s0, s0)
    ql, kl, vl, rl, gfl, gbl = proj(h_lat)
    o_l, _, _ = bidir(ql, kl, vl, gfl, gbl, sf, sb)
    oc = out(o_c, rc) if ctx_out else None
    return oc, out(o_l, rl)


def conv_ffn(h, w_up, conv_w, conv_b, w_down):
    u = h @ w_up
    n = u.shape[1]
    up = jnp.pad(u, ((0, 0), (CONV_W // 2, CONV_W // 2), (0, 0)))
    u = conv_b + sum(conv_w[j] * up[:, j:j + n] for j in range(CONV_W))
    gate, val = jnp.split(u, 2, axis=-1)
    return (jax.nn.silu(gate) * val) @ w_down


def _fwd_setup_inputs(seed: int = 0) -> dict:
    key = jax.random.key(seed)
    ks = jax.random.split(key, 32)
    D = D_MODEL
    nrm = jax.random.normal
    f = jnp.float32
    qkv_w = ATT_HEADS * HEAD_DIM + 2 * ATT_KV_HEADS * HEAD_DIM
    gla_in_w = 2 * GLA_DK + 2 * GLA_DV
    return {
        "x": nrm(ks[0], (BATCH, SEQ, D), f),
        "c": nrm(ks[1], (BATCH, D), f),
        "ctx": nrm(ks[2], (BATCH, CTX_LEN, D), f),
        "c_ctx": nrm(ks[3], (D,), f),
        "ada_w": nrm(ks[4], (DEPTH, D, N_MOD * D), f) * (0.5 * D ** -0.5),
        "ada_b": nrm(ks[5], (DEPTH, N_MOD * D), f) * 0.02,
        "norm_mix_g": 1.0 + 0.02 * nrm(ks[6], (DEPTH, D), f),
        "norm_ffn_g": 1.0 + 0.02 * nrm(ks[7], (DEPTH, D), f),
        "ffn_w_up": nrm(ks[8], (DEPTH, D, 2 * D_FF), f) * D ** -0.5,
        "ffn_conv_w": nrm(ks[9], (DEPTH, CONV_W, 2 * D_FF), f) * CONV_W ** -0.5,
        "ffn_conv_b": nrm(ks[10], (DEPTH, 2 * D_FF), f) * 0.02,
        "ffn_w_down": nrm(ks[11], (DEPTH, D_FF, D), f) * D_FF ** -0.5,
        "attn_w_qkv": nrm(ks[12], (N_ATT_LAYERS, D, qkv_w), f) * D ** -0.5,
        "attn_sink": nrm(ks[13], (N_ATT_LAYERS, ATT_HEADS), f) * 0.5,
        "attn_w_o": nrm(ks[14], (N_ATT_LAYERS, ATT_HEADS * HEAD_DIM, D), f) * (ATT_HEADS * HEAD_DIM) ** -0.5,
        "gla_w_in": nrm(ks[15], (N_GLA_LAYERS, D, gla_in_w), f) * D ** -0.5,
        "gla_gf_w1": nrm(ks[16], (N_GLA_LAYERS, D, GLA_GATE_RANK), f) * D ** -0.5,
        "gla_gf_w2": nrm(ks[17], (N_GLA_LAYERS, GLA_GATE_RANK, GLA_DK), f) * GLA_GATE_RANK ** -0.5,
        "gla_gf_b": nrm(ks[18], (N_GLA_LAYERS, GLA_DK), f) * 0.1,
        "gla_gb_w1": nrm(ks[19], (N_GLA_LAYERS, D, GLA_GATE_RANK), f) * D ** -0.5,
        "gla_gb_w2": nrm(ks[20], (N_GLA_LAYERS, GLA_GATE_RANK, GLA_DK), f) * GLA_GATE_RANK ** -0.5,
        "gla_gb_b": nrm(ks[21], (N_GLA_LAYERS, GLA_DK), f) * 0.1,
        "gla_onorm_g": 1.0 + 0.02 * nrm(ks[22], (N_GLA_LAYERS, GLA_DV_HEAD), f),
        "gla_w_o": nrm(ks[23], (N_GLA_LAYERS, GLA_DV, D), f) * GLA_DV ** -0.5,
        "final_norm_g": 1.0 + 0.02 * nrm(ks[24], (D,), f),
    }


def _fwd_reference(x, c, ctx, c_ctx, ada_w, ada_b, norm_mix_g, norm_ffn_g, ffn_w_up, ffn_conv_w,
              ffn_conv_b, ffn_w_down, attn_w_qkv, attn_sink, attn_w_o, gla_w_in, gla_gf_w1,
              gla_gf_w2, gla_gf_b, gla_gb_w1, gla_gb_w2, gla_gb_b, gla_onorm_g, gla_w_o,
              final_norm_g):
    n_lat = x.shape[1]
    rope = axial_rope_tables(n_lat)
    sc = jax.nn.silu(c)[:, None, :]
    scc = jax.nn.silu(c_ctx)[None, None, :]
    xc = ctx
    for i in range(DEPTH):
        last = i == DEPTH - 1
        mod_l = jnp.split(sc @ ada_w[i] + ada_b[i], N_MOD, axis=-1)
        mod_c = jnp.split(scc @ ada_w[i] + ada_b[i], N_MOD, axis=-1)
        hl = modulate(x, norm_mix_g[i], mod_l[0], mod_l[1])
        hc = modulate(xc, norm_mix_g[i], mod_c[0], mod_c[1])
        j = i // N_MIXERS
        if i % N_MIXERS == 0:
            oc, ol = attn_mixer(hc, hl, attn_w_qkv[j], attn_sink[j], attn_w_o[j], rope, not last)
        else:
            oc, ol = gla_mixer(hc, hl, gla_w_in[j], gla_gf_w1[j], gla_gf_w2[j], gla_gf_b[j],
                               gla_gb_w1[j], gla_gb_w2[j], gla_gb_b[j], gla_onorm_g[j],
                               gla_w_o[j], not last)
        x = x + mod_l[2] * ol
        x = x + mod_l[5] * conv_ffn(modulate(x, norm_ffn_g[i], mod_l[3], mod_l[4]),
                                    ffn_w_up[i], ffn_conv_w[i], ffn_conv_b[i], ffn_w_down[i])
        if not last:
            xc = xc + mod_c[2] * oc
            xc = xc + mod_c[5] * conv_ffn(modulate(xc, norm_ffn_g[i], mod_c[3], mod_c[4]),
                                          ffn_w_up[i], ffn_conv_w[i], ffn_conv_b[i], ffn_w_down[i])
    return rmsnorm(x, final_norm_g)


import jax as _jax
import jax.numpy as _jnp

TWIN_FORMAT = 'train_step'
FWD_PARAMS = ['x', 'c', 'ctx', 'c_ctx', 'ada_w', 'ada_b', 'norm_mix_g', 'norm_ffn_g', 'ffn_w_up', 'ffn_conv_w', 'ffn_conv_b', 'ffn_w_down', 'attn_w_qkv', 'attn_sink', 'attn_w_o', 'gla_w_in', 'gla_gf_w1', 'gla_gf_w2', 'gla_gf_b', 'gla_gb_w1', 'gla_gb_w2', 'gla_gb_b', 'gla_onorm_g', 'gla_w_o', 'final_norm_g']
TWIN_WEIGHTS = ['c_ctx', 'ada_w', 'ada_b', 'norm_mix_g', 'norm_ffn_g', 'ffn_w_up', 'ffn_conv_w', 'ffn_conv_b', 'ffn_w_down', 'attn_w_qkv', 'attn_sink', 'attn_w_o', 'gla_w_in', 'gla_gf_w1', 'gla_gf_w2', 'gla_gf_b', 'gla_gb_w1', 'gla_gb_w2', 'gla_gb_b', 'gla_onorm_g', 'gla_w_o', 'final_norm_g']
TWIN_DIFF_INPUT = 'x'
TWIN_INPUTS = ['x', 'c', 'ctx', 'c_ctx', 'ada_w', 'ada_b', 'norm_mix_g', 'norm_ffn_g', 'ffn_w_up', 'ffn_conv_w', 'ffn_conv_b', 'ffn_w_down', 'attn_w_qkv', 'attn_sink', 'attn_w_o', 'gla_w_in', 'gla_gf_w1', 'gla_gf_w2', 'gla_gf_b', 'gla_gb_w1', 'gla_gb_w2', 'gla_gb_b', 'gla_onorm_g', 'gla_w_o', 'final_norm_g', 'loss_target', 'm_c_ctx', 'm_ada_w', 'm_ada_b', 'm_norm_mix_g', 'm_norm_ffn_g', 'm_ffn_w_up', 'm_ffn_conv_w', 'm_ffn_conv_b', 'm_ffn_w_down', 'm_attn_w_qkv', 'm_attn_sink', 'm_attn_w_o', 'm_gla_w_in', 'm_gla_gf_w1', 'm_gla_gf_w2', 'm_gla_gf_b', 'm_gla_gb_w1', 'm_gla_gb_w2', 'm_gla_gb_b', 'm_gla_onorm_g', 'm_gla_w_o', 'm_final_norm_g', 'v_c_ctx', 'v_ada_w', 'v_ada_b', 'v_norm_mix_g', 'v_norm_ffn_g', 'v_ffn_w_up', 'v_ffn_conv_w', 'v_ffn_conv_b', 'v_ffn_w_down', 'v_attn_w_qkv', 'v_attn_sink', 'v_attn_w_o', 'v_gla_w_in', 'v_gla_gf_w1', 'v_gla_gf_w2', 'v_gla_gf_b', 'v_gla_gb_w1', 'v_gla_gb_w2', 'v_gla_gb_b', 'v_gla_onorm_g', 'v_gla_w_o', 'v_final_norm_g']
TWIN_OUTPUTS = ['loss', 'grad_x', 'grad_c_ctx', 'grad_ada_w', 'grad_ada_b', 'grad_norm_mix_g', 'grad_norm_ffn_g', 'grad_ffn_w_up', 'grad_ffn_conv_w', 'grad_ffn_conv_b', 'grad_ffn_w_down', 'grad_attn_w_qkv', 'grad_attn_sink', 'grad_attn_w_o', 'grad_gla_w_in', 'grad_gla_gf_w1', 'grad_gla_gf_w2', 'grad_gla_gf_b', 'grad_gla_gb_w1', 'grad_gla_gb_w2', 'grad_gla_gb_b', 'grad_gla_onorm_g', 'grad_gla_w_o', 'grad_final_norm_g', 'delta_c_ctx', 'delta_ada_w', 'delta_ada_b', 'delta_norm_mix_g', 'delta_norm_ffn_g', 'delta_ffn_w_up', 'delta_ffn_conv_w', 'delta_ffn_conv_b', 'delta_ffn_w_down', 'delta_attn_w_qkv', 'delta_attn_sink', 'delta_attn_w_o', 'delta_gla_w_in', 'delta_gla_gf_w1', 'delta_gla_gf_w2', 'delta_gla_gf_b', 'delta_gla_gb_w1', 'delta_gla_gb_w2', 'delta_gla_gb_b', 'delta_gla_onorm_g', 'delta_gla_w_o', 'delta_final_norm_g', 'new_m_c_ctx', 'new_m_ada_w', 'new_m_ada_b', 'new_m_norm_mix_g', 'new_m_norm_ffn_g', 'new_m_ffn_w_up', 'new_m_ffn_conv_w', 'new_m_ffn_conv_b', 'new_m_ffn_w_down', 'new_m_attn_w_qkv', 'new_m_attn_sink', 'new_m_attn_w_o', 'new_m_gla_w_in', 'new_m_gla_gf_w1', 'new_m_gla_gf_w2', 'new_m_gla_gf_b', 'new_m_gla_gb_w1', 'new_m_gla_gb_w2', 'new_m_gla_gb_b', 'new_m_gla_onorm_g', 'new_m_gla_w_o', 'new_m_final_norm_g', 'new_v_c_ctx', 'new_v_ada_w', 'new_v_ada_b', 'new_v_norm_mix_g', 'new_v_norm_ffn_g', 'new_v_ffn_w_up', 'new_v_ffn_conv_w', 'new_v_ffn_conv_b', 'new_v_ffn_w_down', 'new_v_attn_w_qkv', 'new_v_attn_sink', 'new_v_attn_w_o', 'new_v_gla_w_in', 'new_v_gla_gf_w1', 'new_v_gla_gf_w2', 'new_v_gla_gf_b', 'new_v_gla_gb_w1', 'new_v_gla_gb_w2', 'new_v_gla_gb_b', 'new_v_gla_onorm_g', 'new_v_gla_w_o', 'new_v_final_norm_g']
TWIN_LEAF_KINDS = {'loss': 'loss', 'grad_x': 'grad_x', 'grad_c_ctx': 'grad_w', 'grad_ada_w': 'grad_w', 'grad_ada_b': 'grad_w', 'grad_norm_mix_g': 'grad_w', 'grad_norm_ffn_g': 'grad_w', 'grad_ffn_w_up': 'grad_w', 'grad_ffn_conv_w': 'grad_w', 'grad_ffn_conv_b': 'grad_w', 'grad_ffn_w_down': 'grad_w', 'grad_attn_w_qkv': 'grad_w', 'grad_attn_sink': 'grad_w', 'grad_attn_w_o': 'grad_w', 'grad_gla_w_in': 'grad_w', 'grad_gla_gf_w1': 'grad_w', 'grad_gla_gf_w2': 'grad_w', 'grad_gla_gf_b': 'grad_w', 'grad_gla_gb_w1': 'grad_w', 'grad_gla_gb_w2': 'grad_w', 'grad_gla_gb_b': 'grad_w', 'grad_gla_onorm_g': 'grad_w', 'grad_gla_w_o': 'grad_w', 'grad_final_norm_g': 'grad_w', 'delta_c_ctx': 'delta_w', 'delta_ada_w': 'delta_w', 'delta_ada_b': 'delta_w', 'delta_norm_mix_g': 'delta_w', 'delta_norm_ffn_g': 'delta_w', 'delta_ffn_w_up': 'delta_w', 'delta_ffn_conv_w': 'delta_w', 'delta_ffn_conv_b': 'delta_w', 'delta_ffn_w_down': 'delta_w', 'delta_attn_w_qkv': 'delta_w', 'delta_attn_sink': 'delta_w', 'delta_attn_w_o': 'delta_w', 'delta_gla_w_in': 'delta_w', 'delta_gla_gf_w1': 'delta_w', 'delta_gla_gf_w2': 'delta_w', 'delta_gla_gf_b': 'delta_w', 'delta_gla_gb_w1': 'delta_w', 'delta_gla_gb_w2': 'delta_w', 'delta_gla_gb_b': 'delta_w', 'delta_gla_onorm_g': 'delta_w', 'delta_gla_w_o': 'delta_w', 'delta_final_norm_g': 'delta_w', 'new_m_c_ctx': 'new_m', 'new_m_ada_w': 'new_m', 'new_m_ada_b': 'new_m', 'new_m_norm_mix_g': 'new_m', 'new_m_norm_ffn_g': 'new_m', 'new_m_ffn_w_up': 'new_m', 'new_m_ffn_conv_w': 'new_m', 'new_m_ffn_conv_b': 'new_m', 'new_m_ffn_w_down': 'new_m', 'new_m_attn_w_qkv': 'new_m', 'new_m_attn_sink': 'new_m', 'new_m_attn_w_o': 'new_m', 'new_m_gla_w_in': 'new_m', 'new_m_gla_gf_w1': 'new_m', 'new_m_gla_gf_w2': 'new_m', 'new_m_gla_gf_b': 'new_m', 'new_m_gla_gb_w1': 'new_m', 'new_m_gla_gb_w2': 'new_m', 'new_m_gla_gb_b': 'new_m', 'new_m_gla_onorm_g': 'new_m', 'new_m_gla_w_o': 'new_m', 'new_m_final_norm_g': 'new_m', 'new_v_c_ctx': 'new_v', 'new_v_ada_w': 'new_v', 'new_v_ada_b': 'new_v', 'new_v_norm_mix_g': 'new_v', 'new_v_norm_ffn_g': 'new_v', 'new_v_ffn_w_up': 'new_v', 'new_v_ffn_conv_w': 'new_v', 'new_v_ffn_conv_b': 'new_v', 'new_v_ffn_w_down': 'new_v', 'new_v_attn_w_qkv': 'new_v', 'new_v_attn_sink': 'new_v', 'new_v_attn_w_o': 'new_v', 'new_v_gla_w_in': 'new_v', 'new_v_gla_gf_w1': 'new_v', 'new_v_gla_gf_w2': 'new_v', 'new_v_gla_gf_b': 'new_v', 'new_v_gla_gb_w1': 'new_v', 'new_v_gla_gb_w2': 'new_v', 'new_v_gla_gb_b': 'new_v', 'new_v_gla_onorm_g': 'new_v', 'new_v_gla_w_o': 'new_v', 'new_v_final_norm_g': 'new_v'}


def _forward(args):
    return _fwd_reference(*[args[k] for k in FWD_PARAMS])


def _output_shape():
    def fwd():
        inp = _fwd_setup_inputs(0)
        return _fwd_reference(*[inp[k] for k in FWD_PARAMS])
    out = _jax.eval_shape(fwd)
    return out.shape, out.dtype

N_MICROBATCH = 1
ADAM_LR = 0.001
ADAM_B1 = 0.9
ADAM_B2 = 0.999
ADAM_EPS = 1e-08
ADAM_WD = 0.01
ADAM_STEP = 10
PER_EXAMPLE_BATCH_AXIS = {'x': 0, 'c': 0, 'ctx': 0, 'loss_target': 0}
SHARED_INPUTS = []
_WEIGHT_DTYPES = {'c_ctx': _jnp.float32, 'ada_w': _jnp.float32, 'ada_b': _jnp.float32, 'norm_mix_g': _jnp.float32, 'norm_ffn_g': _jnp.float32, 'ffn_w_up': _jnp.float32, 'ffn_conv_w': _jnp.float32, 'ffn_conv_b': _jnp.float32, 'ffn_w_down': _jnp.float32, 'attn_w_qkv': _jnp.float32, 'attn_sink': _jnp.float32, 'attn_w_o': _jnp.float32, 'gla_w_in': _jnp.float32, 'gla_gf_w1': _jnp.float32, 'gla_gf_w2': _jnp.float32, 'gla_gf_b': _jnp.float32, 'gla_gb_w1': _jnp.float32, 'gla_gb_w2': _jnp.float32, 'gla_gb_b': _jnp.float32, 'gla_onorm_g': _jnp.float32, 'gla_w_o': _jnp.float32, 'final_norm_g': _jnp.float32}
MOMENT_SCALE = {'c_ctx': 9.967656e-03, 'ada_w': 2.468225e-02, 'ada_b': 4.214656e-02, 'norm_mix_g': 2.355249e-02, 'norm_ffn_g': 2.670828e-02, 'ffn_w_up': 1.188039e-02, 'ffn_conv_w': 1.195552e-02, 'ffn_conv_b': 1.092566e-02, 'ffn_w_down': 1.940337e-02, 'attn_w_qkv': 7.693096e-03, 'attn_sink': 1.336062e-04, 'attn_w_o': 7.859961e-03, 'gla_w_in': 2.085922e-02, 'gla_gf_w1': 2.547780e-02, 'gla_gf_w2': 2.938936e-03, 'gla_gf_b': 7.849546e-03, 'gla_gb_w1': 2.471738e-02, 'gla_gb_w2': 3.084617e-03, 'gla_gb_b': 8.112194e-03, 'gla_onorm_g': 3.371804e-02, 'gla_w_o': 1.722687e-02, 'final_norm_g': 1.602027e+01}


def _to_microbatches(a, axis):
    t = _jnp.moveaxis(a, axis, 0)
    t = t.reshape((N_MICROBATCH, t.shape[0] // N_MICROBATCH) + t.shape[1:])
    return _jnp.moveaxis(t, 1, axis + 1)


def setup_inputs(seed: int = 0) -> dict:
    inp = _fwd_setup_inputs(seed)
    key = _jax.random.fold_in(_jax.random.key(seed), 7919)
    shape, _ = _output_shape()
    out = dict(inp)
    out["loss_target"] = _jax.random.normal(_jax.random.fold_in(key, 0), shape, _jnp.float32)
    for i, name in enumerate(TWIN_WEIGHTS):
        w = inp[name].astype(_jnp.float32)
        if MOMENT_SCALE is None:
            s = _jnp.sqrt(_jnp.mean(_jnp.square(w)) + 1e-30)
        else:
            s = MOMENT_SCALE[name]
        km, kv = _jax.random.split(_jax.random.fold_in(key, i + 1))
        out[name] = w
        out["m_" + name] = s * _jax.random.normal(km, w.shape, _jnp.float32)
        out["v_" + name] = (s * s) * _jax.random.uniform(kv, w.shape, _jnp.float32, 0.5, 1.5)
    if N_MICROBATCH > 1:
        for name, axis in PER_EXAMPLE_BATCH_AXIS.items():
            out[name] = _to_microbatches(out[name], axis)
    return {'x': out['x'], 'c': out['c'], 'ctx': out['ctx'], 'c_ctx': out['c_ctx'], 'ada_w': out['ada_w'], 'ada_b': out['ada_b'], 'norm_mix_g': out['norm_mix_g'], 'norm_ffn_g': out['norm_ffn_g'], 'ffn_w_up': out['ffn_w_up'], 'ffn_conv_w': out['ffn_conv_w'], 'ffn_conv_b': out['ffn_conv_b'], 'ffn_w_down': out['ffn_w_down'], 'attn_w_qkv': out['attn_w_qkv'], 'attn_sink': out['attn_sink'], 'attn_w_o': out['attn_w_o'], 'gla_w_in': out['gla_w_in'], 'gla_gf_w1': out['gla_gf_w1'], 'gla_gf_w2': out['gla_gf_w2'], 'gla_gf_b': out['gla_gf_b'], 'gla_gb_w1': out['gla_gb_w1'], 'gla_gb_w2': out['gla_gb_w2'], 'gla_gb_b': out['gla_gb_b'], 'gla_onorm_g': out['gla_onorm_g'], 'gla_w_o': out['gla_w_o'], 'final_norm_g': out['final_norm_g'], 'loss_target': out['loss_target'], 'm_c_ctx': out['m_c_ctx'], 'm_ada_w': out['m_ada_w'], 'm_ada_b': out['m_ada_b'], 'm_norm_mix_g': out['m_norm_mix_g'], 'm_norm_ffn_g': out['m_norm_ffn_g'], 'm_ffn_w_up': out['m_ffn_w_up'], 'm_ffn_conv_w': out['m_ffn_conv_w'], 'm_ffn_conv_b': out['m_ffn_conv_b'], 'm_ffn_w_down': out['m_ffn_w_down'], 'm_attn_w_qkv': out['m_attn_w_qkv'], 'm_attn_sink': out['m_attn_sink'], 'm_attn_w_o': out['m_attn_w_o'], 'm_gla_w_in': out['m_gla_w_in'], 'm_gla_gf_w1': out['m_gla_gf_w1'], 'm_gla_gf_w2': out['m_gla_gf_w2'], 'm_gla_gf_b': out['m_gla_gf_b'], 'm_gla_gb_w1': out['m_gla_gb_w1'], 'm_gla_gb_w2': out['m_gla_gb_w2'], 'm_gla_gb_b': out['m_gla_gb_b'], 'm_gla_onorm_g': out['m_gla_onorm_g'], 'm_gla_w_o': out['m_gla_w_o'], 'm_final_norm_g': out['m_final_norm_g'], 'v_c_ctx': out['v_c_ctx'], 'v_ada_w': out['v_ada_w'], 'v_ada_b': out['v_ada_b'], 'v_norm_mix_g': out['v_norm_mix_g'], 'v_norm_ffn_g': out['v_norm_ffn_g'], 'v_ffn_w_up': out['v_ffn_w_up'], 'v_ffn_conv_w': out['v_ffn_conv_w'], 'v_ffn_conv_b': out['v_ffn_conv_b'], 'v_ffn_w_down': out['v_ffn_w_down'], 'v_attn_w_qkv': out['v_attn_w_qkv'], 'v_attn_sink': out['v_attn_sink'], 'v_attn_w_o': out['v_attn_w_o'], 'v_gla_w_in': out['v_gla_w_in'], 'v_gla_gf_w1': out['v_gla_gf_w1'], 'v_gla_gf_w2': out['v_gla_gf_w2'], 'v_gla_gf_b': out['v_gla_gf_b'], 'v_gla_gb_w1': out['v_gla_gb_w1'], 'v_gla_gb_w2': out['v_gla_gb_w2'], 'v_gla_gb_b': out['v_gla_gb_b'], 'v_gla_onorm_g': out['v_gla_onorm_g'], 'v_gla_w_o': out['v_gla_w_o'], 'v_final_norm_g': out['v_final_norm_g']}


def _loss(weights, diff, rest, loss_target):
    with _jax.named_scope("forward"):
        args = {**rest, TWIN_DIFF_INPUT: diff, **{k: w.astype(_WEIGHT_DTYPES[k]) for k, w in weights.items()}}
        y = _forward(args)
    with _jax.named_scope("loss_head"):
        err = _jnp.square(y.astype(_jnp.float32) - loss_target)
        return 0.5 * _jnp.sum(_jnp.mean(err, axis=-1)) if err.ndim else 0.5 * err


def _adamw(w, g, m, v):
    m = ADAM_B1 * m + (1.0 - ADAM_B1) * g
    v = ADAM_B2 * v + (1.0 - ADAM_B2) * _jnp.square(g)
    m_hat = m / (1.0 - ADAM_B1 ** ADAM_STEP)
    v_hat = v / (1.0 - ADAM_B2 ** ADAM_STEP)
    delta = -ADAM_LR * (m_hat / (_jnp.sqrt(v_hat) + ADAM_EPS) + ADAM_WD * w)
    return delta, m, v


def reference(x, c, ctx, c_ctx, ada_w, ada_b, norm_mix_g, norm_ffn_g, ffn_w_up, ffn_conv_w, ffn_conv_b, ffn_w_down, attn_w_qkv, attn_sink, attn_w_o, gla_w_in, gla_gf_w1, gla_gf_w2, gla_gf_b, gla_gb_w1, gla_gb_w2, gla_gb_b, gla_onorm_g, gla_w_o, final_norm_g, loss_target, m_c_ctx, m_ada_w, m_ada_b, m_norm_mix_g, m_norm_ffn_g, m_ffn_w_up, m_ffn_conv_w, m_ffn_conv_b, m_ffn_w_down, m_attn_w_qkv, m_attn_sink, m_attn_w_o, m_gla_w_in, m_gla_gf_w1, m_gla_gf_w2, m_gla_gf_b, m_gla_gb_w1, m_gla_gb_w2, m_gla_gb_b, m_gla_onorm_g, m_gla_w_o, m_final_norm_g, v_c_ctx, v_ada_w, v_ada_b, v_norm_mix_g, v_norm_ffn_g, v_ffn_w_up, v_ffn_conv_w, v_ffn_conv_b, v_ffn_w_down, v_attn_w_qkv, v_attn_sink, v_attn_w_o, v_gla_w_in, v_gla_gf_w1, v_gla_gf_w2, v_gla_gf_b, v_gla_gb_w1, v_gla_gb_w2, v_gla_gb_b, v_gla_onorm_g, v_gla_w_o, v_final_norm_g):
    given = dict(x=x, c=c, ctx=ctx, c_ctx=c_ctx, ada_w=ada_w, ada_b=ada_b, norm_mix_g=norm_mix_g, norm_ffn_g=norm_ffn_g, ffn_w_up=ffn_w_up, ffn_conv_w=ffn_conv_w, ffn_conv_b=ffn_conv_b, ffn_w_down=ffn_w_down, attn_w_qkv=attn_w_qkv, attn_sink=attn_sink, attn_w_o=attn_w_o, gla_w_in=gla_w_in, gla_gf_w1=gla_gf_w1, gla_gf_w2=gla_gf_w2, gla_gf_b=gla_gf_b, gla_gb_w1=gla_gb_w1, gla_gb_w2=gla_gb_w2, gla_gb_b=gla_gb_b, gla_onorm_g=gla_onorm_g, gla_w_o=gla_w_o, final_norm_g=final_norm_g, loss_target=loss_target, m_c_ctx=m_c_ctx, m_ada_w=m_ada_w, m_ada_b=m_ada_b, m_norm_mix_g=m_norm_mix_g, m_norm_ffn_g=m_norm_ffn_g, m_ffn_w_up=m_ffn_w_up, m_ffn_conv_w=m_ffn_conv_w, m_ffn_conv_b=m_ffn_conv_b, m_ffn_w_down=m_ffn_w_down, m_attn_w_qkv=m_attn_w_qkv, m_attn_sink=m_attn_sink, m_attn_w_o=m_attn_w_o, m_gla_w_in=m_gla_w_in, m_gla_gf_w1=m_gla_gf_w1, m_gla_gf_w2=m_gla_gf_w2, m_gla_gf_b=m_gla_gf_b, m_gla_gb_w1=m_gla_gb_w1, m_gla_gb_w2=m_gla_gb_w2, m_gla_gb_b=m_gla_gb_b, m_gla_onorm_g=m_gla_onorm_g, m_gla_w_o=m_gla_w_o, m_final_norm_g=m_final_norm_g, v_c_ctx=v_c_ctx, v_ada_w=v_ada_w, v_ada_b=v_ada_b, v_norm_mix_g=v_norm_mix_g, v_norm_ffn_g=v_norm_ffn_g, v_ffn_w_up=v_ffn_w_up, v_ffn_conv_w=v_ffn_conv_w, v_ffn_conv_b=v_ffn_conv_b, v_ffn_w_down=v_ffn_w_down, v_attn_w_qkv=v_attn_w_qkv, v_attn_sink=v_attn_sink, v_attn_w_o=v_attn_w_o, v_gla_w_in=v_gla_w_in, v_gla_gf_w1=v_gla_gf_w1, v_gla_gf_w2=v_gla_gf_w2, v_gla_gf_b=v_gla_gf_b, v_gla_gb_w1=v_gla_gb_w1, v_gla_gb_w2=v_gla_gb_w2, v_gla_gb_b=v_gla_gb_b, v_gla_onorm_g=v_gla_onorm_g, v_gla_w_o=v_gla_w_o, v_final_norm_g=v_final_norm_g)
    weights = {n: given[n] for n in TWIN_WEIGHTS}
    shared = {n: given[n] for n in SHARED_INPUTS}
    per_example = {n: given[n] for n in ['x', 'c', 'ctx']}
    grad_fn = _jax.value_and_grad(_loss, argnums=(0, 1))

    def one_microbatch(ex, loss_target):
        ex = dict(ex)
        diff = ex.pop(TWIN_DIFF_INPUT)
        return grad_fn(weights, diff, {**shared, **ex}, loss_target)

    if N_MICROBATCH == 1:
        loss, (grad_w, grad_x) = one_microbatch(per_example, given["loss_target"])
    else:
        def body(carry, xs):
            loss_sum, grad_sum = carry
            l_k, (gw_k, gx_k) = one_microbatch(xs[0], xs[1])
            with _jax.named_scope("update"):
                return (loss_sum + l_k, _jax.tree.map(_jnp.add, grad_sum, gw_k)), gx_k

        init = (_jnp.zeros((), _jnp.float32), _jax.tree.map(_jnp.zeros_like, weights))
        (loss, grad_w), grad_x = _jax.lax.scan(body, init, (per_example, given["loss_target"]))
    with _jax.named_scope("update"):
        delta_w, new_m, new_v = {}, {}, {}
        for n in TWIN_WEIGHTS:
            delta_w[n], new_m[n], new_v[n] = _adamw(weights[n], grad_w[n], given["m_" + n], given["v_" + n])
    return (loss, grad_x, *[grad_w[n] for n in TWIN_WEIGHTS], *[delta_w[n] for n in TWIN_WEIGHTS],
            *[new_m[n] for n in TWIN_WEIGHTS], *[new_v[n] for n in TWIN_WEIGHTS])
```

```python
import functools

import jax
import jax.numpy as jnp
from jax import lax
from jax.experimental import pallas as pl
from jax.experimental.pallas import tpu as pltpu

F32 = jnp.float32
CDT = jnp.bfloat16
VMEM_LIMIT = 56 * 1024 * 1024
MESH = pl.DeviceIdType.MESH

ATT_HEADS = 16
ATT_KV = 4
ATT_G = ATT_HEADS // ATT_KV
BLK = 128
GRID_W = 64
ROPE_BASE = 10000.0
GLA_H = 4
GATE_RANK = 16
GATE_NORM = 16.0
CHUNK = 64
EPS = 1e-6
N_MOD = 6
LR, B1, B2, AEPS, WD, STEP = 0.001, 0.9, 0.999, 1e-08, 0.01, 10
LANES = 1024
PACK_ROWS = 512

NN = (((1,), (0,)), ((), ()))
NT = (((1,), (1,)), ((), ()))
TN = (((0,), (0,)), ((), ()))


def _dg(a, b, dims):
    return lax.dot_general(a, b, dims, preferred_element_type=F32)


def _pick(dim, cands):
    for c in cands:
        if dim % c == 0:
            return c
    return dim


def _cparams(sem):
    return pltpu.CompilerParams(dimension_semantics=sem, vmem_limit_bytes=VMEM_LIMIT)


def _silu(x):
    return x * (1.0 / (1.0 + jnp.exp(-x)))


def _dsilu(x):
    s = 1.0 / (1.0 + jnp.exp(-x))
    return s * (1.0 + x * (1.0 - s))


def _mm(a, b, *, ta=False, tb=False, out_dtype=F32, name):
    if ta:
        K, M = a.shape
    else:
        M, K = a.shape
    if tb:
        N, K2 = b.shape
    else:
        K2, N = b.shape
    assert K == K2, (a.shape, b.shape, ta, tb)
    tm = _pick(M, (512, 256))
    tn = _pick(N, (1024, 896, 768, 512, 256, 128))
    tk = _pick(K, (2176, 2048, 1408, 1024, 896, 512))
    nk = K // tk
    dims = TN if ta else (NT if tb else NN)

    def body(a_ref, b_ref, o_ref, acc_ref):
        k = pl.program_id(2)

        @pl.when(k == 0)
        def _():
            acc_ref[...] = jnp.zeros_like(acc_ref)

        acc_ref[...] += _dg(a_ref[...], b_ref[...], dims)

        @pl.when(k == nk - 1)
        def _():
            o_ref[...] = acc_ref[...].astype(o_ref.dtype)

    a_spec = pl.BlockSpec((tk, tm), lambda n, m, k: (k, m)) if ta else pl.BlockSpec((tm, tk), lambda n, m, k: (m, k))
    b_spec = pl.BlockSpec((tn, tk), lambda n, m, k: (n, k)) if tb else pl.BlockSpec((tk, tn), lambda n, m, k: (k, n))
    return pl.pallas_call(
        body, name=name, grid=(N // tn, M // tm, nk),
        in_specs=[a_spec, b_spec],
        out_specs=pl.BlockSpec((tm, tn), lambda n, m, k: (m, n)),
        out_shape=jax.ShapeDtypeStruct((M, N), out_dtype),
        scratch_shapes=[pltpu.VMEM((tm, tn), F32)],
        compiler_params=_cparams(("parallel", "parallel", "arbitrary")),
    )(a, b)


def _seg_spec(D, first_lat):
    return pl.BlockSpec((1, 1, D), lambda i: (jnp.where(i >= first_lat, 1, 0), 0, 0))


def _norm_fwd(x, y, gate, g, shift, scale, *, M, name):
    T, D = x.shape
    tm = _pick(T, (256,))
    first_lat = M // tm
    has_res = y is not None
    seg = _seg_spec(D, first_lat)
    row = pl.BlockSpec((tm, D), lambda i: (i, 0))

    def body(*refs):
        if has_res:
            x_ref, y_ref, gate_ref, g_ref, sh_ref, sc_ref, xo_ref, h_ref = refs
            xv = x_ref[...] + gate_ref[0] * y_ref[...].astype(F32)
            xo_ref[...] = xv
        else:
            x_ref, g_ref, sh_ref, sc_ref, h_ref = refs
            xv = x_ref[...]
        rstd = lax.rsqrt(jnp.mean(xv * xv, axis=-1, keepdims=True) + EPS)
        h = xv * rstd * g_ref[...] * (1.0 + sc_ref[0]) + sh_ref[0]
        h_ref[...] = h.astype(h_ref.dtype)

    gspec = pl.BlockSpec((1, D), lambda i: (0, 0))
    if has_res:
        ins = [x, y, gate, g, shift, scale]
        in_specs = [row, row, seg, gspec, seg, seg]
        out_shape = (jax.ShapeDtypeStruct((T, D), F32), jax.ShapeDtypeStruct((T, D), CDT))
        out_specs = (row, row)
    else:
        ins = [x, g, shift, scale]
        in_specs = [row, gspec, seg, seg]
        out_shape = jax.ShapeDtypeStruct((T, D), CDT)
        out_specs = row
    out = pl.pallas_call(
        body, name=name, grid=(T // tm,), in_specs=in_specs, out_specs=out_specs, out_shape=out_shape,
        compiler_params=_cparams(("parallel",)),
    )(*ins)
    return out if has_res else (x, out)


def _norm_bwd(x, dh, dx_in, g, scale, y_prev, gate_prev, *, M, name):
    T, D = x.shape
    tm = _pick(T, (256,))
    first_lat = M // tm
    has_prev = y_prev is not None
    seg = _seg_spec(D, first_lat)
    row = pl.BlockSpec((tm, D), lambda i: (i, 0))
    gspec = pl.BlockSpec((1, D), lambda i: (0, 0))

    def body(*refs):
        if has_prev:
            x_ref, dh_ref, dxi_ref, g_ref, sc_ref, yp_ref, gp_ref, dx_ref, dy_ref, acc_ref = refs
        else:
            x_ref, dh_ref, dxi_ref, g_ref, sc_ref, dx_ref, acc_ref = refs
        i = pl.program_id(0)

        @pl.when(jnp.logical_or(i == 0, i == first_lat))
        def _():
            acc_ref[...] = jnp.zeros_like(acc_ref)

        xv = x_ref[...]
        rstd = lax.rsqrt(jnp.mean(xv * xv, axis=-1, keepdims=True) + EPS)
        xn = xv * rstd
        dh = dh_ref[...].astype(F32)
        dxn = dh * (g_ref[...] * (1.0 + sc_ref[0]))
        dx = dxi_ref[...] + rstd * (dxn - xn * jnp.mean(dxn * xn, axis=-1, keepdims=True))
        dx_ref[...] = dx
        acc_ref[0, 0:1, :] += jnp.sum(dh, axis=0, keepdims=True)
        acc_ref[0, 1:2, :] += jnp.sum(dh * xn, axis=0, keepdims=True)
        if has_prev:
            dy_ref[...] = (dx * gp_ref[0]).astype(dy_ref.dtype)
            acc_ref[0, 2:3, :] += jnp.sum(dx * yp_ref[...].astype(F32), axis=0, keepdims=True)

    acc_spec = pl.BlockSpec((1, 8, D), lambda i: (jnp.where(i >= first_lat, 1, 0), 0, 0))
    acc_shape = jax.ShapeDtypeStruct((2, 8, D), F32)
    if has_prev:
        ins = [x, dh, dx_in, g, scale, y_prev, gate_prev]
        in_specs = [row, row, row, gspec, seg, row, seg]
        out_shape = (jax.ShapeDtypeStruct((T, D), F32), jax.ShapeDtypeStruct((T, D), CDT), acc_shape)
        out_specs = (row, row, acc_spec)
    else:
        ins = [x, dh, dx_in, g, scale]
        in_specs = [row, row, row, gspec, seg]
        out_shape = (jax.ShapeDtypeStruct((T, D), F32), acc_shape)
        out_specs = (row, acc_spec)
    out = pl.pallas_call(
        body, name=name, grid=(T // tm,), in_specs=in_specs, out_specs=out_specs, out_shape=out_shape,
        compiler_params=_cparams(("arbitrary",)),
    )(*ins)
    if has_prev:
        return out
    return out[0], None, out[1]


def _final_loss(x, y_prev, gate_prev, tgt, g, *, M, name):
    T, D = x.shape
    tm = _pick(T, (256,))
    first_lat = M // tm
    nt = T // tm
    seg = _seg_spec(D, first_lat)
    row = pl.BlockSpec((tm, D), lambda i: (i, 0))
    gspec = pl.BlockSpec((1, D), lambda i: (0, 0))
    tspec = pl.BlockSpec((tm, D), lambda i: (jnp.maximum(i - first_lat, 0), 0))

    def body(x_ref, yp_ref, gp_ref, t_ref, g_ref, loss_ref, dx_ref, dy_ref, acc_ref):
        i = pl.program_id(0)

        @pl.when(jnp.logical_or(i == 0, i == first_lat))
        def _():
            acc_ref[...] = jnp.zeros_like(acc_ref)

        lat = jnp.where(i >= first_lat, 1.0, 0.0)
        yp = yp_ref[...].astype(F32)
        xv = x_ref[...] + gp_ref[0] * yp
        rstd = lax.rsqrt(jnp.mean(xv * xv, axis=-1, keepdims=True) + EPS)
        xn = xv * rstd
        diff = (xn * g_ref[...] - t_ref[...]) * lat
        part = 0.5 * jnp.sum(jnp.sum(diff * diff, axis=-1, keepdims=True), axis=0, keepdims=True) * (1.0 / D)
        loss_ref[0] = jnp.broadcast_to(part, (8, 128))
        dyv = diff * (1.0 / D)
        dxn = dyv * g_ref[...]
        dx = rstd * (dxn - xn * jnp.mean(dxn * xn, axis=-1, keepdims=True))
        dx_ref[...] = dx
        dy_ref[...] = (dx * gp_ref[0]).astype(dy_ref.dtype)
        acc_ref[0, 0:1, :] += jnp.sum(dyv * xn, axis=0, keepdims=True)
        acc_ref[0, 2:3, :] += jnp.sum(dx * yp, axis=0, keepdims=True)

    return pl.pallas_call(
        body, name=name, grid=(nt,),
        in_specs=[row, row, seg, tspec, gspec],
        out_specs=(pl.BlockSpec((1, 8, 128), lambda i: (i, 0, 0)), row, row,
                   pl.BlockSpec((1, 8, D), lambda i: (jnp.where(i >= first_lat, 1, 0), 0, 0))),
        out_shape=(jax.ShapeDtypeStruct((nt, 8, 128), F32), jax.ShapeDtypeStruct((T, D), F32),
                   jax.ShapeDtypeStruct((T, D), CDT), jax.ShapeDtypeStruct((2, 8, D), F32)),
        compiler_params=_cparams(("arbitrary",)),
    )(x, y_prev, gate_prev, tgt, g)


HALO = 16


def _taps(uc, prev16, next16, keep_prev, keep_next):
    tm = uc.shape[0]
    u = uc.astype(F32)
    rows = lax.broadcasted_iota(jnp.int32, u.shape, 0)
    pr = prev16[HALO - 1:HALO, :].astype(F32) * keep_prev
    nx = next16[0:1, :].astype(F32) * keep_next
    um = jnp.where(rows == 0, pr, pltpu.roll(u, 1, 0))
    up = jnp.where(rows == tm - 1, nx, pltpu.roll(u, tm - 1, 0))
    return um, u, up


def _conv3(uc, prev16, next16, w, bias, keep_prev, keep_next):
    um, u, up = _taps(uc, prev16, next16, keep_prev, keep_next)
    out = w[0:1, :] * um + w[1:2, :] * u + w[2:3, :] * up
    return out if bias is None else out + bias


def _conv_specs(tm, tc, T, col):
    hb = tm // HALO
    last = T // HALO - 1
    return [
        pl.BlockSpec((tm, tc), lambda j, i: (i, col(j))),
        pl.BlockSpec((HALO, tc), lambda j, i: (jnp.maximum(i * hb - 1, 0), col(j))),
        pl.BlockSpec((HALO, tc), lambda j, i: (jnp.minimum((i + 1) * hb, last), col(j))),
    ]


def _seg_keep(i, first_lat, nt):
    keep_prev = jnp.where(jnp.logical_or(i == 0, i == first_lat), 0.0, 1.0)
    keep_next = jnp.where(jnp.logical_or(i == first_lat - 1, i == nt - 1), 0.0, 1.0)
    return keep_prev, keep_next


def _conv_gate_fwd(u, cw, cb, *, M, name):
    T, F2 = u.shape
    Fh = F2 // 2
    tm = _pick(T, (256,))
    tc = _pick(Fh, (512,))
    nf = Fh // tc
    nt = T // tm
    first_lat = M // tm

    def body(ug, ugp, ugn, uv, uvp, uvn, wg, wv, bg, bv, o_ref):
        kp, kn = _seg_keep(pl.program_id(1), first_lat, nt)
        gc = _conv3(ug[...], ugp[...], ugn[...], wg[...], bg[...], kp, kn)
        vc = _conv3(uv[...], uvp[...], uvn[...], wv[...], bv[...], kp, kn)
        o_ref[...] = (_silu(gc) * vc).astype(o_ref.dtype)

    wspec = lambda off: pl.BlockSpec((3, tc), lambda j, i: (0, j + off))
    bspec = lambda off: pl.BlockSpec((1, tc), lambda j, i: (0, j + off))
    return pl.pallas_call(
        body, name=name, grid=(nf, nt),
        in_specs=_conv_specs(tm, tc, T, lambda j: j) + _conv_specs(tm, tc, T, lambda j: j + nf)
        + [wspec(0), wspec(nf), bspec(0), bspec(nf)],
        out_specs=pl.BlockSpec((tm, tc), lambda j, i: (i, j)),
        out_shape=jax.ShapeDtypeStruct((T, Fh), CDT),
        compiler_params=_cparams(("parallel", "parallel")),
    )(u, u, u, u, u, u, cw, cw, cb, cb)


def _conv_gate_bwd(u, dact, cw, cb, *, M, name):
    T, F2 = u.shape
    Fh = F2 // 2
    tm = _pick(T, (256,))
    tc = _pick(Fh, (512,))
    nf = Fh // tc
    nt = T // tm
    first_lat = M // tm

    def body(ut, utp, utn, uo, uop, uon, da, wt, wo, bt, bo, d_ref, acc_ref):
        j = pl.program_id(0)
        i = pl.program_id(1)

        @pl.when(i == 0)
        def _():
            acc_ref[...] = jnp.zeros_like(acc_ref)

        kp, kn = _seg_keep(i, first_lat, nt)
        u_m, u_c, u_p = _taps(ut[...], utp[...], utn[...], kp, kn)
        w = wt[...]
        ct = w[0:1, :] * u_m + w[1:2, :] * u_c + w[2:3, :] * u_p + bt[...]
        co = _conv3(uo[...], uop[...], uon[...], wo[...], bo[...], kp, kn)
        dav = da[...].astype(F32)
        is_gate = j < nf
        d = jnp.where(is_gate, dav * co * _dsilu(ct), dav * _silu(co))
        d_ref[...] = d.astype(d_ref.dtype)
        acc_ref[0:1, :] += jnp.sum(d * u_m, axis=0, keepdims=True)
        acc_ref[1:2, :] += jnp.sum(d * u_c, axis=0, keepdims=True)
        acc_ref[2:3, :] += jnp.sum(d * u_p, axis=0, keepdims=True)
        acc_ref[3:4, :] += jnp.sum(d, axis=0, keepdims=True)

    part = lambda j: (j + nf) % (2 * nf)
    wspec = lambda col: pl.BlockSpec((3, tc), lambda j, i: (0, col(j)))
    bspec = lambda col: pl.BlockSpec((1, tc), lambda j, i: (0, col(j)))
    return pl.pallas_call(
        body, name=name, grid=(2 * nf, nt),
        in_specs=_conv_specs(tm, tc, T, lambda j: j) + _conv_specs(tm, tc, T, part)
        + [pl.BlockSpec((tm, tc), lambda j, i: (i, j % nf)), wspec(lambda j: j), wspec(part), bspec(lambda j: j), bspec(part)],
        out_specs=(pl.BlockSpec((tm, tc), lambda j, i: (i, j)), pl.BlockSpec((8, tc), lambda j, i: (0, j))),
        out_shape=(jax.ShapeDtypeStruct((T, F2), CDT), jax.ShapeDtypeStruct((8, F2), F32)),
        compiler_params=_cparams(("parallel", "arbitrary")),
    )(u, u, u, u, u, u, dact, cw, cw, cb, cb)


def _conv_t(d, cw_flipped, *, M, name):
    T, F2 = d.shape
    tm = _pick(T, (256,))
    tc = _pick(F2, (512,))
    nt = T // tm
    first_lat = M // tm

    def body(dc, dp, dn, w, o_ref):
        kp, kn = _seg_keep(pl.program_id(1), first_lat, nt)
        o_ref[...] = _conv3(dc[...], dp[...], dn[...], w[...], None, kp, kn).astype(o_ref.dtype)

    return pl.pallas_call(
        body, name=name, grid=(F2 // tc, nt),
        in_specs=_conv_specs(tm, tc, T, lambda j: j) + [pl.BlockSpec((3, tc), lambda j, i: (0, j))],
        out_specs=pl.BlockSpec((tm, tc), lambda j, i: (i, j)),
        out_shape=jax.ShapeDtypeStruct((T, F2), CDT),
        compiler_params=_cparams(("parallel", "parallel")),
    )(d, d, d, cw_flipped)


def _rope_tables(N, M, HD):
    ax = HD // 2
    pos = jnp.arange(N, dtype=jnp.int32)
    row = (pos // GRID_W).astype(F32)
    col = (pos % GRID_W).astype(F32)
    inv = ROPE_BASE ** (-jnp.arange(0, ax, 2, dtype=F32) / ax)
    ar = row[:, None] * inv[None, :]
    ac = col[:, None] * inv[None, :]
    cos = jnp.concatenate([jnp.cos(ar), jnp.cos(ar), jnp.cos(ac), jnp.cos(ac)], axis=1)
    sin = jnp.concatenate([-jnp.sin(ar), jnp.sin(ar), -jnp.sin(ac), jnp.sin(ac)], axis=1)
    cos = jnp.concatenate([jnp.ones((M, HD), F32), cos], axis=0)
    sin = jnp.concatenate([jnp.zeros((M, HD), F32), sin], axis=0)
    return cos, sin


def _pair_swap(x, nf):
    w = x.shape[1]
    lane = lax.broadcasted_iota(jnp.int32, x.shape, 1)
    first = (lane % (2 * nf)) < nf
    return jnp.where(first, pltpu.roll(x, w - nf, 1), pltpu.roll(x, nf, 1))


def _rope_fwd(qkv, cos, sin, *, QW, KW, HD, name):
    T = qkv.shape[0]
    tm = _pick(T, (256,))
    nf = HD // 4

    def body(qkv_ref, c_ref, s_ref, q_ref, k_ref, v_ref):
        c = c_ref[...]
        s = s_ref[...]
        for ref, off, w in ((q_ref, 0, QW), (k_ref, QW, KW)):
            xv = qkv_ref[:, off:off + w]
            ct = jnp.tile(c, (1, w // HD))
            st = jnp.tile(s, (1, w // HD))
            ref[...] = (xv * ct + _pair_swap(xv, nf) * st).astype(ref.dtype)
        v_ref[...] = qkv_ref[:, QW + KW:QW + 2 * KW].astype(v_ref.dtype)

    tspec = pl.BlockSpec((tm, HD), lambda i: (i, 0))
    return pl.pallas_call(
        body, name=name, grid=(T // tm,),
        in_specs=[pl.BlockSpec((tm, QW + 2 * KW), lambda i: (i, 0)), tspec, tspec],
        out_specs=(pl.BlockSpec((tm, QW), lambda i: (i, 0)), pl.BlockSpec((tm, KW), lambda i: (i, 0)),
                   pl.BlockSpec((tm, KW), lambda i: (i, 0))),
        out_shape=(jax.ShapeDtypeStruct((T, QW), CDT), jax.ShapeDtypeStruct((T, KW), CDT),
                   jax.ShapeDtypeStruct((T, KW), CDT)),
        compiler_params=_cparams(("parallel",)),
    )(qkv, cos, sin)


def _rope_bwd(dq, dk, dv, cos, sin, *, HD, name):
    T, QW = dq.shape
    KW = dk.shape[1]
    tm = _pick(T, (256,))
    nf = HD // 4

    def body(dq_ref, dk_ref, dv_ref, c_ref, s_ref, o_ref):
        c = c_ref[...]
        s = s_ref[...]
        for ref, off, w in ((dq_ref, 0, QW), (dk_ref, QW, KW)):
            g = ref[...].astype(F32)
            ct = jnp.tile(c, (1, w // HD))
            st = jnp.tile(s, (1, w // HD))
            o_ref[:, off:off + w] = (g * ct + _pair_swap(g * st, nf)).astype(o_ref.dtype)
        o_ref[:, QW + KW:QW + 2 * KW] = dv_ref[...].astype(o_ref.dtype)

    tspec = pl.BlockSpec((tm, HD), lambda i: (i, 0))
    return pl.pallas_call(
        body, name=name, grid=(T // tm,),
        in_specs=[pl.BlockSpec((tm, QW), lambda i: (i, 0)), pl.BlockSpec((tm, KW), lambda i: (i, 0)),
                  pl.BlockSpec((tm, KW), lambda i: (i, 0)), tspec, tspec],
        out_specs=pl.BlockSpec((tm, QW + 2 * KW), lambda i: (i, 0)),
        out_shape=jax.ShapeDtypeStruct((T, QW + 2 * KW), CDT),
        compiler_params=_cparams(("parallel",)),
    )(dq, dk, dv, cos, sin)


def _attn_scores(q_ref, kc_ref, kp_ref, kn_ref, kx_ref, sink_ref, i, *, M, HD, nblk, nbc):
    qs = jnp.concatenate([q_ref[:, g * HD:(g + 1) * HD] for g in range(ATT_G)], axis=0)
    kall = jnp.concatenate([kc_ref[...], kp_ref[...], kn_ref[...], kx_ref[...]], axis=0)
    s = _dg(qs, kall, NT) * (HD ** -0.5)
    shape = s.shape
    r = lax.broadcasted_iota(jnp.int32, shape, 0) % BLK
    c = lax.broadcasted_iota(jnp.int32, shape, 1) - M
    far = 4 * BLK
    lat_off = jnp.where(i >= nbc, 0, far)
    lo = jnp.maximum(r, jnp.where(i - 1 >= nbc, 0, BLK)) + lat_off
    hi = jnp.minimum(r + 2 * BLK, jnp.where(i + 1 < nblk, 3 * BLK - 1, 2 * BLK - 1))
    allowed = jnp.logical_or(c < 0, jnp.logical_and(c >= lo, c <= hi))
    s = jnp.where(allowed, s, -1e30)
    sink = sink_ref[0]
    m = jnp.maximum(jnp.max(s, axis=-1, keepdims=True), sink)
    e = jnp.exp(s - m)
    es = jnp.exp(sink - m)
    inv = 1.0 / (jnp.sum(e, axis=-1, keepdims=True) + es)
    return qs, kall, e * inv, es * inv


def _attn_specs(M, HD, nblk):
    kv_blk = lambda f: pl.BlockSpec((BLK, HD), lambda h, i: (f(i), h))
    ctx = pl.BlockSpec((M, HD), lambda h, i: (0, h))
    win = [kv_blk(lambda i: jnp.maximum(i - 1, 0)), kv_blk(lambda i: i), kv_blk(lambda i: jnp.minimum(i + 1, nblk - 1))]
    qspec = pl.BlockSpec((BLK, ATT_G * HD), lambda h, i: (i, h))
    sspec = pl.BlockSpec((1, ATT_G * BLK, 1), lambda h, i: (h, 0, 0))
    return qspec, [ctx] + win, sspec


def _attn_fwd(q, k, v, sink_col, *, M, name):
    T, QW = q.shape
    HD = QW // ATT_HEADS
    nblk = T // BLK
    nbc = M // BLK

    def body(q_ref, kc, kp, kn, kx, vc, vp, vn, vx, sink_ref, o_ref):
        i = pl.program_id(1)
        _, _, p, _ = _attn_scores(q_ref, kc, kp, kn, kx, sink_ref, i, M=M, HD=HD, nblk=nblk, nbc=nbc)
        vall = jnp.concatenate([vc[...], vp[...], vn[...], vx[...]], axis=0)
        o = _dg(p.astype(CDT), vall, NN)
        for g in range(ATT_G):
            o_ref[:, g * HD:(g + 1) * HD] = o[g * BLK:(g + 1) * BLK, :].astype(o_ref.dtype)

    qspec, kvs, sspec = _attn_specs(M, HD, nblk)
    return pl.pallas_call(
        body, name=name, grid=(ATT_KV, nblk),
        in_specs=[qspec] + kvs + kvs + [sspec],
        out_specs=qspec,
        out_shape=jax.ShapeDtypeStruct((T, QW), CDT),
        compiler_params=_cparams(("parallel", "parallel")),
    )(q, k, k, k, k, v, v, v, v, sink_col)


def _attn_bwd(q, k, v, sink_col, do, *, M, name):
    T, QW = q.shape
    HD = QW // ATT_HEADS
    KW = ATT_KV * HD
    nblk = T // BLK
    nbc = M // BLK

    def body(q_ref, kc, kp, kn, kx, vc, vp, vn, vx, sink_ref, do_ref, dq_ref, dkc_ref, dvc_ref, dkw_ref, dvw_ref, ds_ref):
        i = pl.program_id(1)

        @pl.when(i == 0)
        def _():
            dkc_ref[...] = jnp.zeros_like(dkc_ref)
            dvc_ref[...] = jnp.zeros_like(dvc_ref)
            ds_ref[...] = jnp.zeros_like(ds_ref)

        qs, kall, p, p_sink = _attn_scores(q_ref, kc, kp, kn, kx, sink_ref, i, M=M, HD=HD, nblk=nblk, nbc=nbc)
        vall = jnp.concatenate([vc[...], vp[...], vn[...], vx[...]], axis=0)
        dos = jnp.concatenate([do_ref[:, g * HD:(g + 1) * HD] for g in range(ATT_G)], axis=0)
        dp = _dg(dos, vall, NT)
        dsum = jnp.sum(p * dp, axis=-1, keepdims=True)
        dsc = (p * (dp - dsum) * (HD ** -0.5)).astype(CDT)
        dq = _dg(dsc, kall, NN)
        dkall = _dg(dsc, qs, TN)
        dvall = _dg(p.astype(CDT), dos, TN)
        for g in range(ATT_G):
            dq_ref[:, g * HD:(g + 1) * HD] = dq[g * BLK:(g + 1) * BLK, :].astype(dq_ref.dtype)
        dkc_ref[...] += dkall[0:M]
        dvc_ref[...] += dvall[0:M]
        dkw_ref[...] = dkall[M:]
        dvw_ref[...] = dvall[M:]
        ds_ref[0] += -(p_sink * dsum)

    qspec, kvs, sspec = _attn_specs(M, HD, nblk)
    ctx_out = pl.BlockSpec((M, HD), lambda h, i: (0, h))
    win_out = pl.BlockSpec((3 * BLK, HD), lambda h, i: (i, h))
    return pl.pallas_call(
        body, name=name, grid=(ATT_KV, nblk),
        in_specs=[qspec] + kvs + kvs + [sspec, qspec],
        out_specs=(qspec, ctx_out, ctx_out, win_out, win_out, sspec),
        out_shape=(jax.ShapeDtypeStruct((T, QW), CDT), jax.ShapeDtypeStruct((M, KW), F32), jax.ShapeDtypeStruct((M, KW), F32),
                   jax.ShapeDtypeStruct((nblk * 3 * BLK, KW), F32), jax.ShapeDtypeStruct((nblk * 3 * BLK, KW), F32),
                   jax.ShapeDtypeStruct((ATT_KV, ATT_G * BLK, 1), F32)),
        compiler_params=_cparams(("parallel", "arbitrary")),
    )(q, k, k, k, k, v, v, v, v, sink_col, do)


def _window_combine(part, *, nbc, name):
    rows, KW = part.shape
    nblk = rows // (3 * BLK)
    nbl = nblk - nbc

    def body(a_ref, b_ref, c_ref, o_ref):
        j = pl.program_id(0)
        o_ref[...] = (a_ref[...] * jnp.where(j + 1 < nbl, 1.0, 0.0) + b_ref[...]
                      + c_ref[...] * jnp.where(j >= 1, 1.0, 0.0))

    return pl.pallas_call(
        body, name=name, grid=(nbl,),
        in_specs=[pl.BlockSpec((BLK, KW), lambda j: (3 * jnp.minimum(nbc + j + 1, nblk - 1), 0)),
                  pl.BlockSpec((BLK, KW), lambda j: (3 * (nbc + j) + 1, 0)),
                  pl.BlockSpec((BLK, KW), lambda j: (3 * jnp.maximum(nbc + j - 1, 0) + 2, 0))],
        out_specs=pl.BlockSpec((BLK, KW), lambda j: (j, 0)),
        out_shape=jax.ShapeDtypeStruct((nbl * BLK, KW), F32),
        compiler_params=_cparams(("parallel",)),
    )(part, part, part)


def _split3(x):
    hi = x.astype(CDT)
    r1 = x - hi.astype(F32)
    mid = r1.astype(CDT)
    lo = (r1 - mid.astype(F32)).astype(CDT)
    return hi, mid, lo


def _tri_sum(tri, x, terms):
    parts = _split3(x)[:terms]
    out = _dg(tri, parts[0], NN)
    for p in parts[1:]:
        out = out + _dg(tri, p, NN)
    return out


def _gla_dims(D):
    dk = D // 2 // GLA_H
    dv = D // GLA_H
    return dk, dv


def _chunk_of(s, rev, ncc, ns):
    if not rev:
        return s
    return jnp.where(s < ncc, ncc - 1 - s, ns - 1 - (s - ncc))


def _gla_chunk(q_ref, k_ref, g_ref, rev, dk):
    C = q_ref.shape[0]
    r = lax.broadcasted_iota(jnp.int32, (C, C), 0)
    c = lax.broadcasted_iota(jnp.int32, (C, C), 1)
    causal = (r <= c) if rev else (r >= c)
    b = _tri_sum(causal.astype(CDT), g_ref[...], 3)
    B = b[0:1, :] if rev else b[C - 1:C, :]
    q = q_ref[...].astype(F32) * (dk ** -0.5)
    k = k_ref[...].astype(F32)
    return causal, b, B, q * jnp.exp(b), k * jnp.exp(-b), k * jnp.exp(B - b)


def _gla_scan_fwd(proj, g, *, rev, M, D, name):
    T = proj.shape[0]
    dk, dv = _gla_dims(D)
    C = CHUNK
    ns = T // C
    ncc = M // C
    koff = (D // 2) // dk
    voff = D // dv
    cm = lambda s: _chunk_of(s, rev, ncc, ns)

    def body(q_ref, k_ref, v_ref, g_ref, o_ref, st_ref, S):
        s = pl.program_id(1)

        @pl.when(s == 0)
        def _():
            S[...] = jnp.zeros_like(S)

        causal, b, B, qt, kt, kh = _gla_chunk(q_ref, k_ref, g_ref, rev, dk)
        v = v_ref[...]
        A = jnp.where(causal, _dg(qt.astype(CDT), kt.astype(CDT), NT), 0.0)
        Sin = S[...]
        st_ref[0] = Sin
        o_ref[...] = _dg(A.astype(CDT), v, NN) + _dg(qt.astype(CDT), Sin.astype(CDT), NT)
        S[...] = Sin * jnp.exp(B) + _dg(v, kh.astype(CDT), TN)

    return pl.pallas_call(
        body, name=name, grid=(GLA_H, ns),
        in_specs=[pl.BlockSpec((C, dk), lambda h, s: (cm(s), h)),
                  pl.BlockSpec((C, dk), lambda h, s: (cm(s), koff + h)),
                  pl.BlockSpec((C, dv), lambda h, s: (cm(s), voff + h)),
                  pl.BlockSpec((C, dk), lambda h, s: (cm(s), h))],
        out_specs=(pl.BlockSpec((C, dv), lambda h, s: (cm(s), h)),
                   pl.BlockSpec((1, dv, dk), lambda h, s: (h * ns + s, 0, 0))),
        out_shape=(jax.ShapeDtypeStruct((T, D), F32), jax.ShapeDtypeStruct((GLA_H * ns, dv, dk), F32)),
        scratch_shapes=[pltpu.VMEM((dv, dk), F32)],
        compiler_params=_cparams(("parallel", "arbitrary")),
    )(proj, proj, proj, g)


def _gla_scan_bwd(proj, g, st, do, *, rev, M, D, name):
    T = proj.shape[0]
    dk, dv = _gla_dims(D)
    C = CHUNK
    ns = T // C
    ncc = M // C
    koff = (D // 2) // dk
    voff = D // dv
    cm = lambda j: _chunk_of(ns - 1 - j, rev, ncc, ns)

    def body(q_ref, k_ref, v_ref, g_ref, st_ref, do_ref, dq_ref, dk_ref, dv_ref, dg_ref, dS):
        j = pl.program_id(1)

        @pl.when(j == 0)
        def _():
            dS[...] = jnp.zeros_like(dS)

        causal, b, B, qt, kt, kh = _gla_chunk(q_ref, k_ref, g_ref, rev, dk)
        v = v_ref[...]
        dov = do_ref[...]
        ST = st_ref[0]
        dSo = dS[...]
        qtb, ktb, khb = qt.astype(CDT), kt.astype(CDT), kh.astype(CDT)
        dSb = dSo.astype(CDT)
        A = jnp.where(causal, _dg(qtb, ktb, NT), 0.0).astype(CDT)
        dA = jnp.where(causal, _dg(dov, v, NT), 0.0).astype(CDT)
        dqt = _dg(dA, ktb, NN) + _dg(dov, ST.astype(CDT), NN)
        dkt = _dg(dA, qtb, TN)
        dvv = _dg(A, dov, TN) + _dg(khb, dSb, NT)
        dkh = _dg(v, dSb, NN)
        eB = jnp.exp(B)
        dB = eB * jnp.sum(ST * dSo, axis=0, keepdims=True) + jnp.sum(dkh * kh, axis=0, keepdims=True)
        rows = lax.broadcasted_iota(jnp.int32, (C, dk), 0)
        db = dqt * qt - dkt * kt - dkh * kh + jnp.where(rows == (0 if rev else C - 1), dB, 0.0)
        anti = jnp.logical_not(causal) | (lax.broadcasted_iota(jnp.int32, (C, C), 0) == lax.broadcasted_iota(jnp.int32, (C, C), 1))
        dg_ref[...] = _tri_sum(anti.astype(CDT), db, 2)
        dq_ref[...] = dqt * jnp.exp(b) * (dk ** -0.5)
        dk_ref[...] = dkt * jnp.exp(-b) + dkh * jnp.exp(B - b)
        dv_ref[...] = dvv
        dS[...] = dSo * eB + _dg(dov, qtb, TN)

    return pl.pallas_call(
        body, name=name, grid=(GLA_H, ns),
        in_specs=[pl.BlockSpec((C, dk), lambda h, j: (cm(j), h)),
                  pl.BlockSpec((C, dk), lambda h, j: (cm(j), koff + h)),
                  pl.BlockSpec((C, dv), lambda h, j: (cm(j), voff + h)),
                  pl.BlockSpec((C, dk), lambda h, j: (cm(j), h)),
                  pl.BlockSpec((1, dv, dk), lambda h, j: (h * ns + ns - 1 - j, 0, 0)),
                  pl.BlockSpec((C, dv), lambda h, j: (cm(j), h))],
        out_specs=(pl.BlockSpec((C, dk), lambda h, j: (cm(j), h)), pl.BlockSpec((C, dk), lambda h, j: (cm(j), h)),
                   pl.BlockSpec((C, dv), lambda h, j: (cm(j), h)), pl.BlockSpec((C, dk), lambda h, j: (cm(j), h))),
        out_shape=(jax.ShapeDtypeStruct((T, D // 2), F32), jax.ShapeDtypeStruct((T, D // 2), F32),
                   jax.ShapeDtypeStruct((T, D), F32), jax.ShapeDtypeStruct((T, D // 2), F32)),
        scratch_shapes=[pltpu.VMEM((dv, dk), F32)],
        compiler_params=_cparams(("parallel", "arbitrary")),
    )(proj, proj, proj, g, st, do)


def _log_sigmoid_parts(z):
    t = jnp.exp(-jnp.abs(z))
    return jnp.minimum(z, 0.0) - jnp.log(1.0 + t), jnp.where(z >= 0, t / (1.0 + t), 1.0 / (1.0 + t))


def _gla_gate_fwd(proj, w2, bias, *, D, name):
    T, PW = proj.shape
    tm = _pick(T, (256,))
    lrb = (PW - 128) // 128
    Dh = D // 2

    def body(lr_ref, w_ref, b_ref, gf_ref, gb_ref):
        z = _dg(lr_ref[...], w_ref[...], NN) + b_ref[...]
        g, _ = _log_sigmoid_parts(z)
        g = g * (1.0 / GATE_NORM)
        gf_ref[...] = g[:, 0:Dh]
        gb_ref[...] = g[:, Dh:D]

    half = pl.BlockSpec((tm, Dh), lambda i: (i, 0))
    return pl.pallas_call(
        body, name=name, grid=(T // tm,),
        in_specs=[pl.BlockSpec((tm, 128), lambda i: (i, lrb)), pl.BlockSpec((128, D), lambda i: (0, 0)),
                  pl.BlockSpec((1, D), lambda i: (0, 0))],
        out_specs=(half, half),
        out_shape=(jax.ShapeDtypeStruct((T, Dh), F32), jax.ShapeDtypeStruct((T, Dh), F32)),
        compiler_params=_cparams(("parallel",)),
    )(proj, w2, bias)


def _gla_proj_bwd(proj, w2, bias, dqf, dkf, dvf, dgf, dqb, dkb, dvb, dgb, dr, *, D, name):
    T, PW = proj.shape
    tm = _pick(T, (256,))
    lrb = (PW - 128) // 128
    Dh = D // 2

    def body(lr_ref, w_ref, b_ref, dqf_r, dkf_r, dvf_r, dgf_r, dqb_r, dkb_r, dvb_r, dgb_r, dr_ref, dp_ref, dw_ref, db_ref):
        @pl.when(pl.program_id(0) == 0)
        def _():
            dw_ref[...] = jnp.zeros_like(dw_ref)
            db_ref[...] = jnp.zeros_like(db_ref)

        lr = lr_ref[...]
        z = _dg(lr, w_ref[...], NN) + b_ref[...]
        _, sneg = _log_sigmoid_parts(z)
        dz = jnp.concatenate([dgf_r[...], dgb_r[...]], axis=1) * sneg * (1.0 / GATE_NORM)
        dzb = dz.astype(CDT)
        dp_ref[:, 0:Dh] = (dqf_r[...] + dqb_r[...]).astype(dp_ref.dtype)
        dp_ref[:, Dh:D] = (dkf_r[...] + dkb_r[...]).astype(dp_ref.dtype)
        dp_ref[:, D:2 * D] = (dvf_r[...] + dvb_r[...]).astype(dp_ref.dtype)
        dp_ref[:, 2 * D:3 * D] = dr_ref[...]
        dp_ref[:, 3 * D:3 * D + 128] = _dg(dzb, w_ref[...], NT).astype(dp_ref.dtype)
        dw_ref[...] += _dg(lr, dzb, TN)
        db_ref[0:1, :] += jnp.sum(dz, axis=0, keepdims=True)

    half = pl.BlockSpec((tm, Dh), lambda i: (i, 0))
    full = pl.BlockSpec((tm, D), lambda i: (i, 0))
    return pl.pallas_call(
        body, name=name, grid=(T // tm,),
        in_specs=[pl.BlockSpec((tm, 128), lambda i: (i, lrb)), pl.BlockSpec((128, D), lambda i: (0, 0)),
                  pl.BlockSpec((1, D), lambda i: (0, 0)), half, half, full, half, half, half, full, half, full],
        out_specs=(pl.BlockSpec((tm, PW), lambda i: (i, 0)), pl.BlockSpec((128, D), lambda i: (0, 0)),
                   pl.BlockSpec((8, D), lambda i: (0, 0))),
        out_shape=(jax.ShapeDtypeStruct((T, PW), CDT), jax.ShapeDtypeStruct((128, D), F32), jax.ShapeDtypeStruct((8, D), F32)),
        compiler_params=_cparams(("arbitrary",)),
    )(proj, w2, bias, dqf, dkf, dvf, dgf, dqb, dkb, dvb, dgb, dr)


def _gla_out_fwd(of, ob, proj, gn, *, D, name):
    T = of.shape[0]
    tm = _pick(T, (256,))
    dv = D // GLA_H

    def body(of_ref, ob_ref, r_ref, g_ref, y_ref):
        for h in range(GLA_H):
            sl = slice(h * dv, (h + 1) * dv)
            o = of_ref[:, sl] + ob_ref[:, sl]
            rstd = lax.rsqrt(jnp.mean(o * o, axis=-1, keepdims=True) + EPS)
            y_ref[:, sl] = (o * rstd * g_ref[...] * _silu(r_ref[:, sl].astype(F32))).astype(y_ref.dtype)

    full = pl.BlockSpec((tm, D), lambda i: (i, 0))
    return pl.pallas_call(
        body, name=name, grid=(T // tm,),
        in_specs=[full, full, pl.BlockSpec((tm, D), lambda i: (i, 2)), pl.BlockSpec((1, dv), lambda i: (0, 0))],
        out_specs=full, out_shape=jax.ShapeDtypeStruct((T, D), CDT),
        compiler_params=_cparams(("parallel",)),
    )(of, ob, proj, gn)


def _gla_out_bwd(of, ob, proj, gn, dy, *, D, name):
    T = of.shape[0]
    tm = _pick(T, (256,))
    dv = D // GLA_H

    def body(of_ref, ob_ref, r_ref, g_ref, dy_ref, do_ref, dr_ref, dg_ref):
        @pl.when(pl.program_id(0) == 0)
        def _():
            dg_ref[...] = jnp.zeros_like(dg_ref)

        gv = g_ref[...]
        for h in range(GLA_H):
            sl = slice(h * dv, (h + 1) * dv)
            o = of_ref[:, sl] + ob_ref[:, sl]
            rstd = lax.rsqrt(jnp.mean(o * o, axis=-1, keepdims=True) + EPS)
            oh = o * rstd
            r = r_ref[:, sl].astype(F32)
            dyv = dy_ref[:, sl].astype(F32)
            don = dyv * _silu(r)
            dr_ref[:, sl] = (dyv * oh * gv * _dsilu(r)).astype(dr_ref.dtype)
            dg_ref[0:1, :] += jnp.sum(don * oh, axis=0, keepdims=True)
            dn = don * gv
            do_ref[:, sl] = (rstd * (dn - oh * jnp.mean(dn * oh, axis=-1, keepdims=True))).astype(do_ref.dtype)

    full = pl.BlockSpec((tm, D), lambda i: (i, 0))
    return pl.pallas_call(
        body, name=name, grid=(T // tm,),
        in_specs=[full, full, pl.BlockSpec((tm, D), lambda i: (i, 2)), pl.BlockSpec((1, dv), lambda i: (0, 0)), full],
        out_specs=(full, full, pl.BlockSpec((8, dv), lambda i: (0, 0))),
        out_shape=(jax.ShapeDtypeStruct((T, D), CDT), jax.ShapeDtypeStruct((T, D), CDT), jax.ShapeDtypeStruct((8, dv), F32)),
        compiler_params=_cparams(("arbitrary",)),
    )(of, ob, proj, gn, dy)


def _adamw(w, g, m, v, *, name):
    R, Cc = w.shape
    tr = R
    for cand in (512, 256, 128, 64, 32, 16, 8):
        if R % cand == 0 and cand * Cc * 4 <= 2 * 1024 * 1024:
            tr = cand
            break

    def body(w_ref, g_ref, m_ref, v_ref, d_ref, mo_ref, vo_ref):
        gv = g_ref[...]
        mn = B1 * m_ref[...] + (1.0 - B1) * gv
        vn = B2 * v_ref[...] + (1.0 - B2) * (gv * gv)
        mh = mn / (1.0 - B1 ** STEP)
        vh = vn / (1.0 - B2 ** STEP)
        d_ref[...] = -LR * (mh / (jnp.sqrt(vh) + AEPS) + WD * w_ref[...])
        mo_ref[...] = mn
        vo_ref[...] = vn

    spec = pl.BlockSpec((tr, Cc), lambda i: (i, 0))
    sh = jax.ShapeDtypeStruct((R, Cc), F32)
    return pl.pallas_call(
        body, name=name, grid=(R // tr,), in_specs=[spec] * 4, out_specs=(spec,) * 3, out_shape=(sh,) * 3,
        compiler_params=_cparams(("parallel",)),
    )(w, g, m, v)


def _attn_layer_fwd(h, w, tabs, M, tag):
    cos, sin = tabs
    QW = w["w_o"].shape[0]
    HD = QW // ATT_HEADS
    KW = ATT_KV * HD
    qkv = _mm(h, w["w_qkv"], out_dtype=F32, name=f"{tag}_qkv")
    q, k, v = _rope_fwd(qkv, cos, sin, QW=QW, KW=KW, HD=HD, name=f"{tag}_rope")
    sink_col = jnp.repeat(w["sink"].astype(F32), BLK).reshape(ATT_KV, ATT_G * BLK, 1)
    o = _attn_fwd(q, k, v, sink_col, M=M, name=f"{tag}_attn")
    y = _mm(o, w["w_o"], out_dtype=CDT, name=f"{tag}_wo")
    return y, dict(h=h, q=q, k=k, v=v, o=o, sink_col=sink_col)


def _attn_layer_bwd(dy, sv, w, tabs, M, tag):
    cos, sin = tabs
    QW = w["w_o"].shape[0]
    HD = QW // ATT_HEADS
    do = _mm(dy, w["w_o"], tb=True, out_dtype=CDT, name=f"{tag}_dwo_x")
    g = {"w_o": _mm(sv["o"], dy, ta=True, out_dtype=CDT, name=f"{tag}_dwo_w")}
    dq, dkc, dvc, dkw, dvw, dsink = _attn_bwd(sv["q"], sv["k"], sv["v"], sv["sink_col"], do, M=M, name=f"{tag}_attn_bwd")
    nbc = M // BLK
    dk = jnp.concatenate([dkc, _window_combine(dkw, nbc=nbc, name=f"{tag}_dk_comb")], axis=0)
    dv = jnp.concatenate([dvc, _window_combine(dvw, nbc=nbc, name=f"{tag}_dv_comb")], axis=0)
    dqkv = _rope_bwd(dq, dk, dv, cos, sin, HD=HD, name=f"{tag}_rope_bwd")
    dh = _mm(dqkv, w["w_qkv"], tb=True, out_dtype=CDT, name=f"{tag}_dqkv_x")
    g["w_qkv"] = _mm(sv["h"], dqkv, ta=True, out_dtype=CDT, name=f"{tag}_dqkv_w")
    g["sink"] = jnp.sum(dsink.reshape(ATT_HEADS, BLK), axis=1)
    return dh, g


def _gla_layer_fwd(h, w, M, tag):
    D = h.shape[1]
    proj = _mm(h, w["w_in"], out_dtype=CDT, name=f"{tag}_in")
    gf, gb = _gla_gate_fwd(proj, w["w2"], w["gbias"], D=D, name=f"{tag}_gate")
    of, stf = _gla_scan_fwd(proj, gf, rev=False, M=M, D=D, name=f"{tag}_scan_f")
    ob, stb = _gla_scan_fwd(proj, gb, rev=True, M=M, D=D, name=f"{tag}_scan_b")
    yg = _gla_out_fwd(of, ob, proj, w["onorm"], D=D, name=f"{tag}_out")
    y = _mm(yg, w["w_o"], out_dtype=CDT, name=f"{tag}_wo")
    return y, dict(h=h, proj=proj, gf=gf, gb=gb, of=of, ob=ob, stf=stf, stb=stb, yg=yg)


def _gla_layer_bwd(dy, sv, w, M, tag):
    D = dy.shape[1]
    dyg = _mm(dy, w["w_o"], tb=True, out_dtype=CDT, name=f"{tag}_dwo_x")
    g = {"w_o": _mm(sv["yg"], dy, ta=True, out_dtype=CDT, name=f"{tag}_dwo_w")}
    do, dr, dgn = _gla_out_bwd(sv["of"], sv["ob"], sv["proj"], w["onorm"], dyg, D=D, name=f"{tag}_out_bwd")
    df = _gla_scan_bwd(sv["proj"], sv["gf"], sv["stf"], do, rev=False, M=M, D=D, name=f"{tag}_scan_f_bwd")
    db = _gla_scan_bwd(sv["proj"], sv["gb"], sv["stb"], do, rev=True, M=M, D=D, name=f"{tag}_scan_b_bwd")
    dproj, dw2, dbias = _gla_proj_bwd(sv["proj"], w["w2"], w["gbias"], *df, *db, dr, D=D, name=f"{tag}_proj_bwd")
    dh = _mm(dproj, w["w_in"], tb=True, out_dtype=CDT, name=f"{tag}_din_x")
    g["w_in"] = _mm(sv["h"], dproj, ta=True, out_dtype=CDT, name=f"{tag}_din_w")
    g["w2"] = dw2
    g["gbias"] = dbias[0]
    g["onorm"] = dgn[0]
    return dh, g


def _ffn_fwd(h2, w, M, tag):
    u = _mm(h2, w["w_up"], out_dtype=CDT, name=f"{tag}_up")
    act = _conv_gate_fwd(u, w["conv_w"], w["conv_b"], M=M, name=f"{tag}_conv")
    f = _mm(act, w["w_down"], out_dtype=CDT, name=f"{tag}_down")
    return f, dict(h2=h2, u=u, act=act)


def _ffn_bwd(dyf, sv, w, M, tag):
    dact = _mm(dyf, w["w_down"], tb=True, out_dtype=CDT, name=f"{tag}_ddown_x")
    g = {"w_down": _mm(sv["act"], dyf, ta=True, out_dtype=CDT, name=f"{tag}_ddown_w")}
    duc, cacc = _conv_gate_bwd(sv["u"], dact, w["conv_w"], w["conv_b"], M=M, name=f"{tag}_conv_bwd")
    du = _conv_t(duc, w["conv_w"][::-1], M=M, name=f"{tag}_conv_t")
    dh2 = _mm(du, w["w_up"], tb=True, out_dtype=CDT, name=f"{tag}_dup_x")
    g["w_up"] = _mm(sv["h2"], du, ta=True, out_dtype=CDT, name=f"{tag}_dup_w")
    g["conv_w"] = cacc[0:3]
    g["conv_b"] = cacc[3]
    return dh2, g


def _norm_grads(acc, gain, scale):
    p = acc[:, 1]
    return acc[:, 0], p * gain, jnp.sum(p * (1.0 + scale[:, 0]), axis=0)


def _local_step(x, tgt, mods, lw, final_g, *, M):
    T, D = x.shape
    L = len(lw)
    HD = D // ATT_HEADS
    tabs = _rope_tables(T - M, M, HD)
    sel = lambda i, k: mods[i][:, k:k + 1, :]
    saved = []
    xs, y_prev, gate_prev = x, None, None
    for i in range(L):
        w = lw[i]
        x_in, h = _norm_fwd(xs, y_prev, gate_prev, w["g_mix"], sel(i, 0), sel(i, 1), M=M, name=f"l{i}_norm_mix")
        if "w_qkv" in w:
            y_mix, sm = _attn_layer_fwd(h, w, tabs, M, f"l{i}")
        else:
            y_mix, sm = _gla_layer_fwd(h, w, M, f"l{i}")
        x_mid, h2 = _norm_fwd(x_in, y_mix, sel(i, 2), w["g_ffn"], sel(i, 3), sel(i, 4), M=M, name=f"l{i}_norm_ffn")
        f, sf = _ffn_fwd(h2, w, M, f"l{i}")
        saved.append(dict(x_in=x_in, x_mid=x_mid, y_mix=y_mix, f=f, sm=sm, sf=sf))
        xs, y_prev, gate_prev = x_mid, f, sel(i, 5)

    loss_parts, dx, dyf, acc = _final_loss(xs, y_prev, gate_prev, tgt, final_g, M=M, name="final_loss")
    loss = jnp.sum(loss_parts[:, 0, 0])
    d_final_g = acc[0, 0] + acc[1, 0]
    dmods = [None] * L
    grads = [None] * L
    dgate_ffn = acc[:, 2]
    for i in reversed(range(L)):
        w, sv = lw[i], saved[i]
        dh2, g = _ffn_bwd(dyf, sv["sf"], w, M, f"l{i}")
        dx, dy_mix, acc = _norm_bwd(sv["x_mid"], dh2, dx, w["g_ffn"], sel(i, 4), sv["y_mix"], sel(i, 2), M=M, name=f"l{i}_norm_ffn_bwd")
        dsh_f, dsc_f, g["g_ffn"] = _norm_grads(acc, w["g_ffn"], sel(i, 4))
        dgate_mix = acc[:, 2]
        if "w_qkv" in w:
            dh, gm = _attn_layer_bwd(dy_mix, sv["sm"], w, tabs, M, f"l{i}")
        else:
            dh, gm = _gla_layer_bwd(dy_mix, sv["sm"], w, M, f"l{i}")
        g.update(gm)
        if i > 0:
            dx, dyf, acc = _norm_bwd(sv["x_in"], dh, dx, w["g_mix"], sel(i, 1), saved[i - 1]["f"], sel(i - 1, 5), M=M, name=f"l{i}_norm_mix_bwd")
        else:
            dx, dyf, acc = _norm_bwd(sv["x_in"], dh, dx, w["g_mix"], sel(i, 1), None, None, M=M, name=f"l{i}_norm_mix_bwd")
        dsh_m, dsc_m, g["g_mix"] = _norm_grads(acc, w["g_mix"], sel(i, 1))
        dmods[i] = jnp.stack([dsh_m, dsc_m, dgate_mix, dsh_f, dsc_f, dgate_ffn], axis=1)
        dgate_ffn = acc[:, 2]
        grads[i] = g
    return loss, dx, jnp.stack(dmods, axis=0), grads, d_final_g


ANY = pl.BlockSpec(memory_space=pl.ANY)


def _me():
    return lax.axis_index("x"), lax.axis_index("y"), lax.axis_index("c")


def _other_chips(mx, my):
    return [(1 - mx, my), (mx, 1 - my), (1 - mx, 1 - my)]


def _rcopy(src, dst, sems, k, dev):
    send_sems, recv_sems = sems
    return pltpu.make_async_remote_copy(src_ref=src, dst_ref=dst, send_sem=send_sems.at[k], recv_sem=recv_sems.at[k],
                                        device_id=dev, device_id_type=MESH)


def _all_gather8(x, *, name):
    m, n = x.shape

    def body(x_ref, out_ref, send_sems, recv_sems, local_sem):
        mx, my, mc = _me()
        sems = (send_sems, recv_sems)
        me, sib = (mx, my, mc), (mx, my, 1 - mc)
        chips = _other_chips(mx, my)
        blk = lambda d: out_ref.at[4 * d[0] + 2 * d[1] + d[2]]
        mine = pltpu.make_async_copy(x_ref, blk(me), local_sem)
        mine.start()
        first = [_rcopy(x_ref, blk(me), sems, 0, sib)]
        first += [_rcopy(x_ref, blk(me), sems, 1 + j, (*ch, mc)) for j, ch in enumerate(chips)]
        for cp in first:
            cp.start()
        passed = [_rcopy(blk((*ch, mc)), blk((*ch, mc)), sems, 4 + j, sib) for j, ch in enumerate(chips)]
        for j, ch in enumerate(chips):
            _rcopy(x_ref, blk((*ch, mc)), sems, 1 + j, me).wait_recv()
            passed[j].start()
        _rcopy(x_ref, blk(sib), sems, 0, me).wait_recv()
        for j, ch in enumerate(chips):
            _rcopy(x_ref, blk((*ch, 1 - mc)), sems, 4 + j, me).wait_recv()
        for cp in first + passed:
            cp.wait_send()
        mine.wait()

    return pl.pallas_call(
        body, name=name, out_shape=jax.ShapeDtypeStruct((8, m, n), x.dtype), in_specs=[ANY], out_specs=ANY,
        scratch_shapes=[pltpu.SemaphoreType.DMA((7,)), pltpu.SemaphoreType.DMA((7,)), pltpu.SemaphoreType.DMA],
    )(x)


def _gather_chips(x, *, name):
    R, W = x.shape
    Rh = R // 2

    def body(x_ref, out_ref, send_sems, recv_sems, local_sem):
        mx, my, mc = _me()
        sems = (send_sems, recv_sems)
        me, sib = (mx, my, mc), (mx, my, 1 - mc)
        chips = _other_chips(mx, my)
        mine_rows = pl.ds(mc * Rh, Rh)
        sib_rows = pl.ds((1 - mc) * Rh, Rh)
        p = 2 * mx + my
        local = pltpu.make_async_copy(x_ref, out_ref.at[p], local_sem)
        local.start()
        first = [_rcopy(x_ref.at[mine_rows], out_ref.at[p, mine_rows], sems, j, (*ch, mc)) for j, ch in enumerate(chips)]
        for cp in first:
            cp.start()
        passed = []
        for j, ch in enumerate(chips):
            q = 2 * ch[0] + ch[1]
            _rcopy(x_ref.at[mine_rows], out_ref.at[q, mine_rows], sems, j, me).wait_recv()
            fwd = _rcopy(out_ref.at[q, mine_rows], out_ref.at[q, mine_rows], sems, 3 + j, sib)
            fwd.start()
            passed.append(fwd)
        for j, ch in enumerate(chips):
            q = 2 * ch[0] + ch[1]
            _rcopy(x_ref.at[mine_rows], out_ref.at[q, sib_rows], sems, 3 + j, me).wait_recv()
        for cp in first + passed:
            cp.wait_send()
        local.wait()

    return pl.pallas_call(
        body, name=name, out_shape=jax.ShapeDtypeStruct((4, R, W), x.dtype), in_specs=[ANY], out_specs=ANY,
        scratch_shapes=[pltpu.SemaphoreType.DMA((6,)), pltpu.SemaphoreType.DMA((6,)), pltpu.SemaphoreType.DMA],
    )(x)


def _pair_split(g, *, name):
    _, R, W = g.shape
    Rh = R // 2

    def body(g_ref, own_ref, got_ref, send_sems, recv_sems, local_sem):
        mx, my, mc = _me()
        sems = (send_sems, recv_sems)
        sib = (mx, my, 1 - mc)
        keep = pltpu.make_async_copy(g_ref.at[:, pl.ds(mc * Rh, Rh), :], own_ref, local_sem)
        keep.start()
        cp = _rcopy(g_ref.at[:, pl.ds((1 - mc) * Rh, Rh), :], got_ref, sems, 0, sib)
        cp.start()
        cp.wait_recv()
        cp.wait_send()
        keep.wait()

    sh = jax.ShapeDtypeStruct((4, Rh, W), g.dtype)
    return pl.pallas_call(
        body, name=name, out_shape=(sh, sh), in_specs=[ANY], out_specs=(ANY, ANY),
        scratch_shapes=[pltpu.SemaphoreType.DMA((1,)), pltpu.SemaphoreType.DMA((1,)), pltpu.SemaphoreType.DMA],
    )(g)


def _scatter_chips(pz, *, name):
    _, Rh, W = pz.shape

    def body(p_ref, out_ref, send_sems, recv_sems, local_sem):
        mx, my, mc = _me()
        sems = (send_sems, recv_sems)
        me = (mx, my, mc)
        chips = _other_chips(mx, my)
        p = 2 * mx + my
        local = pltpu.make_async_copy(p_ref.at[p], out_ref.at[p], local_sem)
        local.start()
        sends = [_rcopy(p_ref.at[2 * ch[0] + ch[1]], out_ref.at[p], sems, j, (*ch, mc)) for j, ch in enumerate(chips)]
        for cp in sends:
            cp.start()
        for j, ch in enumerate(chips):
            _rcopy(p_ref.at[p], out_ref.at[2 * ch[0] + ch[1]], sems, j, me).wait_recv()
        for cp in sends:
            cp.wait_send()
        local.wait()

    return pl.pallas_call(
        body, name=name, out_shape=jax.ShapeDtypeStruct(pz.shape, pz.dtype), in_specs=[ANY], out_specs=ANY,
        scratch_shapes=[pltpu.SemaphoreType.DMA((3,)), pltpu.SemaphoreType.DMA((3,)), pltpu.SemaphoreType.DMA],
    )(pz)


def _pair_share(half, *, name):
    Rh, W = half.shape

    def body(h_ref, out_ref, send_sems, recv_sems, local_sem):
        mx, my, mc = _me()
        sems = (send_sems, recv_sems)
        sib = (mx, my, 1 - mc)
        rows = pl.ds(mc * Rh, Rh)
        keep = pltpu.make_async_copy(h_ref, out_ref.at[rows], local_sem)
        keep.start()
        cp = _rcopy(h_ref, out_ref.at[rows], sems, 0, sib)
        cp.start()
        _rcopy(h_ref, out_ref.at[pl.ds((1 - mc) * Rh, Rh)], sems, 0, sib).wait_recv()
        cp.wait_send()
        keep.wait()

    return pl.pallas_call(
        body, name=name, out_shape=jax.ShapeDtypeStruct((2 * Rh, W), half.dtype), in_specs=[ANY], out_specs=ANY,
        scratch_shapes=[pltpu.SemaphoreType.DMA((1,)), pltpu.SemaphoreType.DMA((1,)), pltpu.SemaphoreType.DMA],
    )(half)


def _add2(a, b, *, name):
    n, Rh, W = a.shape
    tr = _pick(Rh, (PACK_ROWS,))
    spec = pl.BlockSpec((1, tr, W), lambda q, i: (q, i, 0))

    def body(a_ref, b_ref, o_ref):
        o_ref[...] = (a_ref[...].astype(F32) + b_ref[...].astype(F32)).astype(o_ref.dtype)

    return pl.pallas_call(
        body, name=name, grid=(n, Rh // tr), in_specs=[spec, spec], out_specs=spec,
        out_shape=jax.ShapeDtypeStruct(a.shape, a.dtype), compiler_params=_cparams(("parallel", "parallel")),
    )(a, b)


def _sum_lead(a, *, name):
    n, R, W = a.shape
    tr = _pick(R, (PACK_ROWS,))
    specs = [pl.BlockSpec((1, tr, W), functools.partial(lambda i, q: (q, i, 0), q=q)) for q in range(n)]

    def body(*refs):
        acc = refs[0][0].astype(F32)
        for r in refs[1:n]:
            acc = acc + r[0].astype(F32)
        refs[n][...] = acc

    return pl.pallas_call(
        body, name=name, grid=(R // tr,), in_specs=specs, out_specs=pl.BlockSpec((tr, W), lambda i: (i, 0)),
        out_shape=jax.ShapeDtypeStruct((R, W), F32), compiler_params=_cparams(("parallel",)),
    )(*([a] * n))


SHARDED = (("ffn_w_up", 2), ("ffn_conv_w", 2), ("ffn_w_down", 1), ("attn_w_qkv", 2), ("attn_w_o", 1), ("gla_w_in", 2),
           ("gla_gf_w1", 1), ("gla_gf_w2", 2), ("gla_gf_b", 1), ("gla_gb_w1", 1), ("gla_gb_w2", 2), ("gla_gb_b", 1),
           ("gla_onorm_g", 1), ("gla_w_o", 1))
EXACT = ("ffn_conv_w", "gla_gf_b", "gla_gb_b", "gla_onorm_g")
PACK_UNIT = 2 * PACK_ROWS * LANES


def _pack(flat):
    n = flat.shape[-1]
    total = -(-n // PACK_UNIT) * PACK_UNIT
    pad = [(0, 0)] * (flat.ndim - 1) + [(0, total - n)]
    return jnp.pad(flat, pad).reshape(flat.shape[:-1] + (total // LANES, LANES))


def _gather_weights(shards):
    segs, layout = [], []
    for name, ax in SHARDED:
        w = shards[name]
        hi = w.astype(CDT)
        parts = [hi, (w - hi.astype(F32)).astype(CDT)] if name in EXACT else [hi]
        for part in parts:
            segs.append(part.reshape(-1))
            layout.append((name, ax, w.shape))
    got = _gather_chips(_pack(jnp.concatenate(segs)), name="gather_weights")
    got = got.reshape(4, -1)
    full, off = {}, 0
    for name, ax, shape in layout:
        n = 1
        for s in shape:
            n *= s
        seg = jnp.moveaxis(got[:, off:off + n].reshape((4,) + shape), 0, ax)
        seg = seg.reshape(shape[:ax] + (4 * shape[ax],) + shape[ax + 1:])
        off += n
        full[name] = full[name].astype(F32) + seg.astype(F32) if name in full else seg
    return full


def _reduce_grads(gfull, shard_shapes):
    segs = []
    for name, ax in SHARDED:
        g = gfull[name]
        shape = shard_shapes[name]
        g = g.reshape(shape[:ax] + (4, shape[ax]) + shape[ax + 1:])
        segs.append(jnp.moveaxis(g, ax, 0).reshape(4, -1).astype(CDT))
    buf = _pack(jnp.concatenate(segs, axis=1))
    own, got = _pair_split(buf, name="grads_pair_split")
    chip_part = _add2(own, got, name="grads_pair_sum")
    parts = _scatter_chips(chip_part, name="grads_scatter")
    half = _sum_lead(parts, name="grads_chip_sum")
    red = _pair_share(half, name="grads_pair_share").reshape(-1)
    out, off = {}, 0
    for name, ax in SHARDED:
        shape = shard_shapes[name]
        n = 1
        for s in shape:
            n *= s
        out[name] = red[off:off + n].reshape(shape)
        off += n
    return out


WEIGHTS = ("c_ctx", "ada_w", "ada_b", "norm_mix_g", "norm_ffn_g", "ffn_w_up", "ffn_conv_w", "ffn_conv_b", "ffn_w_down",
           "attn_w_qkv", "attn_sink", "attn_w_o", "gla_w_in", "gla_gf_w1", "gla_gf_w2", "gla_gf_b", "gla_gb_w1", "gla_gb_w2",
           "gla_gb_b", "gla_onorm_g", "gla_w_o", "final_norm_g")
REPLICATED = ("norm_mix_g", "norm_ffn_g", "ffn_conv_b", "attn_sink", "final_norm_g", "c_ctx")
SMALL_W = 2048
ROWS16 = 16


def _layer_weights(full, rep, L):
    D = rep["norm_mix_g"].shape[1]
    lw = []
    for i in range(L):
        j = i // 2
        w = dict(g_mix=rep["norm_mix_g"][i][None], g_ffn=rep["norm_ffn_g"][i][None], w_up=full["ffn_w_up"][i],
                 conv_w=full["ffn_conv_w"][i], conv_b=rep["ffn_conv_b"][i][None], w_down=full["ffn_w_down"][i])
        if i % 2 == 0:
            w.update(w_qkv=full["attn_w_qkv"][j], sink=rep["attn_sink"][j], w_o=full["attn_w_o"][j])
        else:
            pad = jnp.zeros((D, 128 - 2 * GATE_RANK), CDT)
            w2 = jnp.zeros((128, D), CDT)
            w2 = w2.at[0:GATE_RANK, 0:D // 2].set(full["gla_gf_w2"][j])
            w2 = w2.at[GATE_RANK:2 * GATE_RANK, D // 2:].set(full["gla_gb_w2"][j])
            w.update(w_in=jnp.concatenate([full["gla_w_in"][j], full["gla_gf_w1"][j], full["gla_gb_w1"][j], pad], axis=1),
                     w2=w2, gbias=jnp.concatenate([full["gla_gf_b"][j], full["gla_gb_b"][j]])[None],
                     onorm=full["gla_onorm_g"][j][None], w_o=full["gla_w_o"][j])
        lw.append(w)
    return lw


def _full_grads(grads, D):
    att = [g for g in grads if "w_qkv" in g]
    gla = [g for g in grads if "w_in" in g]
    st = lambda xs: jnp.stack(xs, axis=0)
    r = GATE_RANK
    return {
        "ffn_w_up": st([g["w_up"] for g in grads]), "ffn_conv_w": st([g["conv_w"] for g in grads]),
        "ffn_w_down": st([g["w_down"] for g in grads]),
        "attn_w_qkv": st([g["w_qkv"] for g in att]), "attn_w_o": st([g["w_o"] for g in att]),
        "gla_w_in": st([g["w_in"][:, :3 * D] for g in gla]),
        "gla_gf_w1": st([g["w_in"][:, 3 * D:3 * D + r] for g in gla]),
        "gla_gb_w1": st([g["w_in"][:, 3 * D + r:3 * D + 2 * r] for g in gla]),
        "gla_gf_w2": st([g["w2"][0:r, :D // 2] for g in gla]), "gla_gb_w2": st([g["w2"][r:2 * r, D // 2:] for g in gla]),
        "gla_gf_b": st([g["gbias"][:D // 2] for g in gla]), "gla_gb_b": st([g["gbias"][D // 2:] for g in gla]),
        "gla_onorm_g": st([g["onorm"] for g in gla]), "gla_w_o": st([g["w_o"] for g in gla]),
        "norm_mix_g": st([g["g_mix"] for g in grads]), "norm_ffn_g": st([g["g_ffn"] for g in grads]),
        "ffn_conv_b": st([g["conv_b"] for g in grads]), "attn_sink": st([g["sink"] for g in att]),
    }


def kernel(x, c, ctx, c_ctx, ada_w, ada_b, norm_mix_g, norm_ffn_g, ffn_w_up, ffn_conv_w, ffn_conv_b, ffn_w_down, attn_w_qkv, attn_sink, attn_w_o, gla_w_in, gla_gf_w1, gla_gf_w2, gla_gf_b, gla_gb_w1, gla_gb_w2, gla_gb_b, gla_onorm_g, gla_w_o, final_norm_g, loss_target, m_c_ctx, m_ada_w, m_ada_b, m_norm_mix_g, m_norm_ffn_g, m_ffn_w_up, m_ffn_conv_w, m_ffn_conv_b, m_ffn_w_down, m_attn_w_qkv, m_attn_sink, m_attn_w_o, m_gla_w_in, m_gla_gf_w1, m_gla_gf_w2, m_gla_gf_b, m_gla_gb_w1, m_gla_gb_w2, m_gla_gb_b, m_gla_onorm_g, m_gla_w_o, m_final_norm_g, v_c_ctx, v_ada_w, v_ada_b, v_norm_mix_g, v_norm_ffn_g, v_ffn_w_up, v_ffn_conv_w, v_ffn_conv_b, v_ffn_w_down, v_attn_w_qkv, v_attn_sink, v_attn_w_o, v_gla_w_in, v_gla_gf_w1, v_gla_gf_w2, v_gla_gf_b, v_gla_gb_w1, v_gla_gb_w2, v_gla_gb_b, v_gla_onorm_g, v_gla_w_o, v_final_norm_g):
    wts = dict(c_ctx=c_ctx, ada_w=ada_w, ada_b=ada_b, norm_mix_g=norm_mix_g, norm_ffn_g=norm_ffn_g, ffn_w_up=ffn_w_up,
               ffn_conv_w=ffn_conv_w, ffn_conv_b=ffn_conv_b, ffn_w_down=ffn_w_down, attn_w_qkv=attn_w_qkv, attn_sink=attn_sink,
               attn_w_o=attn_w_o, gla_w_in=gla_w_in, gla_gf_w1=gla_gf_w1, gla_gf_w2=gla_gf_w2, gla_gf_b=gla_gf_b,
               gla_gb_w1=gla_gb_w1, gla_gb_w2=gla_gb_w2, gla_gb_b=gla_gb_b, gla_onorm_g=gla_onorm_g, gla_w_o=gla_w_o,
               final_norm_g=final_norm_g)
    mom_m = dict(zip(WEIGHTS, (m_c_ctx, m_ada_w, m_ada_b, m_norm_mix_g, m_norm_ffn_g, m_ffn_w_up, m_ffn_conv_w, m_ffn_conv_b,
                               m_ffn_w_down, m_attn_w_qkv, m_attn_sink, m_attn_w_o, m_gla_w_in, m_gla_gf_w1, m_gla_gf_w2,
                               m_gla_gf_b, m_gla_gb_w1, m_gla_gb_w2, m_gla_gb_b, m_gla_onorm_g, m_gla_w_o, m_final_norm_g)))
    mom_v = dict(zip(WEIGHTS, (v_c_ctx, v_ada_w, v_ada_b, v_norm_mix_g, v_norm_ffn_g, v_ffn_w_up, v_ffn_conv_w, v_ffn_conv_b,
                               v_ffn_w_down, v_attn_w_qkv, v_attn_sink, v_attn_w_o, v_gla_w_in, v_gla_gf_w1, v_gla_gf_w2,
                               v_gla_gf_b, v_gla_gb_w1, v_gla_gb_w2, v_gla_gb_b, v_gla_onorm_g, v_gla_w_o, v_final_norm_g)))
    L, D, W6 = ada_w.shape[0], ada_w.shape[1], ada_w.shape[2]
    M = ctx.shape[1]
    mx, my, mc = _me()
    chip = 2 * mx + my
    batch = 4 * mx + 2 * my + mc

    crow = jnp.concatenate([c.astype(F32), jnp.zeros((7, D), F32)], axis=0)
    call = _all_gather8(crow, name="gather_c")[:, 0, :]
    s16 = jnp.concatenate([jax.nn.silu(call), jax.nn.silu(c_ctx)[None], jnp.zeros((ROWS16 - 9, D), F32)], axis=0)
    s16c = s16.astype(CDT)
    ada_c = ada_w.astype(CDT)
    mod_cols = jnp.concatenate([_mm(s16c, ada_c[i], out_dtype=F32, name=f"mods_l{i}") for i in range(L)], axis=0)
    mod_all = _all_gather8(mod_cols, name="gather_mods")
    mod_all = mod_all.reshape(4, 2, L, ROWS16, W6)[:, 0]
    mod_all = jnp.moveaxis(mod_all, 0, 2).reshape(L, ROWS16, 4 * W6) + ada_b[:, None, :]
    mod_mine = jnp.stack([mod_all[:, 8], lax.dynamic_index_in_dim(mod_all, batch, axis=1, keepdims=False)], axis=1)
    mods = mod_mine.reshape(L, 2, N_MOD, D)

    full = _gather_weights({name: wts[name] for name, _ in SHARDED})
    lw = _layer_weights(full, wts, L)
    xcat = jnp.concatenate([ctx[0], x[0]], axis=0)
    loss, dx, dmods, grads, d_final_g = _local_step(xcat, loss_target[0], mods, lw, final_norm_g[None], M=M)
    loss = lax.psum(loss, ("x", "y", "c"))
    grad_x = dx[M:][None]

    dm_all = _all_gather8(dmods.reshape(L * 2, N_MOD * D), name="gather_dmods")
    dm_sum = _sum_lead(dm_all, name="dmods_sum").reshape(L, 2, N_MOD * D)
    dm_rows = dm_all.reshape(8, L, 2, N_MOD * D)[:, :, 1]
    dm16 = jnp.concatenate([jnp.moveaxis(dm_rows, 0, 1), dm_sum[:, 0:1], jnp.zeros((L, ROWS16 - 9, N_MOD * D), F32)], axis=1)
    dm16 = lax.dynamic_slice_in_dim(dm16, chip * W6, W6, axis=2).astype(CDT)
    g_ada_w = jnp.stack([_mm(s16c, dm16[i], ta=True, out_dtype=F32, name=f"dada_w_l{i}") for i in range(L)], axis=0)
    ds16 = _mm(dm16[0], ada_c[0], tb=True, out_dtype=F32, name="dcond_l0")
    for i in range(1, L):
        ds16 = ds16 + _mm(dm16[i], ada_c[i], tb=True, out_dtype=F32, name=f"dcond_l{i}")
    d_sctx = ds16[8] * jnp.where(mc == 0, 1.0, 0.0)

    gfull = _full_grads(grads, D)
    gfull["final_norm_g"] = d_final_g
    gfull["c_ctx"] = d_sctx
    red = _reduce_grads(gfull, {name: wts[name].shape for name, _ in SHARDED})

    flat = jnp.concatenate([gfull[name].astype(F32).reshape(-1) for name in REPLICATED])
    rows = -(-flat.shape[0] // (8 * SMALL_W)) * 8
    small = jnp.pad(flat, (0, rows * SMALL_W - flat.shape[0])).reshape(rows, SMALL_W)
    small = _sum_lead(_all_gather8(small, name="gather_small"), name="small_sum").reshape(-1)
    off = 0
    for name in REPLICATED:
        n = wts[name].size
        red[name] = small[off:off + n].reshape(wts[name].shape)
        off += n
    red["c_ctx"] = red["c_ctx"] * _dsilu(c_ctx)
    red["ada_w"] = g_ada_w
    red["ada_b"] = dm_sum[:, 0] + dm_sum[:, 1]

    deltas, new_m, new_v = {}, {}, {}
    for name in WEIGHTS:
        w = wts[name]
        view = (lambda a: a.reshape(-1, a.shape[-1])) if w.ndim > 1 else (lambda a: a.reshape(1, -1))
        d, m2, v2 = _adamw(view(w), view(red[name]), view(mom_m[name]), view(mom_v[name]), name=f"adamw_{name}")
        deltas[name], new_m[name], new_v[name] = d.reshape(w.shape), m2.reshape(w.shape), v2.reshape(w.shape)
    return (loss, grad_x, *[red[n] for n in WEIGHTS], *[deltas[n] for n in WEIGHTS], *[new_m[n] for n in WEIGHTS],
            *[new_v[n] for n in WEIGHTS])
```

```python
import functools

import jax
import jax.numpy as jnp
from jax import lax
from jax.experimental import pallas as pl
from jax.experimental.pallas import tpu as pltpu

F32 = jnp.float32
CDT = jnp.bfloat16
VMEM_LIMIT = 56 * 1024 * 1024
MESH = pl.DeviceIdType.MESH

ATT_HEADS = 16
ATT_KV = 4
ATT_G = ATT_HEADS // ATT_KV
BLK = 128
GRID_W = 64
ROPE_BASE = 10000.0
GLA_H = 4
GATE_RANK = 16
GATE_NORM = 16.0
CHUNK = 64
EPS = 1e-6
N_MOD = 6
LR, B1, B2, AEPS, WD, STEP = 0.001, 0.9, 0.999, 1e-08, 0.01, 10
PACK_ROWS = 512

NN = (((1,), (0,)), ((), ()))
NT = (((1,), (1,)), ((), ()))
TN = (((0,), (0,)), ((), ()))


def _dg(a, b, dims):
    return lax.dot_general(a, b, dims, preferred_element_type=F32)


def _pick(dim, cands):
    for c in cands:
        if dim % c == 0:
            return c
    return dim


def _cparams(sem):
    return pltpu.CompilerParams(dimension_semantics=sem, vmem_limit_bytes=VMEM_LIMIT)


def _silu(x):
    return x * (1.0 / (1.0 + jnp.exp(-x)))


def _dsilu(x):
    s = 1.0 / (1.0 + jnp.exp(-x))
    return s * (1.0 + x * (1.0 - s))


MM_VMEM_BUDGET = 40 * 1024 * 1024
TILE_M = (2048, 1408, 1088, 1024, 544, 512, 256, 128)
TILE_N = (2048, 1536, 1408, 1024, 768, 512, 256, 128)
TILE_K = (2176, 2048, 1408, 1088, 1024, 768, 512)


def _mm_tiles(m_unit, n_unit, k_unit, out_bytes):
    best = None
    for tm in [c for c in TILE_M if m_unit % c == 0] or [m_unit]:
        for tn in [c for c in TILE_N if n_unit % c == 0] or [n_unit]:
            for tk in [c for c in TILE_K if k_unit % c == 0] or [k_unit]:
                vmem = 4 * tk * (tm + tn) + tm * tn * (2 * out_bytes + 4)
                if vmem > MM_VMEM_BUDGET:
                    continue
                key = (tm * tn / (tm + tn), tk)
                if best is None or key > best[0]:
                    best = (key, (tm, tn, tk))
    assert best is not None, (m_unit, n_unit, k_unit)
    return best[1]


def _mm(a, b, *, ta=False, tb=False, out_dtype=F32, name, a_split=False, b_chip=None, out_chip=None, extra=None):
    if a_split:
        assert not ta
        M, K = a.shape[1], a.shape[0] * a.shape[2]
    elif ta:
        K, M = a.shape
    else:
        M, K = a.shape
    bs = list(b.shape) if b_chip is None else list(b.shape[1:])
    if b_chip is not None:
        bs[b_chip] *= b.shape[0]
    N, K2 = bs if tb else bs[::-1]
    assert K == K2, (a.shape, b.shape, ta, tb, b_chip)
    m_unit, n_unit, k_unit = M, N, K
    if a_split:
        k_unit = a.shape[2]
    if b_chip is not None:
        if (b_chip == 0) == tb:
            n_unit = N // b.shape[0]
        else:
            k_unit = min(k_unit, K // b.shape[0])
    if out_chip == 0:
        m_unit = M // 4
    elif out_chip == 1:
        n_unit = min(n_unit, N // 4)
    tm, tn, tk = _mm_tiles(m_unit, n_unit, k_unit, jnp.dtype(out_dtype).itemsize)
    nk = K // tk
    dims = TN if ta else (NT if tb else NN)

    def body(*refs):
        if extra is None:
            a_ref, b_ref, o_ref, acc_ref = refs
        else:
            a_ref, b_ref, a2_ref, b2_ref, o_ref, acc_ref = refs
        k = pl.program_id(2)

        @pl.when(k == 0)
        def _():
            if extra is None:
                acc_ref[...] = jnp.zeros_like(acc_ref)
            else:
                acc_ref[...] = _dg(a2_ref[...], b2_ref[...], dims)

        acc_ref[...] += _dg(a_ref[...], b_ref[...], dims)

        @pl.when(k == nk - 1)
        def _():
            o_ref[...] = acc_ref[...].astype(o_ref.dtype)

    def b_index(n, m, k):
        i0, i1 = (n, k) if tb else (k, n)
        if b_chip is None:
            return (i0, i1)
        if b_chip == 0:
            nb = b.shape[1] // b_block[0]
            return (i0 // nb, i0 % nb, i1)
        nb = b.shape[2] // b_block[1]
        return (i1 // nb, i0, i1 % nb)

    def o_index(n, m, k):
        if out_chip is None:
            return (m, n)
        if out_chip == 0:
            mb = m_unit // tm
            return (m // mb, m % mb, n)
        nb = n_unit // tn
        return (n // nb, m, n % nb)

    b_block = (tn, tk) if tb else (tk, tn)
    lead = lambda blk, on: ((None,) + blk) if on else blk
    if a_split:
        kb = a.shape[2] // tk
        a_spec = pl.BlockSpec((None, tm, tk), lambda n, m, k: (k // kb, m, k % kb))
    elif ta:
        a_spec = pl.BlockSpec((tk, tm), lambda n, m, k: (k, m))
    else:
        a_spec = pl.BlockSpec((tm, tk), lambda n, m, k: (m, k))
    in_specs = [a_spec, pl.BlockSpec(lead(b_block, b_chip is not None), b_index)]
    args = [a, b]
    if extra is not None:
        assert not ta
        a2, b2 = extra
        E = a2.shape[1]
        in_specs += [pl.BlockSpec((tm, E), lambda n, m, k: (m, 0)),
                     pl.BlockSpec((tn, E), lambda n, m, k: (n, 0)) if tb else pl.BlockSpec((E, tn), lambda n, m, k: (0, n))]
        args += [a2, b2]
    out_full = (M, N) if out_chip is None else ((4, M // 4, N) if out_chip == 0 else (4, M, N // 4))
    return pl.pallas_call(
        body, name=name, grid=(N // tn, M // tm, nk),
        in_specs=in_specs,
        out_specs=pl.BlockSpec(lead((tm, tn), out_chip is not None), o_index),
        out_shape=jax.ShapeDtypeStruct(out_full, out_dtype),
        scratch_shapes=[pltpu.VMEM((tm, tn), F32)],
        compiler_params=_cparams(("parallel", "parallel", "arbitrary")),
    )(*args)


def _seg_spec(D, first_lat):
    return pl.BlockSpec((1, 1, D), lambda i: (jnp.where(i >= first_lat, 1, 0), 0, 0))


def _norm_fwd(x, y, gate, g, shift, scale, *, M, name):
    T, D = x.shape
    tm = _pick(T, (256,))
    first_lat = M // tm
    has_res = y is not None
    seg = _seg_spec(D, first_lat)
    row = pl.BlockSpec((tm, D), lambda i: (i, 0))

    def body(*refs):
        if has_res:
            x_ref, y_ref, gate_ref, g_ref, sh_ref, sc_ref, xo_ref, h_ref = refs
            xv = x_ref[...] + gate_ref[0] * y_ref[...].astype(F32)
            xo_ref[...] = xv
        else:
            x_ref, g_ref, sh_ref, sc_ref, h_ref = refs
            xv = x_ref[...]
        rstd = lax.rsqrt(jnp.mean(xv * xv, axis=-1, keepdims=True) + EPS)
        h = xv * rstd * g_ref[...] * (1.0 + sc_ref[0]) + sh_ref[0]
        h_ref[...] = h.astype(h_ref.dtype)

    gspec = pl.BlockSpec((1, D), lambda i: (0, 0))
    if has_res:
        ins = [x, y, gate, g, shift, scale]
        in_specs = [row, row, seg, gspec, seg, seg]
        out_shape = (jax.ShapeDtypeStruct((T, D), F32), jax.ShapeDtypeStruct((T, D), CDT))
        out_specs = (row, row)
    else:
        ins = [x, g, shift, scale]
        in_specs = [row, gspec, seg, seg]
        out_shape = jax.ShapeDtypeStruct((T, D), CDT)
        out_specs = row
    out = pl.pallas_call(
        body, name=name, grid=(T // tm,), in_specs=in_specs, out_specs=out_specs, out_shape=out_shape,
        compiler_params=_cparams(("parallel",)),
    )(*ins)
    return out if has_res else (x, out)


def _norm_bwd(x, dh, dx_in, g, scale, y_prev, gate_prev, *, M, name):
    T, D = x.shape
    tm = _pick(T, (256,))
    first_lat = M // tm
    has_prev = y_prev is not None
    seg = _seg_spec(D, first_lat)
    row = pl.BlockSpec((tm, D), lambda i: (i, 0))
    gspec = pl.BlockSpec((1, D), lambda i: (0, 0))

    def body(*refs):
        if has_prev:
            x_ref, dh_ref, dxi_ref, g_ref, sc_ref, yp_ref, gp_ref, dx_ref, dy_ref, acc_ref = refs
        else:
            x_ref, dh_ref, dxi_ref, g_ref, sc_ref, dx_ref, acc_ref = refs
        i = pl.program_id(0)

        @pl.when(jnp.logical_or(i == 0, i == first_lat))
        def _():
            acc_ref[...] = jnp.zeros_like(acc_ref)

        xv = x_ref[...]
        rstd = lax.rsqrt(jnp.mean(xv * xv, axis=-1, keepdims=True) + EPS)
        xn = xv * rstd
        dh = dh_ref[...].astype(F32)
        dxn = dh * (g_ref[...] * (1.0 + sc_ref[0]))
        dx = dxi_ref[...] + rstd * (dxn - xn * jnp.mean(dxn * xn, axis=-1, keepdims=True))
        dx_ref[...] = dx
        acc_ref[0, 0:1, :] += jnp.sum(dh, axis=0, keepdims=True)
        acc_ref[0, 1:2, :] += jnp.sum(dh * xn, axis=0, keepdims=True)
        if has_prev:
            dy_ref[...] = (dx * gp_ref[0]).astype(dy_ref.dtype)
            acc_ref[0, 2:3, :] += jnp.sum(dx * yp_ref[...].astype(F32), axis=0, keepdims=True)

    acc_spec = pl.BlockSpec((1, 8, D), lambda i: (jnp.where(i >= first_lat, 1, 0), 0, 0))
    acc_shape = jax.ShapeDtypeStruct((2, 8, D), F32)
    if has_prev:
        ins = [x, dh, dx_in, g, scale, y_prev, gate_prev]
        in_specs = [row, row, row, gspec, seg, row, seg]
        out_shape = (jax.ShapeDtypeStruct((T, D), F32), jax.ShapeDtypeStruct((T, D), CDT), acc_shape)
        out_specs = (row, row, acc_spec)
    else:
        ins = [x, dh, dx_in, g, scale]
        in_specs = [row, row, row, gspec, seg]
        out_shape = (jax.ShapeDtypeStruct((T, D), F32), acc_shape)
        out_specs = (row, acc_spec)
    out = pl.pallas_call(
        body, name=name, grid=(T // tm,), in_specs=in_specs, out_specs=out_specs, out_shape=out_shape,
        compiler_params=_cparams(("arbitrary",)),
    )(*ins)
    if has_prev:
        return out
    return out[0], None, out[1]


def _final_loss(x, y_prev, gate_prev, tgt, g, *, M, name):
    T, D = x.shape
    tm = _pick(T, (256,))
    first_lat = M // tm
    nt = T // tm
    seg = _seg_spec(D, first_lat)
    row = pl.BlockSpec((tm, D), lambda i: (i, 0))
    gspec = pl.BlockSpec((1, D), lambda i: (0, 0))
    tspec = pl.BlockSpec((tm, D), lambda i: (jnp.maximum(i - first_lat, 0), 0))

    def body(x_ref, yp_ref, gp_ref, t_ref, g_ref, loss_ref, dx_ref, dy_ref, acc_ref):
        i = pl.program_id(0)

        @pl.when(jnp.logical_or(i == 0, i == first_lat))
        def _():
            acc_ref[...] = jnp.zeros_like(acc_ref)

        lat = jnp.where(i >= first_lat, 1.0, 0.0)
        yp = yp_ref[...].astype(F32)
        xv = x_ref[...] + gp_ref[0] * yp
        rstd = lax.rsqrt(jnp.mean(xv * xv, axis=-1, keepdims=True) + EPS)
        xn = xv * rstd
        diff = (xn * g_ref[...] - t_ref[...]) * lat
        part = 0.5 * jnp.sum(jnp.sum(diff * diff, axis=-1, keepdims=True), axis=0, keepdims=True) * (1.0 / D)
        loss_ref[0] = jnp.broadcast_to(part, (8, 128))
        dyv = diff * (1.0 / D)
        dxn = dyv * g_ref[...]
        dx = rstd * (dxn - xn * jnp.mean(dxn * xn, axis=-1, keepdims=True))
        dx_ref[...] = dx
        dy_ref[...] = (dx * gp_ref[0]).astype(dy_ref.dtype)
        acc_ref[0, 0:1, :] += jnp.sum(dyv * xn, axis=0, keepdims=True)
        acc_ref[0, 2:3, :] += jnp.sum(dx * yp, axis=0, keepdims=True)

    return pl.pallas_call(
        body, name=name, grid=(nt,),
        in_specs=[row, row, seg, tspec, gspec],
        out_specs=(pl.BlockSpec((1, 8, 128), lambda i: (i, 0, 0)), row, row,
                   pl.BlockSpec((1, 8, D), lambda i: (jnp.where(i >= first_lat, 1, 0), 0, 0))),
        out_shape=(jax.ShapeDtypeStruct((nt, 8, 128), F32), jax.ShapeDtypeStruct((T, D), F32),
                   jax.ShapeDtypeStruct((T, D), CDT), jax.ShapeDtypeStruct((2, 8, D), F32)),
        compiler_params=_cparams(("arbitrary",)),
    )(x, y_prev, gate_prev, tgt, g)


HALO = 16
CONV_TC = (1408, 512)


def _taps(uc, prev16, next16, keep_prev, keep_next):
    tm = uc.shape[0]
    u = uc.astype(F32)
    rows = lax.broadcasted_iota(jnp.int32, u.shape, 0)
    pr = prev16[HALO - 1:HALO, :].astype(F32) * keep_prev
    nx = next16[0:1, :].astype(F32) * keep_next
    um = jnp.where(rows == 0, pr, pltpu.roll(u, 1, 0))
    up = jnp.where(rows == tm - 1, nx, pltpu.roll(u, tm - 1, 0))
    return um, u, up


def _conv3(uc, prev16, next16, w, bias, keep_prev, keep_next):
    um, u, up = _taps(uc, prev16, next16, keep_prev, keep_next)
    out = w[0:1, :] * um + w[1:2, :] * u + w[2:3, :] * up
    return out if bias is None else out + bias


def _conv_specs(tm, tc, T, col):
    hb = tm // HALO
    last = T // HALO - 1
    return [
        pl.BlockSpec((tm, tc), lambda j, i: (i, col(j))),
        pl.BlockSpec((HALO, tc), lambda j, i: (jnp.maximum(i * hb - 1, 0), col(j))),
        pl.BlockSpec((HALO, tc), lambda j, i: (jnp.minimum((i + 1) * hb, last), col(j))),
    ]


def _seg_keep(i, first_lat, nt):
    keep_prev = jnp.where(jnp.logical_or(i == 0, i == first_lat), 0.0, 1.0)
    keep_next = jnp.where(jnp.logical_or(i == first_lat - 1, i == nt - 1), 0.0, 1.0)
    return keep_prev, keep_next


def _conv_gate_fwd(u, cw, cb, *, M, name):
    T, F2 = u.shape
    Fh = F2 // 2
    tm = _pick(T, (256,))
    tc = _pick(Fh, CONV_TC)
    nf = Fh // tc
    nt = T // tm
    first_lat = M // tm

    def body(ug, ugp, ugn, uv, uvp, uvn, wg, wv, bg, bv, o_ref):
        kp, kn = _seg_keep(pl.program_id(1), first_lat, nt)
        gc = _conv3(ug[...], ugp[...], ugn[...], wg[...], bg[...], kp, kn)
        vc = _conv3(uv[...], uvp[...], uvn[...], wv[...], bv[...], kp, kn)
        o_ref[...] = (_silu(gc) * vc).astype(o_ref.dtype)

    wspec = lambda off: pl.BlockSpec((3, tc), lambda j, i: (0, j + off))
    bspec = lambda off: pl.BlockSpec((1, tc), lambda j, i: (0, j + off))
    return pl.pallas_call(
        body, name=name, grid=(nf, nt),
        in_specs=_conv_specs(tm, tc, T, lambda j: j) + _conv_specs(tm, tc, T, lambda j: j + nf)
        + [wspec(0), wspec(nf), bspec(0), bspec(nf)],
        out_specs=pl.BlockSpec((tm, tc), lambda j, i: (i, j)),
        out_shape=jax.ShapeDtypeStruct((T, Fh), CDT),
        compiler_params=_cparams(("parallel", "parallel")),
    )(u, u, u, u, u, u, cw, cw, cb, cb)


def _conv_gate_bwd(u, dact, cw, cb, *, M, name):
    T, F2 = u.shape
    Fh = F2 // 2
    tm = _pick(T, (256,))
    tc = _pick(Fh, (512,))
    nf = Fh // tc
    nt = T // tm
    first_lat = M // tm

    def body(ug, ugp, ugn, uv, uvp, uvn, da, wg, wv, bg, bv, d_ref, acc_ref):
        i = pl.program_id(1)

        @pl.when(i == 0)
        def _():
            acc_ref[...] = jnp.zeros_like(acc_ref)

        kp, kn = _seg_keep(i, first_lat, nt)
        tg = _taps(ug[...], ugp[...], ugn[...], kp, kn)
        tv = _taps(uv[...], uvp[...], uvn[...], kp, kn)
        w = wg[...]
        gc = w[0:1, :] * tg[0] + w[1:2, :] * tg[1] + w[2:3, :] * tg[2] + bg[...]
        w = wv[...]
        vc = w[0:1, :] * tv[0] + w[1:2, :] * tv[1] + w[2:3, :] * tv[2] + bv[...]
        dav = da[...].astype(F32)
        for half, d, taps in ((0, dav * vc * _dsilu(gc), tg), (1, dav * _silu(gc), tv)):
            d_ref[half] = d.astype(d_ref.dtype)
            acc_ref[half, 0:1, :] += jnp.sum(d * taps[0], axis=0, keepdims=True)
            acc_ref[half, 1:2, :] += jnp.sum(d * taps[1], axis=0, keepdims=True)
            acc_ref[half, 2:3, :] += jnp.sum(d * taps[2], axis=0, keepdims=True)
            acc_ref[half, 3:4, :] += jnp.sum(d, axis=0, keepdims=True)

    wspec = lambda off: pl.BlockSpec((3, tc), lambda j, i: (0, j + off))
    bspec = lambda off: pl.BlockSpec((1, tc), lambda j, i: (0, j + off))
    return pl.pallas_call(
        body, name=name, grid=(nf, nt),
        in_specs=_conv_specs(tm, tc, T, lambda j: j) + _conv_specs(tm, tc, T, lambda j: j + nf)
        + [pl.BlockSpec((tm, tc), lambda j, i: (i, j)), wspec(0), wspec(nf), bspec(0), bspec(nf)],
        out_specs=(pl.BlockSpec((2, tm, tc), lambda j, i: (0, i, j)), pl.BlockSpec((2, 8, tc), lambda j, i: (0, 0, j))),
        out_shape=(jax.ShapeDtypeStruct((2, T, Fh), CDT), jax.ShapeDtypeStruct((2, 8, Fh), F32)),
        compiler_params=_cparams(("parallel", "arbitrary")),
    )(u, u, u, u, u, u, dact, cw, cw, cb, cb)


def _conv_t(d, cw_flipped, *, M, name):
    _, T, Fh = d.shape
    tm = _pick(T, (256,))
    tc = _pick(Fh, CONV_TC)
    nf = Fh // tc
    nt = T // tm
    first_lat = M // tm
    hb = tm // HALO
    last = T // HALO - 1

    def body(dc, dp, dn, w, o_ref):
        kp, kn = _seg_keep(pl.program_id(2), first_lat, nt)
        o_ref[...] = _conv3(dc[...], dp[...], dn[...], w[...], None, kp, kn).astype(o_ref.dtype)

    return pl.pallas_call(
        body, name=name, grid=(2, nf, nt),
        in_specs=[pl.BlockSpec((None, tm, tc), lambda g, j, i: (g, i, j)),
                  pl.BlockSpec((None, HALO, tc), lambda g, j, i: (g, jnp.maximum(i * hb - 1, 0), j)),
                  pl.BlockSpec((None, HALO, tc), lambda g, j, i: (g, jnp.minimum((i + 1) * hb, last), j)),
                  pl.BlockSpec((3, tc), lambda g, j, i: (0, g * nf + j))],
        out_specs=pl.BlockSpec((None, tm, tc), lambda g, j, i: (g, i, j)),
        out_shape=jax.ShapeDtypeStruct((2, T, Fh), CDT),
        compiler_params=_cparams(("parallel", "parallel", "parallel")),
    )(d, d, d, cw_flipped)


def _rope_tables(N, M, HD):
    ax = HD // 2
    pos = jnp.arange(N, dtype=jnp.int32)
    row = (pos // GRID_W).astype(F32)
    col = (pos % GRID_W).astype(F32)
    inv = ROPE_BASE ** (-jnp.arange(0, ax, 2, dtype=F32) / ax)
    ar = row[:, None] * inv[None, :]
    ac = col[:, None] * inv[None, :]
    cos = jnp.concatenate([jnp.cos(ar), jnp.cos(ar), jnp.cos(ac), jnp.cos(ac)], axis=1)
    sin = jnp.concatenate([-jnp.sin(ar), jnp.sin(ar), -jnp.sin(ac), jnp.sin(ac)], axis=1)
    cos = jnp.concatenate([jnp.ones((M, HD), F32), cos], axis=0)
    sin = jnp.concatenate([jnp.zeros((M, HD), F32), sin], axis=0)
    return cos, sin


def _pair_swap(x, nf):
    w = x.shape[1]
    lane = lax.broadcasted_iota(jnp.int32, x.shape, 1)
    first = (lane % (2 * nf)) < nf
    return jnp.where(first, pltpu.roll(x, w - nf, 1), pltpu.roll(x, nf, 1))


def _rope_fwd(qkv, cos, sin, *, QW, KW, HD, name):
    T = qkv.shape[0]
    tm = _pick(T, (256,))
    nf = HD // 4

    def body(qkv_ref, c_ref, s_ref, q_ref, k_ref, v_ref):
        c = c_ref[...]
        s = s_ref[...]
        for ref, off, w in ((q_ref, 0, QW), (k_ref, QW, KW)):
            xv = qkv_ref[:, off:off + w]
            ct = jnp.tile(c, (1, w // HD))
            st = jnp.tile(s, (1, w // HD))
            ref[...] = (xv * ct + _pair_swap(xv, nf) * st).astype(ref.dtype)
        v_ref[...] = qkv_ref[:, QW + KW:QW + 2 * KW].astype(v_ref.dtype)

    tspec = pl.BlockSpec((tm, HD), lambda i: (i, 0))
    return pl.pallas_call(
        body, name=name, grid=(T // tm,),
        in_specs=[pl.BlockSpec((tm, QW + 2 * KW), lambda i: (i, 0)), tspec, tspec],
        out_specs=(pl.BlockSpec((tm, QW), lambda i: (i, 0)), pl.BlockSpec((tm, KW), lambda i: (i, 0)),
                   pl.BlockSpec((tm, KW), lambda i: (i, 0))),
        out_shape=(jax.ShapeDtypeStruct((T, QW), CDT), jax.ShapeDtypeStruct((T, KW), CDT),
                   jax.ShapeDtypeStruct((T, KW), CDT)),
        compiler_params=_cparams(("parallel",)),
    )(qkv, cos, sin)


def _rope_bwd(dq, dk, dv, cos, sin, *, HD, name):
    T, QW = dq.shape
    KW = dk.shape[1]
    tm = _pick(T, (256,))
    nf = HD // 4

    def body(dq_ref, dk_ref, dv_ref, c_ref, s_ref, o_ref):
        c = c_ref[...]
        s = s_ref[...]
        for ref, off, w in ((dq_ref, 0, QW), (dk_ref, QW, KW)):
            g = ref[...].astype(F32)
            ct = jnp.tile(c, (1, w // HD))
            st = jnp.tile(s, (1, w // HD))
            o_ref[:, off:off + w] = (g * ct + _pair_swap(g * st, nf)).astype(o_ref.dtype)
        o_ref[:, QW + KW:QW + 2 * KW] = dv_ref[...].astype(o_ref.dtype)

    tspec = pl.BlockSpec((tm, HD), lambda i: (i, 0))
    return pl.pallas_call(
        body, name=name, grid=(T // tm,),
        in_specs=[pl.BlockSpec((tm, QW), lambda i: (i, 0)), pl.BlockSpec((tm, KW), lambda i: (i, 0)),
                  pl.BlockSpec((tm, KW), lambda i: (i, 0)), tspec, tspec],
        out_specs=pl.BlockSpec((tm, QW + 2 * KW), lambda i: (i, 0)),
        out_shape=jax.ShapeDtypeStruct((T, QW + 2 * KW), CDT),
        compiler_params=_cparams(("parallel",)),
    )(dq, dk, dv, cos, sin)


def _attn_scores(q_ref, kc_ref, kp_ref, kn_ref, kx_ref, sink_ref, i, *, M, HD, nblk, nbc):
    qs = jnp.concatenate([q_ref[:, g * HD:(g + 1) * HD] for g in range(ATT_G)], axis=0)
    kall = jnp.concatenate([kc_ref[...], kp_ref[...], kn_ref[...], kx_ref[...]], axis=0)
    s = _dg(qs, kall, NT) * (HD ** -0.5)
    shape = s.shape
    r = lax.broadcasted_iota(jnp.int32, shape, 0) % BLK
    c = lax.broadcasted_iota(jnp.int32, shape, 1) - M
    far = 4 * BLK
    lat_off = jnp.where(i >= nbc, 0, far)
    lo = jnp.maximum(r, jnp.where(i - 1 >= nbc, 0, BLK)) + lat_off
    hi = jnp.minimum(r + 2 * BLK, jnp.where(i + 1 < nblk, 3 * BLK - 1, 2 * BLK - 1))
    allowed = jnp.logical_or(c < 0, jnp.logical_and(c >= lo, c <= hi))
    s = jnp.where(allowed, s, -1e30)
    sink = sink_ref[0]
    m = jnp.maximum(jnp.max(s, axis=-1, keepdims=True), sink)
    e = jnp.exp(s - m)
    es = jnp.exp(sink - m)
    inv = 1.0 / (jnp.sum(e, axis=-1, keepdims=True) + es)
    return qs, kall, e * inv, es * inv


def _attn_specs(M, HD, nblk):
    kv_blk = lambda f: pl.BlockSpec((BLK, HD), lambda h, i: (f(i), h))
    ctx = pl.BlockSpec((M, HD), lambda h, i: (0, h))
    win = [kv_blk(lambda i: jnp.maximum(i - 1, 0)), kv_blk(lambda i: i), kv_blk(lambda i: jnp.minimum(i + 1, nblk - 1))]
    qspec = pl.BlockSpec((BLK, ATT_G * HD), lambda h, i: (i, h))
    sspec = pl.BlockSpec((1, ATT_G * BLK, 1), lambda h, i: (h, 0, 0))
    return qspec, [ctx] + win, sspec


def _attn_fwd(q, k, v, sink_col, *, M, name):
    T, QW = q.shape
    HD = QW // ATT_HEADS
    nblk = T // BLK
    nbc = M // BLK

    def body(q_ref, kc, kp, kn, kx, vc, vp, vn, vx, sink_ref, o_ref):
        i = pl.program_id(1)
        _, _, p, _ = _attn_scores(q_ref, kc, kp, kn, kx, sink_ref, i, M=M, HD=HD, nblk=nblk, nbc=nbc)
        vall = jnp.concatenate([vc[...], vp[...], vn[...], vx[...]], axis=0)
        o = _dg(p.astype(CDT), vall, NN)
        for g in range(ATT_G):
            o_ref[:, g * HD:(g + 1) * HD] = o[g * BLK:(g + 1) * BLK, :].astype(o_ref.dtype)

    qspec, kvs, sspec = _attn_specs(M, HD, nblk)
    return pl.pallas_call(
        body, name=name, grid=(ATT_KV, nblk),
        in_specs=[qspec] + kvs + kvs + [sspec],
        out_specs=qspec,
        out_shape=jax.ShapeDtypeStruct((T, QW), CDT),
        compiler_params=_cparams(("parallel", "parallel")),
    )(q, k, k, k, k, v, v, v, v, sink_col)


def _attn_bwd(q, k, v, sink_col, do, *, M, name):
    T, QW = q.shape
    HD = QW // ATT_HEADS
    KW = ATT_KV * HD
    nblk = T // BLK
    nbc = M // BLK

    def body(q_ref, kc, kp, kn, kx, vc, vp, vn, vx, sink_ref, do_ref, dq_ref, dkc_ref, dvc_ref, dkw_ref, dvw_ref, ds_ref):
        i = pl.program_id(1)

        @pl.when(i == 0)
        def _():
            dkc_ref[...] = jnp.zeros_like(dkc_ref)
            dvc_ref[...] = jnp.zeros_like(dvc_ref)
            ds_ref[...] = jnp.zeros_like(ds_ref)

        qs, kall, p, p_sink = _attn_scores(q_ref, kc, kp, kn, kx, sink_ref, i, M=M, HD=HD, nblk=nblk, nbc=nbc)
        vall = jnp.concatenate([vc[...], vp[...], vn[...], vx[...]], axis=0)
        dos = jnp.concatenate([do_ref[:, g * HD:(g + 1) * HD] for g in range(ATT_G)], axis=0)
        dp = _dg(dos, vall, NT)
        dsum = jnp.sum(p * dp, axis=-1, keepdims=True)
        dsc = (p * (dp - dsum) * (HD ** -0.5)).astype(CDT)
        dq = _dg(dsc, kall, NN)
        dkall = _dg(dsc, qs, TN)
        dvall = _dg(p.astype(CDT), dos, TN)
        for g in range(ATT_G):
            dq_ref[:, g * HD:(g + 1) * HD] = dq[g * BLK:(g + 1) * BLK, :].astype(dq_ref.dtype)
        dkc_ref[...] += dkall[0:M]
        dvc_ref[...] += dvall[0:M]
        dkw_ref[...] = dkall[M:]
        dvw_ref[...] = dvall[M:]
        ds_ref[0] += -(p_sink * dsum)

    qspec, kvs, sspec = _attn_specs(M, HD, nblk)
    ctx_out = pl.BlockSpec((M, HD), lambda h, i: (0, h))
    win_out = pl.BlockSpec((3 * BLK, HD), lambda h, i: (i, h))
    return pl.pallas_call(
        body, name=name, grid=(ATT_KV, nblk),
        in_specs=[qspec] + kvs + kvs + [sspec, qspec],
        out_specs=(qspec, ctx_out, ctx_out, win_out, win_out, sspec),
        out_shape=(jax.ShapeDtypeStruct((T, QW), CDT), jax.ShapeDtypeStruct((M, KW), F32), jax.ShapeDtypeStruct((M, KW), F32),
                   jax.ShapeDtypeStruct((nblk * 3 * BLK, KW), F32), jax.ShapeDtypeStruct((nblk * 3 * BLK, KW), F32),
                   jax.ShapeDtypeStruct((ATT_KV, ATT_G * BLK, 1), F32)),
        compiler_params=_cparams(("parallel", "arbitrary")),
    )(q, k, k, k, k, v, v, v, v, sink_col, do)


def _window_combine(part, *, nbc, name):
    rows, KW = part.shape
    nblk = rows // (3 * BLK)
    nbl = nblk - nbc

    def body(a_ref, b_ref, c_ref, o_ref):
        j = pl.program_id(0)
        o_ref[...] = (a_ref[...] * jnp.where(j + 1 < nbl, 1.0, 0.0) + b_ref[...]
                      + c_ref[...] * jnp.where(j >= 1, 1.0, 0.0))

    return pl.pallas_call(
        body, name=name, grid=(nbl,),
        in_specs=[pl.BlockSpec((BLK, KW), lambda j: (3 * jnp.minimum(nbc + j + 1, nblk - 1), 0)),
                  pl.BlockSpec((BLK, KW), lambda j: (3 * (nbc + j) + 1, 0)),
                  pl.BlockSpec((BLK, KW), lambda j: (3 * jnp.maximum(nbc + j - 1, 0) + 2, 0))],
        out_specs=pl.BlockSpec((BLK, KW), lambda j: (j, 0)),
        out_shape=jax.ShapeDtypeStruct((nbl * BLK, KW), F32),
        compiler_params=_cparams(("parallel",)),
    )(part, part, part)


def _split3(x):
    hi = x.astype(CDT)
    r1 = x - hi.astype(F32)
    mid = r1.astype(CDT)
    lo = (r1 - mid.astype(F32)).astype(CDT)
    return hi, mid, lo


def _tri_sum(tri, x, terms):
    parts = _split3(x)[:terms]
    out = _dg(tri, parts[0], NN)
    for p in parts[1:]:
        out = out + _dg(tri, p, NN)
    return out


def _gla_dims(D):
    dk = D // 2 // GLA_H
    dv = D // GLA_H
    return dk, dv


def _chunk_of(s, rev, ncc, ns):
    if not rev:
        return s
    return jnp.where(s < ncc, ncc - 1 - s, ns - 1 - (s - ncc))


def _gla_chunk(q_ref, k_ref, g_ref, rev, dk):
    C = q_ref.shape[0]
    r = lax.broadcasted_iota(jnp.int32, (C, C), 0)
    c = lax.broadcasted_iota(jnp.int32, (C, C), 1)
    causal = (r <= c) if rev else (r >= c)
    b = _tri_sum(causal.astype(CDT), g_ref[...], 3)
    B = b[0:1, :] if rev else b[C - 1:C, :]
    q = q_ref[...].astype(F32) * (dk ** -0.5)
    k = k_ref[...].astype(F32)
    return causal, b, B, q * jnp.exp(b), k * jnp.exp(-b), k * jnp.exp(B - b)


def _gla_scan_fwd(proj, g, *, rev, M, D, name):
    T = proj.shape[0]
    dk, dv = _gla_dims(D)
    C = CHUNK
    ns = T // C
    ncc = M // C
    koff = (D // 2) // dk
    voff = D // dv
    cm = lambda s: _chunk_of(s, rev, ncc, ns)

    def body(q_ref, k_ref, v_ref, g_ref, o_ref, st_ref, S):
        s = pl.program_id(1)

        @pl.when(s == 0)
        def _():
            S[...] = jnp.zeros_like(S)

        causal, b, B, qt, kt, kh = _gla_chunk(q_ref, k_ref, g_ref, rev, dk)
        v = v_ref[...]
        A = jnp.where(causal, _dg(qt.astype(CDT), kt.astype(CDT), NT), 0.0)
        Sin = S[...]
        st_ref[0] = Sin
        o_ref[...] = _dg(A.astype(CDT), v, NN) + _dg(qt.astype(CDT), Sin.astype(CDT), NT)
        S[...] = Sin * jnp.exp(B) + _dg(v, kh.astype(CDT), TN)

    return pl.pallas_call(
        body, name=name, grid=(GLA_H, ns),
        in_specs=[pl.BlockSpec((C, dk), lambda h, s: (cm(s), h)),
                  pl.BlockSpec((C, dk), lambda h, s: (cm(s), koff + h)),
                  pl.BlockSpec((C, dv), lambda h, s: (cm(s), voff + h)),
                  pl.BlockSpec((C, dk), lambda h, s: (cm(s), h))],
        out_specs=(pl.BlockSpec((C, dv), lambda h, s: (cm(s), h)),
                   pl.BlockSpec((1, dv, dk), lambda h, s: (h * ns + s, 0, 0))),
        out_shape=(jax.ShapeDtypeStruct((T, D), F32), jax.ShapeDtypeStruct((GLA_H * ns, dv, dk), F32)),
        scratch_shapes=[pltpu.VMEM((dv, dk), F32)],
        compiler_params=_cparams(("parallel", "arbitrary")),
    )(proj, proj, proj, g)


def _gla_scan_bwd(proj, g, st, do, *, rev, M, D, name):
    T = proj.shape[0]
    dk, dv = _gla_dims(D)
    C = CHUNK
    ns = T // C
    ncc = M // C
    koff = (D // 2) // dk
    voff = D // dv
    cm = lambda j: _chunk_of(ns - 1 - j, rev, ncc, ns)

    def body(q_ref, k_ref, v_ref, g_ref, st_ref, do_ref, dq_ref, dk_ref, dv_ref, dg_ref, dS):
        j = pl.program_id(1)

        @pl.when(j == 0)
        def _():
            dS[...] = jnp.zeros_like(dS)

        causal, b, B, qt, kt, kh = _gla_chunk(q_ref, k_ref, g_ref, rev, dk)
        v = v_ref[...]
        dov = do_ref[...]
        ST = st_ref[0]
        dSo = dS[...]
        qtb, ktb, khb = qt.astype(CDT), kt.astype(CDT), kh.astype(CDT)
        dSb = dSo.astype(CDT)
        A = jnp.where(causal, _dg(qtb, ktb, NT), 0.0).astype(CDT)
        dA = jnp.where(causal, _dg(dov, v, NT), 0.0).astype(CDT)
        dqt = _dg(dA, ktb, NN) + _dg(dov, ST.astype(CDT), NN)
        dkt = _dg(dA, qtb, TN)
        dvv = _dg(A, dov, TN) + _dg(khb, dSb, NT)
        dkh = _dg(v, dSb, NN)
        eB = jnp.exp(B)
        dB = eB * jnp.sum(ST * dSo, axis=0, keepdims=True) + jnp.sum(dkh * kh, axis=0, keepdims=True)
        rows = lax.broadcasted_iota(jnp.int32, (C, dk), 0)
        db = dqt * qt - dkt * kt - dkh * kh + jnp.where(rows == (0 if rev else C - 1), dB, 0.0)
        anti = jnp.logical_not(causal) | (lax.broadcasted_iota(jnp.int32, (C, C), 0) == lax.broadcasted_iota(jnp.int32, (C, C), 1))
        dg_ref[...] = _tri_sum(anti.astype(CDT), db, 2)
        dq_ref[...] = dqt * jnp.exp(b) * (dk ** -0.5)
        dk_ref[...] = dkt * jnp.exp(-b) + dkh * jnp.exp(B - b)
        dv_ref[...] = dvv
        dS[...] = dSo * eB + _dg(dov, qtb, TN)

    return pl.pallas_call(
        body, name=name, grid=(GLA_H, ns),
        in_specs=[pl.BlockSpec((C, dk), lambda h, j: (cm(j), h)),
                  pl.BlockSpec((C, dk), lambda h, j: (cm(j), koff + h)),
                  pl.BlockSpec((C, dv), lambda h, j: (cm(j), voff + h)),
                  pl.BlockSpec((C, dk), lambda h, j: (cm(j), h)),
                  pl.BlockSpec((1, dv, dk), lambda h, j: (h * ns + ns - 1 - j, 0, 0)),
                  pl.BlockSpec((C, dv), lambda h, j: (cm(j), h))],
        out_specs=(pl.BlockSpec((C, dk), lambda h, j: (cm(j), h)), pl.BlockSpec((C, dk), lambda h, j: (cm(j), h)),
                   pl.BlockSpec((C, dv), lambda h, j: (cm(j), h)), pl.BlockSpec((C, dk), lambda h, j: (cm(j), h))),
        out_shape=(jax.ShapeDtypeStruct((T, D // 2), F32), jax.ShapeDtypeStruct((T, D // 2), F32),
                   jax.ShapeDtypeStruct((T, D), F32), jax.ShapeDtypeStruct((T, D // 2), F32)),
        scratch_shapes=[pltpu.VMEM((dv, dk), F32)],
        compiler_params=_cparams(("parallel", "arbitrary")),
    )(proj, proj, proj, g, st, do)


def _log_sigmoid_parts(z):
    t = jnp.exp(-jnp.abs(z))
    return jnp.minimum(z, 0.0) - jnp.log(1.0 + t), jnp.where(z >= 0, t / (1.0 + t), 1.0 / (1.0 + t))


def _gla_gate_fwd(lr, w2, bias, *, D, name):
    T = lr.shape[0]
    tm = _pick(T, (256,))
    Dh = D // 2

    def body(lr_ref, w_ref, b_ref, gf_ref, gb_ref):
        z = _dg(lr_ref[...], w_ref[...], NN) + b_ref[...]
        g, _ = _log_sigmoid_parts(z)
        g = g * (1.0 / GATE_NORM)
        gf_ref[...] = g[:, 0:Dh]
        gb_ref[...] = g[:, Dh:D]

    half = pl.BlockSpec((tm, Dh), lambda i: (i, 0))
    return pl.pallas_call(
        body, name=name, grid=(T // tm,),
        in_specs=[pl.BlockSpec((tm, 128), lambda i: (i, 0)), pl.BlockSpec((128, D), lambda i: (0, 0)),
                  pl.BlockSpec((1, D), lambda i: (0, 0))],
        out_specs=(half, half),
        out_shape=(jax.ShapeDtypeStruct((T, Dh), F32), jax.ShapeDtypeStruct((T, Dh), F32)),
        compiler_params=_cparams(("parallel",)),
    )(lr, w2, bias)


def _gla_proj_bwd(lr, w2, bias, dqf, dkf, dvf, dgf, dqb, dkb, dvb, dgb, dr, *, D, name):
    T = lr.shape[0]
    tm = _pick(T, (256,))
    Dh = D // 2

    def body(lr_ref, w_ref, b_ref, dqf_r, dkf_r, dvf_r, dgf_r, dqb_r, dkb_r, dvb_r, dgb_r, dr_ref, dp_ref, dl_ref, dw_ref, db_ref):
        @pl.when(pl.program_id(0) == 0)
        def _():
            dw_ref[...] = jnp.zeros_like(dw_ref)
            db_ref[...] = jnp.zeros_like(db_ref)

        lr = lr_ref[...]
        z = _dg(lr, w_ref[...], NN) + b_ref[...]
        _, sneg = _log_sigmoid_parts(z)
        dz = jnp.concatenate([dgf_r[...], dgb_r[...]], axis=1) * sneg * (1.0 / GATE_NORM)
        dzb = dz.astype(CDT)
        dp_ref[:, 0:Dh] = (dqf_r[...] + dqb_r[...]).astype(dp_ref.dtype)
        dp_ref[:, Dh:D] = (dkf_r[...] + dkb_r[...]).astype(dp_ref.dtype)
        dp_ref[:, D:2 * D] = (dvf_r[...] + dvb_r[...]).astype(dp_ref.dtype)
        dp_ref[:, 2 * D:3 * D] = dr_ref[...]
        dl_ref[...] = _dg(dzb, w_ref[...], NT).astype(dl_ref.dtype)
        dw_ref[...] += _dg(lr, dzb, TN)
        db_ref[0:1, :] += jnp.sum(dz, axis=0, keepdims=True)

    half = pl.BlockSpec((tm, Dh), lambda i: (i, 0))
    full = pl.BlockSpec((tm, D), lambda i: (i, 0))
    return pl.pallas_call(
        body, name=name, grid=(T // tm,),
        in_specs=[pl.BlockSpec((tm, 128), lambda i: (i, 0)), pl.BlockSpec((128, D), lambda i: (0, 0)),
                  pl.BlockSpec((1, D), lambda i: (0, 0)), half, half, full, half, half, half, full, half, full],
        out_specs=(pl.BlockSpec((tm, 3 * D), lambda i: (i, 0)), pl.BlockSpec((tm, 128), lambda i: (i, 0)),
                   pl.BlockSpec((128, D), lambda i: (0, 0)), pl.BlockSpec((8, D), lambda i: (0, 0))),
        out_shape=(jax.ShapeDtypeStruct((T, 3 * D), CDT), jax.ShapeDtypeStruct((T, 128), CDT),
                   jax.ShapeDtypeStruct((128, D), F32), jax.ShapeDtypeStruct((8, D), F32)),
        compiler_params=_cparams(("arbitrary",)),
    )(lr, w2, bias, dqf, dkf, dvf, dgf, dqb, dkb, dvb, dgb, dr)


def _gla_out_fwd(of, ob, proj, gn, *, D, name):
    T = of.shape[0]
    tm = _pick(T, (256,))
    dv = D // GLA_H

    def body(of_ref, ob_ref, r_ref, g_ref, y_ref):
        for h in range(GLA_H):
            sl = slice(h * dv, (h + 1) * dv)
            o = of_ref[:, sl] + ob_ref[:, sl]
            rstd = lax.rsqrt(jnp.mean(o * o, axis=-1, keepdims=True) + EPS)
            y_ref[:, sl] = (o * rstd * g_ref[...] * _silu(r_ref[:, sl].astype(F32))).astype(y_ref.dtype)

    full = pl.BlockSpec((tm, D), lambda i: (i, 0))
    return pl.pallas_call(
        body, name=name, grid=(T // tm,),
        in_specs=[full, full, pl.BlockSpec((tm, D), lambda i: (i, 2)), pl.BlockSpec((1, dv), lambda i: (0, 0))],
        out_specs=full, out_shape=jax.ShapeDtypeStruct((T, D), CDT),
        compiler_params=_cparams(("parallel",)),
    )(of, ob, proj, gn)


def _gla_out_bwd(of, ob, proj, gn, dy, *, D, name):
    T = of.shape[0]
    tm = _pick(T, (256,))
    dv = D // GLA_H

    def body(of_ref, ob_ref, r_ref, g_ref, dy_ref, do_ref, dr_ref, dg_ref):
        @pl.when(pl.program_id(0) == 0)
        def _():
            dg_ref[...] = jnp.zeros_like(dg_ref)

        gv = g_ref[...]
        for h in range(GLA_H):
            sl = slice(h * dv, (h + 1) * dv)
            o = of_ref[:, sl] + ob_ref[:, sl]
            rstd = lax.rsqrt(jnp.mean(o * o, axis=-1, keepdims=True) + EPS)
            oh = o * rstd
            r = r_ref[:, sl].astype(F32)
            dyv = dy_ref[:, sl].astype(F32)
            don = dyv * _silu(r)
            dr_ref[:, sl] = (dyv * oh * gv * _dsilu(r)).astype(dr_ref.dtype)
            dg_ref[0:1, :] += jnp.sum(don * oh, axis=0, keepdims=True)
            dn = don * gv
            do_ref[:, sl] = (rstd * (dn - oh * jnp.mean(dn * oh, axis=-1, keepdims=True))).astype(do_ref.dtype)

    full = pl.BlockSpec((tm, D), lambda i: (i, 0))
    return pl.pallas_call(
        body, name=name, grid=(T // tm,),
        in_specs=[full, full, pl.BlockSpec((tm, D), lambda i: (i, 2)), pl.BlockSpec((1, dv), lambda i: (0, 0)), full],
        out_specs=(full, full, pl.BlockSpec((8, dv), lambda i: (0, 0))),
        out_shape=(jax.ShapeDtypeStruct((T, D), CDT), jax.ShapeDtypeStruct((T, D), CDT), jax.ShapeDtypeStruct((8, dv), F32)),
        compiler_params=_cparams(("arbitrary",)),
    )(of, ob, proj, gn, dy)


def _adamw(w, g, m, v, *, name):
    R, Cc = w.shape
    tr = R
    for cand in (512, 256, 128, 64, 32, 16, 8):
        if R % cand == 0 and cand * Cc * 4 <= 2 * 1024 * 1024:
            tr = cand
            break

    def body(w_ref, g_ref, m_ref, v_ref, d_ref, mo_ref, vo_ref):
        gv = g_ref[...]
        mn = B1 * m_ref[...] + (1.0 - B1) * gv
        vn = B2 * v_ref[...] + (1.0 - B2) * (gv * gv)
        mh = mn / (1.0 - B1 ** STEP)
        vh = vn / (1.0 - B2 ** STEP)
        d_ref[...] = -LR * (mh / (jnp.sqrt(vh) + AEPS) + WD * w_ref[...])
        mo_ref[...] = mn
        vo_ref[...] = vn

    spec = pl.BlockSpec((tr, Cc), lambda i: (i, 0))
    sh = jax.ShapeDtypeStruct((R, Cc), F32)
    return pl.pallas_call(
        body, name=name, grid=(R // tr,), in_specs=[spec] * 4, out_specs=(spec,) * 3, out_shape=(sh,) * 3,
        compiler_params=_cparams(("parallel",)),
    )(w, g, m, v)


def _attn_layer_fwd(h, w, tabs, M, tag):
    cos, sin = tabs
    QW = 4 * w["w_o"].shape[1]
    HD = QW // ATT_HEADS
    KW = ATT_KV * HD
    qkv = _mm(h, w["w_qkv"], b_chip=1, out_dtype=F32, name=f"{tag}_qkv")
    q, k, v = _rope_fwd(qkv, cos, sin, QW=QW, KW=KW, HD=HD, name=f"{tag}_rope")
    sink_col = jnp.repeat(w["sink"].astype(F32), BLK).reshape(ATT_KV, ATT_G * BLK, 1)
    o = _attn_fwd(q, k, v, sink_col, M=M, name=f"{tag}_attn")
    y = _mm(o, w["w_o"], b_chip=0, out_dtype=CDT, name=f"{tag}_wo")
    return y, dict(h=h, q=q, k=k, v=v, o=o, sink_col=sink_col)


def _attn_layer_bwd(dy, sv, w, tabs, M, tag):
    cos, sin = tabs
    QW = 4 * w["w_o"].shape[1]
    HD = QW // ATT_HEADS
    do = _mm(dy, w["w_o"], tb=True, b_chip=0, out_dtype=CDT, name=f"{tag}_dwo_x")
    g = {"w_o": _mm(sv["o"], dy, ta=True, out_chip=0, out_dtype=CDT, name=f"{tag}_dwo_w")}
    dq, dkc, dvc, dkw, dvw, dsink = _attn_bwd(sv["q"], sv["k"], sv["v"], sv["sink_col"], do, M=M, name=f"{tag}_attn_bwd")
    nbc = M // BLK
    dk = jnp.concatenate([dkc, _window_combine(dkw, nbc=nbc, name=f"{tag}_dk_comb")], axis=0)
    dv = jnp.concatenate([dvc, _window_combine(dvw, nbc=nbc, name=f"{tag}_dv_comb")], axis=0)
    dqkv = _rope_bwd(dq, dk, dv, cos, sin, HD=HD, name=f"{tag}_rope_bwd")
    dh = _mm(dqkv, w["w_qkv"], tb=True, b_chip=1, out_dtype=CDT, name=f"{tag}_dqkv_x")
    g["w_qkv"] = _mm(sv["h"], dqkv, ta=True, out_chip=1, out_dtype=CDT, name=f"{tag}_dqkv_w")
    g["sink"] = jnp.sum(dsink.reshape(ATT_HEADS, BLK), axis=1)
    return dh, g


def _gla_layer_fwd(h, w, M, tag):
    D = h.shape[1]
    proj = _mm(h, w["w_in"], b_chip=1, out_dtype=CDT, name=f"{tag}_in")
    lr = _mm(h, w["w1x"], out_dtype=CDT, name=f"{tag}_in_gate")
    gf, gb = _gla_gate_fwd(lr, w["w2"], w["gbias"], D=D, name=f"{tag}_gate")
    of, stf = _gla_scan_fwd(proj, gf, rev=False, M=M, D=D, name=f"{tag}_scan_f")
    ob, stb = _gla_scan_fwd(proj, gb, rev=True, M=M, D=D, name=f"{tag}_scan_b")
    yg = _gla_out_fwd(of, ob, proj, w["onorm"], D=D, name=f"{tag}_out")
    y = _mm(yg, w["w_o"], b_chip=0, out_dtype=CDT, name=f"{tag}_wo")
    return y, dict(h=h, proj=proj, lr=lr, gf=gf, gb=gb, of=of, ob=ob, stf=stf, stb=stb, yg=yg)


def _gla_layer_bwd(dy, sv, w, M, tag):
    D = dy.shape[1]
    dyg = _mm(dy, w["w_o"], tb=True, b_chip=0, out_dtype=CDT, name=f"{tag}_dwo_x")
    g = {"w_o": _mm(sv["yg"], dy, ta=True, out_chip=0, out_dtype=CDT, name=f"{tag}_dwo_w")}
    do, dr, dgn = _gla_out_bwd(sv["of"], sv["ob"], sv["proj"], w["onorm"], dyg, D=D, name=f"{tag}_out_bwd")
    df = _gla_scan_bwd(sv["proj"], sv["gf"], sv["stf"], do, rev=False, M=M, D=D, name=f"{tag}_scan_f_bwd")
    db = _gla_scan_bwd(sv["proj"], sv["gb"], sv["stb"], do, rev=True, M=M, D=D, name=f"{tag}_scan_b_bwd")
    dproj, dlr, dw2, dbias = _gla_proj_bwd(sv["lr"], w["w2"], w["gbias"], *df, *db, dr, D=D, name=f"{tag}_proj_bwd")
    dh = _mm(dproj, w["w_in"], tb=True, b_chip=1, extra=(dlr, w["w1x"]), out_dtype=CDT, name=f"{tag}_din_x")
    g["w_in"] = _mm(sv["h"], dproj, ta=True, out_chip=1, out_dtype=CDT, name=f"{tag}_din_w")
    g["w1x"] = _mm(sv["h"], dlr, ta=True, out_dtype=F32, name=f"{tag}_din_gate_w")
    g["w2"] = dw2
    g["gbias"] = dbias[0]
    g["onorm"] = dgn[0]
    return dh, g


def _ffn_fwd(h2, w, M, tag):
    u = _mm(h2, w["w_up"], b_chip=1, out_dtype=CDT, name=f"{tag}_up")
    act = _conv_gate_fwd(u, w["conv_w"], w["conv_b"], M=M, name=f"{tag}_conv")
    f = _mm(act, w["w_down"], b_chip=0, out_dtype=CDT, name=f"{tag}_down")
    return f, dict(h2=h2, u=u, act=act)


def _ffn_bwd(dyf, sv, w, M, tag):
    dact = _mm(dyf, w["w_down"], tb=True, b_chip=0, out_dtype=CDT, name=f"{tag}_ddown_x")
    g = {"w_down": _mm(sv["act"], dyf, ta=True, out_chip=0, out_dtype=CDT, name=f"{tag}_ddown_w")}
    duc, cacc = _conv_gate_bwd(sv["u"], dact, w["conv_w"], w["conv_b"], M=M, name=f"{tag}_conv_bwd")
    du = _conv_t(duc, w["conv_w"][::-1], M=M, name=f"{tag}_conv_t")
    dh2 = _mm(du, w["w_up"], tb=True, a_split=True, b_chip=1, out_dtype=CDT, name=f"{tag}_dup_x")
    g["w_up"] = _mm(sv["h2"], du, ta=True, b_chip=1, out_chip=1, out_dtype=CDT, name=f"{tag}_dup_w")
    g["conv_w"] = jnp.concatenate([cacc[0, 0:3], cacc[1, 0:3]], axis=1)
    g["conv_b"] = jnp.concatenate([cacc[0, 3], cacc[1, 3]], axis=0)
    return dh2, g


def _norm_grads(acc, gain, scale):
    p = acc[:, 1]
    return acc[:, 0], p * gain, jnp.sum(p * (1.0 + scale[:, 0]), axis=0)


def _local_step(x, tgt, mods, lw, final_g, *, M, on_grads=None):
    T, D = x.shape
    L = len(lw)
    HD = D // ATT_HEADS
    tabs = _rope_tables(T - M, M, HD)
    sel = lambda i, k: mods[i][:, k:k + 1, :]
    saved = []
    xs, y_prev, gate_prev = x, None, None
    for i in range(L):
        w = lw[i]
        x_in, h = _norm_fwd(xs, y_prev, gate_prev, w["g_mix"], sel(i, 0), sel(i, 1), M=M, name=f"l{i}_norm_mix")
        if "w_qkv" in w:
            y_mix, sm = _attn_layer_fwd(h, w, tabs, M, f"l{i}")
        else:
            y_mix, sm = _gla_layer_fwd(h, w, M, f"l{i}")
        x_mid, h2 = _norm_fwd(x_in, y_mix, sel(i, 2), w["g_ffn"], sel(i, 3), sel(i, 4), M=M, name=f"l{i}_norm_ffn")
        f, sf = _ffn_fwd(h2, w, M, f"l{i}")
        saved.append(dict(x_in=x_in, x_mid=x_mid, y_mix=y_mix, f=f, sm=sm, sf=sf))
        xs, y_prev, gate_prev = x_mid, f, sel(i, 5)

    loss_parts, dx, dyf, acc = _final_loss(xs, y_prev, gate_prev, tgt, final_g, M=M, name="final_loss")
    loss = jnp.sum(loss_parts[:, 0, 0])
    d_final_g = acc[0, 0] + acc[1, 0]
    dmods = [None] * L
    grads = [None] * L
    dgate_ffn = acc[:, 2]
    for i in reversed(range(L)):
        w, sv = lw[i], saved[i]
        dh2, g = _ffn_bwd(dyf, sv["sf"], w, M, f"l{i}")
        dx, dy_mix, acc = _norm_bwd(sv["x_mid"], dh2, dx, w["g_ffn"], sel(i, 4), sv["y_mix"], sel(i, 2), M=M, name=f"l{i}_norm_ffn_bwd")
        dsh_f, dsc_f, g["g_ffn"] = _norm_grads(acc, w["g_ffn"], sel(i, 4))
        dgate_mix = acc[:, 2]
        if "w_qkv" in w:
            dh, gm = _attn_layer_bwd(dy_mix, sv["sm"], w, tabs, M, f"l{i}")
        else:
            dh, gm = _gla_layer_bwd(dy_mix, sv["sm"], w, M, f"l{i}")
        g.update(gm)
        if i > 0:
            dx, dyf, acc = _norm_bwd(sv["x_in"], dh, dx, w["g_mix"], sel(i, 1), saved[i - 1]["f"], sel(i - 1, 5), M=M, name=f"l{i}_norm_mix_bwd")
        else:
            dx, dyf, acc = _norm_bwd(sv["x_in"], dh, dx, w["g_mix"], sel(i, 1), None, None, M=M, name=f"l{i}_norm_mix_bwd")
        dsh_m, dsc_m, g["g_mix"] = _norm_grads(acc, w["g_mix"], sel(i, 1))
        dmods[i] = jnp.stack([dsh_m, dsc_m, dgate_mix, dsh_f, dsc_f, dgate_ffn], axis=1)
        dgate_ffn = acc[:, 2]
        grads[i] = g if on_grads is None else on_grads(i, g)
    return loss, dx, jnp.stack(dmods, axis=0), grads, d_final_g


ANY = pl.BlockSpec(memory_space=pl.ANY)


def _me():
    return lax.axis_index("x"), lax.axis_index("y"), lax.axis_index("c")


def _other_chips(mx, my):
    return [(1 - mx, my), (mx, 1 - my), (1 - mx, 1 - my)]


def _rcopy(src, dst, sems, k, dev):
    send_sems, recv_sems = sems
    return pltpu.make_async_remote_copy(src_ref=src, dst_ref=dst, send_sem=send_sems.at[k], recv_sem=recv_sems.at[k],
                                        device_id=dev, device_id_type=MESH)


def _all_gather8(x, *, name):
    m, n = x.shape

    def body(x_ref, out_ref, send_sems, recv_sems, local_sem):
        mx, my, mc = _me()
        sems = (send_sems, recv_sems)
        me, sib = (mx, my, mc), (mx, my, 1 - mc)
        chips = _other_chips(mx, my)
        blk = lambda d: out_ref.at[4 * d[0] + 2 * d[1] + d[2]]
        mine = pltpu.make_async_copy(x_ref, blk(me), local_sem)
        mine.start()
        first = [_rcopy(x_ref, blk(me), sems, 0, sib)]
        first += [_rcopy(x_ref, blk(me), sems, 1 + j, (*ch, mc)) for j, ch in enumerate(chips)]
        for cp in first:
            cp.start()
        passed = [_rcopy(blk((*ch, mc)), blk((*ch, mc)), sems, 4 + j, sib) for j, ch in enumerate(chips)]
        for j, ch in enumerate(chips):
            _rcopy(x_ref, blk((*ch, mc)), sems, 1 + j, me).wait_recv()
            passed[j].start()
        _rcopy(x_ref, blk(sib), sems, 0, me).wait_recv()
        for j, ch in enumerate(chips):
            _rcopy(x_ref, blk((*ch, 1 - mc)), sems, 4 + j, me).wait_recv()
        for cp in first + passed:
            cp.wait_send()
        mine.wait()

    return pl.pallas_call(
        body, name=name, out_shape=jax.ShapeDtypeStruct((8, m, n), x.dtype), in_specs=[ANY], out_specs=ANY,
        scratch_shapes=[pltpu.SemaphoreType.DMA((7,)), pltpu.SemaphoreType.DMA((7,)), pltpu.SemaphoreType.DMA],
    )(x)


ROW_TILES = (512, 352, 256, 128)


def _sem_pairs(n):
    return [pltpu.SemaphoreType.DMA((n,)), pltpu.SemaphoreType.DMA((n,))]


def _place(w, layer, pos, *, name):
    _, a, b = w.shape
    tr = _pick(a, ROW_TILES)

    def body(pos_ref, w_ref, o_ref):
        o_ref[...] = w_ref[...].astype(o_ref.dtype)

    return pl.pallas_call(
        body, name=name, out_shape=jax.ShapeDtypeStruct((4, a, b), CDT),
        grid_spec=pltpu.PrefetchScalarGridSpec(
            num_scalar_prefetch=1, grid=(a // tr,),
            in_specs=[pl.BlockSpec((None, tr, b), lambda i, pos: (layer, i, 0))],
            out_specs=pl.BlockSpec((None, tr, b), lambda i, pos: (pos[0], i, 0))),
        compiler_params=_cparams(("parallel",)),
    )(pos, w)


def _gather_layer(bufs, *, name):
    n = len(bufs)

    def body(*refs):
        outs = refs[n:2 * n]
        sems = (refs[2 * n], refs[2 * n + 1])
        mx, my, mc = _me()
        me, sib = (mx, my, mc), (mx, my, 1 - mc)
        chips = _other_chips(mx, my)
        p = 2 * mx + my
        qs = [2 * ch[0] + ch[1] for ch in chips]
        halves = [(pl.ds(mc * (o.shape[1] // 2), o.shape[1] // 2), pl.ds((1 - mc) * (o.shape[1] // 2), o.shape[1] // 2)) for o in outs]
        first = []
        for t, o in enumerate(outs):
            mine = halves[t][0]
            first += [_rcopy(o.at[p, mine], o.at[p, mine], sems, 6 * t + j, (*ch, mc)) for j, ch in enumerate(chips)]
        for cp in first:
            cp.start()
        passed = []
        for j in range(3):
            for t, o in enumerate(outs):
                mine = halves[t][0]
                _rcopy(o.at[qs[j], mine], o.at[qs[j], mine], sems, 6 * t + j, me).wait_recv()
                fwd = _rcopy(o.at[qs[j], mine], o.at[qs[j], mine], sems, 6 * t + 3 + j, sib)
                fwd.start()
                passed.append(fwd)
        for j in range(3):
            for t, o in enumerate(outs):
                theirs = halves[t][1]
                _rcopy(o.at[qs[j], theirs], o.at[qs[j], theirs], sems, 6 * t + 3 + j, me).wait_recv()
        for cp in first + passed:
            cp.wait_send()

    return pl.pallas_call(
        body, name=name, out_shape=[jax.ShapeDtypeStruct(b.shape, b.dtype) for b in bufs],
        in_specs=[ANY] * n, out_specs=[ANY] * n, input_output_aliases={t: t for t in range(n)},
        scratch_shapes=_sem_pairs(6 * n),
    )(*bufs)


def _rs_split(gs, *, name):
    n = len(gs)

    def body(*refs):
        ins, outs = refs[:n], refs[n:2 * n]
        sems = (refs[2 * n], refs[2 * n + 1])
        mx, my, mc = _me()
        sib = (mx, my, 1 - mc)
        cps = []
        for t, (g, o) in enumerate(zip(ins, outs)):
            ah = g.shape[1] // 2
            cps.append(_rcopy(g.at[:, pl.ds((1 - mc) * ah, ah), :], o, sems, t, sib))
        for cp in cps:
            cp.start()
        for cp in cps:
            cp.wait_recv()
        for cp in cps:
            cp.wait_send()

    return pl.pallas_call(
        body, name=name, out_shape=[jax.ShapeDtypeStruct((4, g.shape[1] // 2, g.shape[2]), g.dtype) for g in gs],
        in_specs=[ANY] * n, out_specs=[ANY] * n, scratch_shapes=_sem_pairs(n),
    )(*gs)


def _rs_add(g, got, pos, *, name):
    _, a, b = g.shape
    ah = a // 2
    tr = _pick(ah, ROW_TILES)
    nb = ah // tr

    def body(pos_ref, g_ref, r_ref, o_ref):
        o_ref[...] = (g_ref[...].astype(F32) + r_ref[...].astype(F32)).astype(o_ref.dtype)

    blk = pl.BlockSpec((None, tr, b), lambda q, i, pos: (q, i, 0))
    return pl.pallas_call(
        body, name=name, out_shape=jax.ShapeDtypeStruct((4, ah, b), g.dtype),
        grid_spec=pltpu.PrefetchScalarGridSpec(
            num_scalar_prefetch=1, grid=(4, nb),
            in_specs=[pl.BlockSpec((None, tr, b), lambda q, i, pos: (q, pos[1] * nb + i, 0)), blk], out_specs=blk),
        compiler_params=_cparams(("parallel", "parallel")),
    )(pos, g, got)


def _rs_scatter(ps, *, name):
    n = len(ps)

    def body(*refs):
        ins, outs = refs[:n], refs[n:2 * n]
        sems = (refs[2 * n], refs[2 * n + 1])
        mx, my, mc = _me()
        me = (mx, my, mc)
        chips = _other_chips(mx, my)
        p = 2 * mx + my
        sends = []
        for t, (s, o) in enumerate(zip(ins, outs)):
            sends += [_rcopy(s.at[2 * ch[0] + ch[1]], o.at[p], sems, 3 * t + j, (*ch, mc)) for j, ch in enumerate(chips)]
        for cp in sends:
            cp.start()
        for t, (s, o) in enumerate(zip(ins, outs)):
            for j, ch in enumerate(chips):
                _rcopy(s.at[p], o.at[2 * ch[0] + ch[1]], sems, 3 * t + j, me).wait_recv()
        for cp in sends:
            cp.wait_send()

    return pl.pallas_call(
        body, name=name, out_shape=[jax.ShapeDtypeStruct(s.shape, s.dtype) for s in ps],
        in_specs=[ANY] * n, out_specs=[ANY] * n, scratch_shapes=_sem_pairs(3 * n),
    )(*ps)


def _rs_sum(part, recv, buf, layer, pos, *, name):
    _, ah, b = part.shape
    tr = _pick(ah, ROW_TILES)
    nb = ah // tr

    def body(pos_ref, p_ref, r0, r1, r2, buf_ref, o_ref):
        o_ref[...] = ((p_ref[...].astype(F32) + r0[...].astype(F32)) + r1[...].astype(F32)) + r2[...].astype(F32)

    other = lambda k: pl.BlockSpec((None, tr, b), lambda i, pos: (jnp.where(pos[0] <= k, k + 1, k), i, 0))
    return pl.pallas_call(
        body, name=name, out_shape=jax.ShapeDtypeStruct(buf.shape, buf.dtype),
        grid_spec=pltpu.PrefetchScalarGridSpec(
            num_scalar_prefetch=1, grid=(nb,),
            in_specs=[pl.BlockSpec((None, tr, b), lambda i, pos: (pos[0], i, 0)), other(0), other(1), other(2), ANY],
            out_specs=pl.BlockSpec((None, tr, b), lambda i, pos: (layer, pos[1] * nb + i, 0))),
        input_output_aliases={5: 0},
        compiler_params=_cparams(("parallel",)),
    )(pos, part, recv, recv, recv, buf)


def _rs_share(bufs, layers, *, name):
    n = len(bufs)

    def body(*refs):
        outs = refs[n:2 * n]
        sems = (refs[2 * n], refs[2 * n + 1])
        mx, my, mc = _me()
        sib = (mx, my, 1 - mc)
        cps = []
        for t, o in enumerate(outs):
            ah = o.shape[1] // 2
            mine = o.at[layers[t], pl.ds(mc * ah, ah)]
            cps.append((_rcopy(mine, mine, sems, t, sib), o.at[layers[t], pl.ds((1 - mc) * ah, ah)]))
        for cp, _ in cps:
            cp.start()
        for t, (cp, theirs) in enumerate(cps):
            _rcopy(theirs, theirs, sems, t, sib).wait_recv()
        for cp, _ in cps:
            cp.wait_send()

    return pl.pallas_call(
        body, name=name, out_shape=[jax.ShapeDtypeStruct(b.shape, b.dtype) for b in bufs],
        in_specs=[ANY] * n, out_specs=[ANY] * n, input_output_aliases={t: t for t in range(n)},
        scratch_shapes=_sem_pairs(n),
    )(*bufs)


def _sum_lead(a, *, name):
    n, R, W = a.shape
    tr = _pick(R, (PACK_ROWS,))
    specs = [pl.BlockSpec((1, tr, W), functools.partial(lambda i, q: (q, i, 0), q=q)) for q in range(n)]

    def body(*refs):
        acc = refs[0][0].astype(F32)
        for r in refs[1:n]:
            acc = acc + r[0].astype(F32)
        refs[n][...] = acc

    return pl.pallas_call(
        body, name=name, grid=(R // tr,), in_specs=specs, out_specs=pl.BlockSpec((tr, W), lambda i: (i, 0)),
        out_shape=jax.ShapeDtypeStruct((R, W), F32), compiler_params=_cparams(("parallel",)),
    )(*([a] * n))


SMALL_SHARDED = (("ffn_conv_w", 2), ("gla_gf_w1", 1), ("gla_gf_w2", 2), ("gla_gf_b", 1), ("gla_gb_w1", 1), ("gla_gb_w2", 2),
                 ("gla_gb_b", 1), ("gla_onorm_g", 1))


def _rows_of(flat, width):
    rows = -(-flat.shape[0] // (8 * width)) * 8
    return jnp.pad(flat, (0, rows * width - flat.shape[0])).reshape(rows, width)


def _size(shape):
    n = 1
    for s in shape:
        n *= s
    return n


def _gather_small(shards):
    flat = jnp.concatenate([shards[name].astype(F32).reshape(-1) for name, _ in SMALL_SHARDED])
    got = _all_gather8(_rows_of(flat, SMALL_W), name="gather_small_w")[0::2].reshape(4, -1)
    full, off = {}, 0
    for name, ax in SMALL_SHARDED:
        shape = shards[name].shape
        n = _size(shape)
        seg = jnp.moveaxis(got[:, off:off + n].reshape((4,) + shape), 0, ax)
        full[name] = seg.reshape(shape[:ax] + (4 * shape[ax],) + shape[ax + 1:])
        off += n
    return full


WEIGHTS = ("c_ctx", "ada_w", "ada_b", "norm_mix_g", "norm_ffn_g", "ffn_w_up", "ffn_conv_w", "ffn_conv_b", "ffn_w_down",
           "attn_w_qkv", "attn_sink", "attn_w_o", "gla_w_in", "gla_gf_w1", "gla_gf_w2", "gla_gf_b", "gla_gb_w1", "gla_gb_w2",
           "gla_gb_b", "gla_onorm_g", "gla_w_o", "final_norm_g")
REPLICATED = ("norm_mix_g", "norm_ffn_g", "ffn_conv_b", "attn_sink", "final_norm_g", "c_ctx")
SMALL_W = 2048
ROWS16 = 16


def _layer_big(i):
    j = i // 2
    mixer = [("w_qkv", "attn_w_qkv", j), ("w_o", "attn_w_o", j)] if i % 2 == 0 else [("w_in", "gla_w_in", j), ("w_o", "gla_w_o", j)]
    return [("w_up", "ffn_w_up", i), ("w_down", "ffn_w_down", i)] + mixer


def _layer_weights(i, big, small, rep):
    D = rep["norm_mix_g"].shape[1]
    j = i // 2
    w = dict(g_mix=rep["norm_mix_g"][i][None], g_ffn=rep["norm_ffn_g"][i][None], conv_w=small["ffn_conv_w"][i],
             conv_b=rep["ffn_conv_b"][i][None], **big)
    if i % 2 == 0:
        w["sink"] = rep["attn_sink"][j]
    else:
        r = GATE_RANK
        w2 = jnp.zeros((128, D), F32)
        w2 = w2.at[0:r, 0:D // 2].set(small["gla_gf_w2"][j]).at[r:2 * r, D // 2:].set(small["gla_gb_w2"][j])
        w1x = jnp.concatenate([small["gla_gf_w1"][j], small["gla_gb_w1"][j], jnp.zeros((D, 128 - 2 * r), F32)], axis=1)
        w.update(w1x=w1x.astype(CDT), w2=w2.astype(CDT), gbias=jnp.concatenate([small["gla_gf_b"][j], small["gla_gb_b"][j]])[None],
                 onorm=small["gla_onorm_g"][j][None])
    return w


def _small_grads(grads, D):
    att = [g for g in grads if "sink" in g]
    gla = [g for g in grads if "w1x" in g]
    st = lambda xs: jnp.stack(xs, axis=0)
    r = GATE_RANK
    return {
        "ffn_conv_w": st([g["conv_w"] for g in grads]),
        "gla_gf_w1": st([g["w1x"][:, 0:r] for g in gla]), "gla_gb_w1": st([g["w1x"][:, r:2 * r] for g in gla]),
        "gla_gf_w2": st([g["w2"][0:r, :D // 2] for g in gla]), "gla_gb_w2": st([g["w2"][r:2 * r, D // 2:] for g in gla]),
        "gla_gf_b": st([g["gbias"][:D // 2] for g in gla]), "gla_gb_b": st([g["gbias"][D // 2:] for g in gla]),
        "gla_onorm_g": st([g["onorm"] for g in gla]),
        "norm_mix_g": st([g["g_mix"] for g in grads]), "norm_ffn_g": st([g["g_ffn"] for g in grads]),
        "ffn_conv_b": st([g["conv_b"] for g in grads]), "attn_sink": st([g["sink"] for g in att]),
    }


def kernel(x, c, ctx, c_ctx, ada_w, ada_b, norm_mix_g, norm_ffn_g, ffn_w_up, ffn_conv_w, ffn_conv_b, ffn_w_down, attn_w_qkv, attn_sink, attn_w_o, gla_w_in, gla_gf_w1, gla_gf_w2, gla_gf_b, gla_gb_w1, gla_gb_w2, gla_gb_b, gla_onorm_g, gla_w_o, final_norm_g, loss_target, m_c_ctx, m_ada_w, m_ada_b, m_norm_mix_g, m_norm_ffn_g, m_ffn_w_up, m_ffn_conv_w, m_ffn_conv_b, m_ffn_w_down, m_attn_w_qkv, m_attn_sink, m_attn_w_o, m_gla_w_in, m_gla_gf_w1, m_gla_gf_w2, m_gla_gf_b, m_gla_gb_w1, m_gla_gb_w2, m_gla_gb_b, m_gla_onorm_g, m_gla_w_o, m_final_norm_g, v_c_ctx, v_ada_w, v_ada_b, v_norm_mix_g, v_norm_ffn_g, v_ffn_w_up, v_ffn_conv_w, v_ffn_conv_b, v_ffn_w_down, v_attn_w_qkv, v_attn_sink, v_attn_w_o, v_gla_w_in, v_gla_gf_w1, v_gla_gf_w2, v_gla_gf_b, v_gla_gb_w1, v_gla_gb_w2, v_gla_gb_b, v_gla_onorm_g, v_gla_w_o, v_final_norm_g):
    wts = dict(c_ctx=c_ctx, ada_w=ada_w, ada_b=ada_b, norm_mix_g=norm_mix_g, norm_ffn_g=norm_ffn_g, ffn_w_up=ffn_w_up,
               ffn_conv_w=ffn_conv_w, ffn_conv_b=ffn_conv_b, ffn_w_down=ffn_w_down, attn_w_qkv=attn_w_qkv, attn_sink=attn_sink,
               attn_w_o=attn_w_o, gla_w_in=gla_w_in, gla_gf_w1=gla_gf_w1, gla_gf_w2=gla_gf_w2, gla_gf_b=gla_gf_b,
               gla_gb_w1=gla_gb_w1, gla_gb_w2=gla_gb_w2, gla_gb_b=gla_gb_b, gla_onorm_g=gla_onorm_g, gla_w_o=gla_w_o,
               final_norm_g=final_norm_g)
    mom_m = dict(zip(WEIGHTS, (m_c_ctx, m_ada_w, m_ada_b, m_norm_mix_g, m_norm_ffn_g, m_ffn_w_up, m_ffn_conv_w, m_ffn_conv_b,
                               m_ffn_w_down, m_attn_w_qkv, m_attn_sink, m_attn_w_o, m_gla_w_in, m_gla_gf_w1, m_gla_gf_w2,
                               m_gla_gf_b, m_gla_gb_w1, m_gla_gb_w2, m_gla_gb_b, m_gla_onorm_g, m_gla_w_o, m_final_norm_g)))
    mom_v = dict(zip(WEIGHTS, (v_c_ctx, v_ada_w, v_ada_b, v_norm_mix_g, v_norm_ffn_g, v_ffn_w_up, v_ffn_conv_w, v_ffn_conv_b,
                               v_ffn_w_down, v_attn_w_qkv, v_attn_sink, v_attn_w_o, v_gla_w_in, v_gla_gf_w1, v_gla_gf_w2,
                               v_gla_gf_b, v_gla_gb_w1, v_gla_gb_w2, v_gla_gb_b, v_gla_onorm_g, v_gla_w_o, v_final_norm_g)))
    L, D, W6 = ada_w.shape[0], ada_w.shape[1], ada_w.shape[2]
    M = ctx.shape[1]
    mx, my, mc = _me()
    chip = 2 * mx + my
    batch = 4 * mx + 2 * my + mc

    crow = jnp.concatenate([c.astype(F32), jnp.zeros((7, D), F32)], axis=0)
    call = _all_gather8(crow, name="gather_c")[:, 0, :]
    s16 = jnp.concatenate([jax.nn.silu(call), jax.nn.silu(c_ctx)[None], jnp.zeros((ROWS16 - 9, D), F32)], axis=0)
    s16c = s16.astype(CDT)
    ada_c = ada_w.astype(CDT)
    mod_cols = jnp.concatenate([_mm(s16c, ada_c[i], out_dtype=F32, name=f"mods_l{i}") for i in range(L)], axis=0)
    mod_all = _all_gather8(mod_cols, name="gather_mods")
    mod_all = mod_all.reshape(4, 2, L, ROWS16, W6)[:, 0]
    mod_all = jnp.moveaxis(mod_all, 0, 2).reshape(L, ROWS16, 4 * W6) + ada_b[:, None, :]
    mod_mine = jnp.stack([mod_all[:, 8], lax.dynamic_index_in_dim(mod_all, batch, axis=1, keepdims=False)], axis=1)
    mods = mod_mine.reshape(L, 2, N_MOD, D)

    pos = jnp.stack([chip, mc]).astype(jnp.int32)
    small_w = _gather_small({name: wts[name] for name, _ in SMALL_SHARDED})
    lw = []
    for i in range(L):
        keys = _layer_big(i)
        bufs = [_place(wts[name], j, pos, name=f"l{i}_place_{key}") for key, name, j in keys]
        bufs = _gather_layer(bufs, name=f"l{i}_gather_w")
        lw.append(_layer_weights(i, {key: buf for (key, _, _), buf in zip(keys, bufs)}, small_w, wts))

    red = {name: jnp.zeros(wts[name].shape, F32) for name in ("ffn_w_up", "ffn_w_down", "attn_w_qkv", "attn_w_o", "gla_w_in", "gla_w_o")}

    def reduce_layer(i, g):
        keys = _layer_big(i)
        gs = [g.pop(key) for key, _, _ in keys]
        gots = _rs_split(gs, name=f"l{i}_rs_split")
        parts = [_rs_add(gv, got, pos, name=f"l{i}_rs_add_{key}") for (key, _, _), gv, got in zip(keys, gs, gots)]
        recvs = _rs_scatter(parts, name=f"l{i}_rs_scatter")
        outs = [_rs_sum(part, recv, red[name], j, pos, name=f"l{i}_rs_sum_{key}") for (key, name, j), part, recv in zip(keys, parts, recvs)]
        outs = _rs_share(outs, [j for _, _, j in keys], name=f"l{i}_rs_share")
        for (_, name, _), out in zip(keys, outs):
            red[name] = out
        return g

    xcat = jnp.concatenate([ctx[0], x[0]], axis=0)
    loss, dx, dmods, grads, d_final_g = _local_step(xcat, loss_target[0], mods, lw, final_norm_g[None], M=M, on_grads=reduce_layer)
    loss = lax.psum(loss, ("x", "y", "c"))
    grad_x = dx[M:][None]

    dm_all = _all_gather8(dmods.reshape(L * 2, N_MOD * D), name="gather_dmods")
    dm_sum = _sum_lead(dm_all, name="dmods_sum").reshape(L, 2, N_MOD * D)
    dm_rows = dm_all.reshape(8, L, 2, N_MOD * D)[:, :, 1]
    dm16 = jnp.concatenate([jnp.moveaxis(dm_rows, 0, 1), dm_sum[:, 0:1], jnp.zeros((L, ROWS16 - 9, N_MOD * D), F32)], axis=1)
    dm16 = lax.dynamic_slice_in_dim(dm16, chip * W6, W6, axis=2).astype(CDT)
    g_ada_w = jnp.stack([_mm(s16c, dm16[i], ta=True, out_dtype=F32, name=f"dada_w_l{i}") for i in range(L)], axis=0)
    ds16 = _mm(dm16[0], ada_c[0], tb=True, out_dtype=F32, name="dcond_l0")
    for i in range(1, L):
        ds16 = ds16 + _mm(dm16[i], ada_c[i], tb=True, out_dtype=F32, name=f"dcond_l{i}")
    d_sctx = ds16[8] * jnp.where(mc == 0, 1.0, 0.0)

    gfull = _small_grads(grads, D)
    gfull["final_norm_g"] = d_final_g
    gfull["c_ctx"] = d_sctx

    small_names = list(REPLICATED) + [name for name, _ in SMALL_SHARDED]
    flat = jnp.concatenate([gfull[name].astype(F32).reshape(-1) for name in small_names])
    small = _sum_lead(_all_gather8(_rows_of(flat, SMALL_W), name="gather_small_g"), name="small_sum").reshape(-1)
    off = 0
    for name in small_names:
        shape = gfull[name].shape
        red[name] = small[off:off + _size(shape)].reshape(shape)
        off += _size(shape)
    for name, ax in SMALL_SHARDED:
        shape = wts[name].shape
        g4 = red[name].reshape(shape[:ax] + (4, shape[ax]) + shape[ax + 1:])
        red[name] = lax.dynamic_index_in_dim(g4, chip, axis=ax, keepdims=False)
    red["c_ctx"] = red["c_ctx"] * _dsilu(c_ctx)
    red["ada_w"] = g_ada_w
    red["ada_b"] = dm_sum[:, 0] + dm_sum[:, 1]

    deltas, new_m, new_v = {}, {}, {}
    for name in WEIGHTS:
        w = wts[name]
        view = (lambda a: a.reshape(-1, a.shape[-1])) if w.ndim > 1 else (lambda a: a.reshape(1, -1))
        d, m2, v2 = _adamw(view(w), view(red[name]), view(mom_m[name]), view(mom_v[name]), name=f"adamw_{name}")
        deltas[name], new_m[name], new_v[name] = d.reshape(w.shape), m2.reshape(w.shape), v2.reshape(w.shape)
    return (loss, grad_x, *[red[n] for n in WEIGHTS], *[deltas[n] for n in WEIGHTS], *[new_m[n] for n in WEIGHTS],
            *[new_v[n] for n in WEIGHTS])
```

```python
import functools

import jax
import jax.numpy as jnp
from jax import lax
from jax.experimental import pallas as pl
from jax.experimental.pallas import tpu as pltpu

F32 = jnp.float32
CDT = jnp.bfloat16
VMEM_LIMIT = 56 * 1024 * 1024
MESH = pl.DeviceIdType.MESH

ATT_HEADS = 16
ATT_KV = 4
ATT_G = ATT_HEADS // ATT_KV
BLK = 128
GRID_W = 64
ROPE_BASE = 10000.0
GLA_H = 4
GATE_RANK = 16
GATE_NORM = 16.0
CHUNK = 64
EPS = 1e-6
N_MOD = 6
LR, B1, B2, AEPS, WD, STEP = 0.001, 0.9, 0.999, 1e-08, 0.01, 10
PACK_ROWS = 512

NN = (((1,), (0,)), ((), ()))
NT = (((1,), (1,)), ((), ()))
TN = (((0,), (0,)), ((), ()))


def _dg(a, b, dims):
    return lax.dot_general(a, b, dims, preferred_element_type=F32)


def _pick(dim, cands):
    for c in cands:
        if dim % c == 0:
            return c
    return dim


def _cparams(sem):
    return pltpu.CompilerParams(dimension_semantics=sem, vmem_limit_bytes=VMEM_LIMIT)


def _silu(x):
    return x * (1.0 / (1.0 + jnp.exp(-x)))


def _dsilu(x):
    s = 1.0 / (1.0 + jnp.exp(-x))
    return s * (1.0 + x * (1.0 - s))


MM_VMEM_BUDGET = 40 * 1024 * 1024
TILE_M = (2048, 1408, 1088, 1024, 544, 512, 256, 128)
TILE_N = (2048, 1536, 1408, 1024, 768, 512, 256, 128)
TILE_K = (2176, 2048, 1408, 1088, 1024, 768, 512)


def _mm_tiles(m_unit, n_unit, k_unit, out_bytes):
    best = None
    for tm in [c for c in TILE_M if m_unit % c == 0] or [m_unit]:
        for tn in [c for c in TILE_N if n_unit % c == 0] or [n_unit]:
            for tk in [c for c in TILE_K if k_unit % c == 0] or [k_unit]:
                vmem = 4 * tk * (tm + tn) + tm * tn * (2 * out_bytes + 4)
                if vmem > MM_VMEM_BUDGET:
                    continue
                key = (tm * tn / (tm + tn), tk)
                if best is None or key > best[0]:
                    best = (key, (tm, tn, tk))
    assert best is not None, (m_unit, n_unit, k_unit)
    return best[1]


def _mm(a, b, *, ta=False, tb=False, out_dtype=F32, name, a_split=False, b_chip=None, out_chip=None, extra=None, ride=None):
    if a_split:
        assert not ta
        M, K = a.shape[1], a.shape[0] * a.shape[2]
    elif ta:
        K, M = a.shape
    else:
        M, K = a.shape
    bs = list(b.shape) if b_chip is None else list(b.shape[1:])
    if b_chip is not None:
        bs[b_chip] *= b.shape[0]
    N, K2 = bs if tb else bs[::-1]
    assert K == K2, (a.shape, b.shape, ta, tb, b_chip)
    m_unit, n_unit, k_unit = M, N, K
    if a_split:
        k_unit = a.shape[2]
    if b_chip is not None:
        if (b_chip == 0) == tb:
            n_unit = N // b.shape[0]
        else:
            k_unit = min(k_unit, K // b.shape[0])
    if out_chip == 0:
        m_unit = M // 4
    elif out_chip == 1:
        n_unit = min(n_unit, N // 4)
    tm, tn, tk = _mm_tiles(m_unit, n_unit, k_unit, jnp.dtype(out_dtype).itemsize)
    nk = K // tk
    dims = TN if ta else (NT if tb else NN)

    n_main = 2 if extra is None else 4
    n_rin = 0 if ride is None else len(ride["ins"])
    n_rout = 0 if ride is None else len(ride["outs"])
    grid = (N // tn, M // tm, nk)

    def body(*refs):
        a_ref, b_ref = refs[0], refs[1]
        o_ref = refs[n_main + n_rin]
        acc_ref = refs[n_main + n_rin + 1 + n_rout]
        k = pl.program_id(2)
        if ride is not None:
            r_in = refs[n_main:n_main + n_rin]
            r_out = refs[n_main + n_rin + 1:n_main + n_rin + 1 + n_rout]
            sems = (refs[-2], refs[-1])
            step = (pl.program_id(0) * grid[1] + pl.program_id(1)) * grid[2] + k

            @pl.when(step == 0)
            def _():
                ride["start"](r_in, r_out, sems)

        @pl.when(k == 0)
        def _():
            if extra is None:
                acc_ref[...] = jnp.zeros_like(acc_ref)
            else:
                acc_ref[...] = _dg(refs[2][...], refs[3][...], dims)

        acc_ref[...] += _dg(a_ref[...], b_ref[...], dims)

        @pl.when(k == nk - 1)
        def _():
            o_ref[...] = acc_ref[...].astype(o_ref.dtype)

        if ride is not None:
            @pl.when(step == grid[0] * grid[1] * grid[2] - 1)
            def _():
                ride["finish"](r_in, r_out, sems)

    def b_index(n, m, k):
        i0, i1 = (n, k) if tb else (k, n)
        if b_chip is None:
            return (i0, i1)
        if b_chip == 0:
            nb = b.shape[1] // b_block[0]
            return (i0 // nb, i0 % nb, i1)
        nb = b.shape[2] // b_block[1]
        return (i1 // nb, i0, i1 % nb)

    def o_index(n, m, k):
        if out_chip is None:
            return (m, n)
        if out_chip == 0:
            mb = m_unit // tm
            return (m // mb, m % mb, n)
        nb = n_unit // tn
        return (n // nb, m, n % nb)

    b_block = (tn, tk) if tb else (tk, tn)
    lead = lambda blk, on: ((None,) + blk) if on else blk
    if a_split:
        kb = a.shape[2] // tk
        a_spec = pl.BlockSpec((None, tm, tk), lambda n, m, k: (k // kb, m, k % kb))
    elif ta:
        a_spec = pl.BlockSpec((tk, tm), lambda n, m, k: (k, m))
    else:
        a_spec = pl.BlockSpec((tm, tk), lambda n, m, k: (m, k))
    in_specs = [a_spec, pl.BlockSpec(lead(b_block, b_chip is not None), b_index)]
    args = [a, b]
    if extra is not None:
        assert not ta
        a2, b2 = extra
        E = a2.shape[1]
        in_specs += [pl.BlockSpec((tm, E), lambda n, m, k: (m, 0)),
                     pl.BlockSpec((tn, E), lambda n, m, k: (n, 0)) if tb else pl.BlockSpec((E, tn), lambda n, m, k: (0, n))]
        args += [a2, b2]
    out_full = (M, N) if out_chip is None else ((4, M // 4, N) if out_chip == 0 else (4, M, N // 4))
    out_spec = pl.BlockSpec(lead((tm, tn), out_chip is not None), o_index)
    out_main = jax.ShapeDtypeStruct(out_full, out_dtype)
    acc = pltpu.VMEM((tm, tn), F32)
    if ride is None:
        return pl.pallas_call(
            body, name=name, grid=grid, in_specs=in_specs, out_specs=out_spec, out_shape=out_main, scratch_shapes=[acc],
            compiler_params=_cparams(("parallel", "parallel", "arbitrary")),
        )(*args)
    any_spec = pl.BlockSpec(memory_space=pl.ANY)
    res = pl.pallas_call(
        body, name=name, grid=grid, in_specs=in_specs + [any_spec] * n_rin,
        out_specs=[out_spec] + [any_spec] * n_rout, out_shape=[out_main] + list(ride["outs"]),
        input_output_aliases={n_main + i: 1 + o for i, o in ride["alias"].items()},
        scratch_shapes=[acc, pltpu.SemaphoreType.DMA((ride["nsem"],)), pltpu.SemaphoreType.DMA((ride["nsem"],))],
        compiler_params=_cparams(("arbitrary", "arbitrary", "arbitrary")),
    )(*args, *ride["ins"])
    ride["result"] = list(res[1:])
    return res[0]


def _seg_spec(D, first_lat):
    return pl.BlockSpec((1, 1, D), lambda i: (jnp.where(i >= first_lat, 1, 0), 0, 0))


def _norm_fwd(x, y, gate, g, shift, scale, *, M, name):
    T, D = x.shape
    tm = _pick(T, (256,))
    first_lat = M // tm
    has_res = y is not None
    seg = _seg_spec(D, first_lat)
    row = pl.BlockSpec((tm, D), lambda i: (i, 0))

    def body(*refs):
        if has_res:
            x_ref, y_ref, gate_ref, g_ref, sh_ref, sc_ref, xo_ref, h_ref = refs
            xv = x_ref[...] + gate_ref[0] * y_ref[...].astype(F32)
            xo_ref[...] = xv
        else:
            x_ref, g_ref, sh_ref, sc_ref, h_ref = refs
            xv = x_ref[...]
        rstd = lax.rsqrt(jnp.mean(xv * xv, axis=-1, keepdims=True) + EPS)
        h = xv * rstd * g_ref[...] * (1.0 + sc_ref[0]) + sh_ref[0]
        h_ref[...] = h.astype(h_ref.dtype)

    gspec = pl.BlockSpec((1, D), lambda i: (0, 0))
    if has_res:
        ins = [x, y, gate, g, shift, scale]
        in_specs = [row, row, seg, gspec, seg, seg]
        out_shape = (jax.ShapeDtypeStruct((T, D), F32), jax.ShapeDtypeStruct((T, D), CDT))
        out_specs = (row, row)
    else:
        ins = [x, g, shift, scale]
        in_specs = [row, gspec, seg, seg]
        out_shape = jax.ShapeDtypeStruct((T, D), CDT)
        out_specs = row
    out = pl.pallas_call(
        body, name=name, grid=(T // tm,), in_specs=in_specs, out_specs=out_specs, out_shape=out_shape,
        compiler_params=_cparams(("parallel",)),
    )(*ins)
    return out if has_res else (x, out)


def _norm_bwd(x, dh, dx_in, g, scale, y_prev, gate_prev, *, M, name):
    T, D = x.shape
    tm = _pick(T, (256,))
    first_lat = M // tm
    has_prev = y_prev is not None
    seg = _seg_spec(D, first_lat)
    row = pl.BlockSpec((tm, D), lambda i: (i, 0))
    gspec = pl.BlockSpec((1, D), lambda i: (0, 0))

    def body(*refs):
        if has_prev:
            x_ref, dh_ref, dxi_ref, g_ref, sc_ref, yp_ref, gp_ref, dx_ref, dy_ref, acc_ref = refs
        else:
            x_ref, dh_ref, dxi_ref, g_ref, sc_ref, dx_ref, acc_ref = refs
        i = pl.program_id(0)

        @pl.when(jnp.logical_or(i == 0, i == first_lat))
        def _():
            acc_ref[...] = jnp.zeros_like(acc_ref)

        xv = x_ref[...]
        rstd = lax.rsqrt(jnp.mean(xv * xv, axis=-1, keepdims=True) + EPS)
        xn = xv * rstd
        dh = dh_ref[...].astype(F32)
        dxn = dh * (g_ref[...] * (1.0 + sc_ref[0]))
        dx = dxi_ref[...] + rstd * (dxn - xn * jnp.mean(dxn * xn, axis=-1, keepdims=True))
        dx_ref[...] = dx
        acc_ref[0, 0:1, :] += jnp.sum(dh, axis=0, keepdims=True)
        acc_ref[0, 1:2, :] += jnp.sum(dh * xn, axis=0, keepdims=True)
        if has_prev:
            dy_ref[...] = (dx * gp_ref[0]).astype(dy_ref.dtype)
            acc_ref[0, 2:3, :] += jnp.sum(dx * yp_ref[...].astype(F32), axis=0, keepdims=True)

    acc_spec = pl.BlockSpec((1, 8, D), lambda i: (jnp.where(i >= first_lat, 1, 0), 0, 0))
    acc_shape = jax.ShapeDtypeStruct((2, 8, D), F32)
    if has_prev:
        ins = [x, dh, dx_in, g, scale, y_prev, gate_prev]
        in_specs = [row, row, row, gspec, seg, row, seg]
        out_shape = (jax.ShapeDtypeStruct((T, D), F32), jax.ShapeDtypeStruct((T, D), CDT), acc_shape)
        out_specs = (row, row, acc_spec)
    else:
        ins = [x, dh, dx_in, g, scale]
        in_specs = [row, row, row, gspec, seg]
        out_shape = (jax.ShapeDtypeStruct((T, D), F32), acc_shape)
        out_specs = (row, acc_spec)
    out = pl.pallas_call(
        body, name=name, grid=(T // tm,), in_specs=in_specs, out_specs=out_specs, out_shape=out_shape,
        compiler_params=_cparams(("arbitrary",)),
    )(*ins)
    if has_prev:
        return out
    return out[0], None, out[1]


def _final_loss(x, y_prev, gate_prev, tgt, g, *, M, name):
    T, D = x.shape
    tm = _pick(T, (256,))
    first_lat = M // tm
    nt = T // tm
    seg = _seg_spec(D, first_lat)
    row = pl.BlockSpec((tm, D), lambda i: (i, 0))
    gspec = pl.BlockSpec((1, D), lambda i: (0, 0))
    tspec = pl.BlockSpec((tm, D), lambda i: (jnp.maximum(i - first_lat, 0), 0))

    def body(x_ref, yp_ref, gp_ref, t_ref, g_ref, loss_ref, dx_ref, dy_ref, acc_ref):
        i = pl.program_id(0)

        @pl.when(jnp.logical_or(i == 0, i == first_lat))
        def _():
            acc_ref[...] = jnp.zeros_like(acc_ref)

        lat = jnp.where(i >= first_lat, 1.0, 0.0)
        yp = yp_ref[...].astype(F32)
        xv = x_ref[...] + gp_ref[0] * yp
        rstd = lax.rsqrt(jnp.mean(xv * xv, axis=-1, keepdims=True) + EPS)
        xn = xv * rstd
        diff = (xn * g_ref[...] - t_ref[...]) * lat
        part = 0.5 * jnp.sum(jnp.sum(diff * diff, axis=-1, keepdims=True), axis=0, keepdims=True) * (1.0 / D)
        loss_ref[0] = jnp.broadcast_to(part, (8, 128))
        dyv = diff * (1.0 / D)
        dxn = dyv * g_ref[...]
        dx = rstd * (dxn - xn * jnp.mean(dxn * xn, axis=-1, keepdims=True))
        dx_ref[...] = dx
        dy_ref[...] = (dx * gp_ref[0]).astype(dy_ref.dtype)
        acc_ref[0, 0:1, :] += jnp.sum(dyv * xn, axis=0, keepdims=True)
        acc_ref[0, 2:3, :] += jnp.sum(dx * yp, axis=0, keepdims=True)

    return pl.pallas_call(
        body, name=name, grid=(nt,),
        in_specs=[row, row, seg, tspec, gspec],
        out_specs=(pl.BlockSpec((1, 8, 128), lambda i: (i, 0, 0)), row, row,
                   pl.BlockSpec((1, 8, D), lambda i: (jnp.where(i >= first_lat, 1, 0), 0, 0))),
        out_shape=(jax.ShapeDtypeStruct((nt, 8, 128), F32), jax.ShapeDtypeStruct((T, D), F32),
                   jax.ShapeDtypeStruct((T, D), CDT), jax.ShapeDtypeStruct((2, 8, D), F32)),
        compiler_params=_cparams(("arbitrary",)),
    )(x, y_prev, gate_prev, tgt, g)


HALO = 16
CONV_TC = (1408, 512)


def _taps(uc, prev16, next16, keep_prev, keep_next):
    tm = uc.shape[0]
    u = uc.astype(F32)
    rows = lax.broadcasted_iota(jnp.int32, u.shape, 0)
    pr = prev16[HALO - 1:HALO, :].astype(F32) * keep_prev
    nx = next16[0:1, :].astype(F32) * keep_next
    um = jnp.where(rows == 0, pr, pltpu.roll(u, 1, 0))
    up = jnp.where(rows == tm - 1, nx, pltpu.roll(u, tm - 1, 0))
    return um, u, up


def _conv3(uc, prev16, next16, w, bias, keep_prev, keep_next):
    um, u, up = _taps(uc, prev16, next16, keep_prev, keep_next)
    out = w[0:1, :] * um + w[1:2, :] * u + w[2:3, :] * up
    return out if bias is None else out + bias


def _conv_specs(tm, tc, T, col):
    hb = tm // HALO
    last = T // HALO - 1
    return [
        pl.BlockSpec((tm, tc), lambda j, i: (i, col(j))),
        pl.BlockSpec((HALO, tc), lambda j, i: (jnp.maximum(i * hb - 1, 0), col(j))),
        pl.BlockSpec((HALO, tc), lambda j, i: (jnp.minimum((i + 1) * hb, last), col(j))),
    ]


def _seg_keep(i, first_lat, nt):
    keep_prev = jnp.where(jnp.logical_or(i == 0, i == first_lat), 0.0, 1.0)
    keep_next = jnp.where(jnp.logical_or(i == first_lat - 1, i == nt - 1), 0.0, 1.0)
    return keep_prev, keep_next


def _conv_gate_fwd(u, cw, cb, *, M, name):
    T, F2 = u.shape
    Fh = F2 // 2
    tm = _pick(T, (256,))
    tc = _pick(Fh, CONV_TC)
    nf = Fh // tc
    nt = T // tm
    first_lat = M // tm

    def body(ug, ugp, ugn, uv, uvp, uvn, wg, wv, bg, bv, o_ref):
        kp, kn = _seg_keep(pl.program_id(1), first_lat, nt)
        gc = _conv3(ug[...], ugp[...], ugn[...], wg[...], bg[...], kp, kn)
        vc = _conv3(uv[...], uvp[...], uvn[...], wv[...], bv[...], kp, kn)
        o_ref[...] = (_silu(gc) * vc).astype(o_ref.dtype)

    wspec = lambda off: pl.BlockSpec((3, tc), lambda j, i: (0, j + off))
    bspec = lambda off: pl.BlockSpec((1, tc), lambda j, i: (0, j + off))
    return pl.pallas_call(
        body, name=name, grid=(nf, nt),
        in_specs=_conv_specs(tm, tc, T, lambda j: j) + _conv_specs(tm, tc, T, lambda j: j + nf)
        + [wspec(0), wspec(nf), bspec(0), bspec(nf)],
        out_specs=pl.BlockSpec((tm, tc), lambda j, i: (i, j)),
        out_shape=jax.ShapeDtypeStruct((T, Fh), CDT),
        compiler_params=_cparams(("parallel", "parallel")),
    )(u, u, u, u, u, u, cw, cw, cb, cb)


def _conv_gate_bwd(u, dact, cw, cb, *, M, name):
    T, F2 = u.shape
    Fh = F2 // 2
    tm = _pick(T, (256,))
    tc = _pick(Fh, (512,))
    nf = Fh // tc
    nt = T // tm
    first_lat = M // tm

    def body(ug, ugp, ugn, uv, uvp, uvn, da, wg, wv, bg, bv, d_ref, acc_ref):
        i = pl.program_id(1)

        @pl.when(i == 0)
        def _():
            acc_ref[...] = jnp.zeros_like(acc_ref)

        kp, kn = _seg_keep(i, first_lat, nt)
        tg = _taps(ug[...], ugp[...], ugn[...], kp, kn)
        tv = _taps(uv[...], uvp[...], uvn[...], kp, kn)
        w = wg[...]
        gc = w[0:1, :] * tg[0] + w[1:2, :] * tg[1] + w[2:3, :] * tg[2] + bg[...]
        w = wv[...]
        vc = w[0:1, :] * tv[0] + w[1:2, :] * tv[1] + w[2:3, :] * tv[2] + bv[...]
        dav = da[...].astype(F32)
        for half, d, taps in ((0, dav * vc * _dsilu(gc), tg), (1, dav * _silu(gc), tv)):
            d_ref[half] = d.astype(d_ref.dtype)
            acc_ref[half, 0:1, :] += jnp.sum(d * taps[0], axis=0, keepdims=True)
            acc_ref[half, 1:2, :] += jnp.sum(d * taps[1], axis=0, keepdims=True)
            acc_ref[half, 2:3, :] += jnp.sum(d * taps[2], axis=0, keepdims=True)
            acc_ref[half, 3:4, :] += jnp.sum(d, axis=0, keepdims=True)

    wspec = lambda off: pl.BlockSpec((3, tc), lambda j, i: (0, j + off))
    bspec = lambda off: pl.BlockSpec((1, tc), lambda j, i: (0, j + off))
    return pl.pallas_call(
        body, name=name, grid=(nf, nt),
        in_specs=_conv_specs(tm, tc, T, lambda j: j) + _conv_specs(tm, tc, T, lambda j: j + nf)
        + [pl.BlockSpec((tm, tc), lambda j, i: (i, j)), wspec(0), wspec(nf), bspec(0), bspec(nf)],
        out_specs=(pl.BlockSpec((2, tm, tc), lambda j, i: (0, i, j)), pl.BlockSpec((2, 8, tc), lambda j, i: (0, 0, j))),
        out_shape=(jax.ShapeDtypeStruct((2, T, Fh), CDT), jax.ShapeDtypeStruct((2, 8, Fh), F32)),
        compiler_params=_cparams(("parallel", "arbitrary")),
    )(u, u, u, u, u, u, dact, cw, cw, cb, cb)


def _conv_t(d, cw, *, M, name):
    _, T, Fh = d.shape
    tm = _pick(T, (256,))
    tc = _pick(Fh, CONV_TC)
    nf = Fh // tc
    nt = T // tm
    first_lat = M // tm
    hb = tm // HALO
    last = T // HALO - 1

    def body(dc, dp, dn, w, o_ref):
        kp, kn = _seg_keep(pl.program_id(2), first_lat, nt)
        dm, d0, dp1 = _taps(dc[...], dp[...], dn[...], kp, kn)
        wv = w[...]
        o_ref[...] = (wv[2:3, :] * dm + wv[1:2, :] * d0 + wv[0:1, :] * dp1).astype(o_ref.dtype)

    return pl.pallas_call(
        body, name=name, grid=(2, nf, nt),
        in_specs=[pl.BlockSpec((None, tm, tc), lambda g, j, i: (g, i, j)),
                  pl.BlockSpec((None, HALO, tc), lambda g, j, i: (g, jnp.maximum(i * hb - 1, 0), j)),
                  pl.BlockSpec((None, HALO, tc), lambda g, j, i: (g, jnp.minimum((i + 1) * hb, last), j)),
                  pl.BlockSpec((3, tc), lambda g, j, i: (0, g * nf + j))],
        out_specs=pl.BlockSpec((None, tm, tc), lambda g, j, i: (g, i, j)),
        out_shape=jax.ShapeDtypeStruct((2, T, Fh), CDT),
        compiler_params=_cparams(("parallel", "parallel", "parallel")),
    )(d, d, d, cw)


def _rope_tables(N, M, HD):
    ax = HD // 2
    pos = jnp.arange(N, dtype=jnp.int32)
    row = (pos // GRID_W).astype(F32)
    col = (pos % GRID_W).astype(F32)
    inv = ROPE_BASE ** (-jnp.arange(0, ax, 2, dtype=F32) / ax)
    ar = row[:, None] * inv[None, :]
    ac = col[:, None] * inv[None, :]
    cos = jnp.concatenate([jnp.cos(ar), jnp.cos(ar), jnp.cos(ac), jnp.cos(ac)], axis=1)
    sin = jnp.concatenate([-jnp.sin(ar), jnp.sin(ar), -jnp.sin(ac), jnp.sin(ac)], axis=1)
    cos = jnp.concatenate([jnp.ones((M, HD), F32), cos], axis=0)
    sin = jnp.concatenate([jnp.zeros((M, HD), F32), sin], axis=0)
    return cos, sin


def _pair_swap(x, nf):
    w = x.shape[1]
    lane = lax.broadcasted_iota(jnp.int32, x.shape, 1)
    first = (lane % (2 * nf)) < nf
    return jnp.where(first, pltpu.roll(x, w - nf, 1), pltpu.roll(x, nf, 1))


def _rope_fwd(qkv, cos, sin, *, QW, KW, HD, name):
    T = qkv.shape[0]
    tm = _pick(T, (256,))
    nf = HD // 4

    def body(qkv_ref, c_ref, s_ref, q_ref, k_ref, v_ref):
        c = c_ref[...]
        s = s_ref[...]
        for ref, off, w in ((q_ref, 0, QW), (k_ref, QW, KW)):
            xv = qkv_ref[:, off:off + w]
            ct = jnp.tile(c, (1, w // HD))
            st = jnp.tile(s, (1, w // HD))
            ref[...] = (xv * ct + _pair_swap(xv, nf) * st).astype(ref.dtype)
        v_ref[...] = qkv_ref[:, QW + KW:QW + 2 * KW].astype(v_ref.dtype)

    tspec = pl.BlockSpec((tm, HD), lambda i: (i, 0))
    return pl.pallas_call(
        body, name=name, grid=(T // tm,),
        in_specs=[pl.BlockSpec((tm, QW + 2 * KW), lambda i: (i, 0)), tspec, tspec],
        out_specs=(pl.BlockSpec((tm, QW), lambda i: (i, 0)), pl.BlockSpec((tm, KW), lambda i: (i, 0)),
                   pl.BlockSpec((tm, KW), lambda i: (i, 0))),
        out_shape=(jax.ShapeDtypeStruct((T, QW), CDT), jax.ShapeDtypeStruct((T, KW), CDT),
                   jax.ShapeDtypeStruct((T, KW), CDT)),
        compiler_params=_cparams(("parallel",)),
    )(qkv, cos, sin)


def _rope_bwd(dq, dk, dv, cos, sin, *, HD, name):
    T, QW = dq.shape
    KW = dk.shape[1]
    tm = _pick(T, (256,))
    nf = HD // 4

    def body(dq_ref, dk_ref, dv_ref, c_ref, s_ref, o_ref):
        c = c_ref[...]
        s = s_ref[...]
        for ref, off, w in ((dq_ref, 0, QW), (dk_ref, QW, KW)):
            g = ref[...].astype(F32)
            ct = jnp.tile(c, (1, w // HD))
            st = jnp.tile(s, (1, w // HD))
            o_ref[:, off:off + w] = (g * ct + _pair_swap(g * st, nf)).astype(o_ref.dtype)
        o_ref[:, QW + KW:QW + 2 * KW] = dv_ref[...].astype(o_ref.dtype)

    tspec = pl.BlockSpec((tm, HD), lambda i: (i, 0))
    return pl.pallas_call(
        body, name=name, grid=(T // tm,),
        in_specs=[pl.BlockSpec((tm, QW), lambda i: (i, 0)), pl.BlockSpec((tm, KW), lambda i: (i, 0)),
                  pl.BlockSpec((tm, KW), lambda i: (i, 0)), tspec, tspec],
        out_specs=pl.BlockSpec((tm, QW + 2 * KW), lambda i: (i, 0)),
        out_shape=jax.ShapeDtypeStruct((T, QW + 2 * KW), CDT),
        compiler_params=_cparams(("parallel",)),
    )(dq, dk, dv, cos, sin)


def _attn_scores(q_ref, kc_ref, kp_ref, kn_ref, kx_ref, sink_ref, i, *, M, HD, nblk, nbc):
    qs = jnp.concatenate([q_ref[:, g * HD:(g + 1) * HD] for g in range(ATT_G)], axis=0)
    kall = jnp.concatenate([kc_ref[...], kp_ref[...], kn_ref[...], kx_ref[...]], axis=0)
    s = _dg(qs, kall, NT) * (HD ** -0.5)
    shape = s.shape
    r = lax.broadcasted_iota(jnp.int32, shape, 0) % BLK
    c = lax.broadcasted_iota(jnp.int32, shape, 1) - M
    far = 4 * BLK
    lat_off = jnp.where(i >= nbc, 0, far)
    lo = jnp.maximum(r, jnp.where(i - 1 >= nbc, 0, BLK)) + lat_off
    hi = jnp.minimum(r + 2 * BLK, jnp.where(i + 1 < nblk, 3 * BLK - 1, 2 * BLK - 1))
    allowed = jnp.logical_or(c < 0, jnp.logical_and(c >= lo, c <= hi))
    s = jnp.where(allowed, s, -1e30)
    sink = sink_ref[0]
    m = jnp.maximum(jnp.max(s, axis=-1, keepdims=True), sink)
    e = jnp.exp(s - m)
    es = jnp.exp(sink - m)
    inv = 1.0 / (jnp.sum(e, axis=-1, keepdims=True) + es)
    return qs, kall, e * inv, es * inv


def _attn_specs(M, HD, nblk):
    kv_blk = lambda f: pl.BlockSpec((BLK, HD), lambda h, i: (f(i), h))
    ctx = pl.BlockSpec((M, HD), lambda h, i: (0, h))
    win = [kv_blk(lambda i: jnp.maximum(i - 1, 0)), kv_blk(lambda i: i), kv_blk(lambda i: jnp.minimum(i + 1, nblk - 1))]
    qspec = pl.BlockSpec((BLK, ATT_G * HD), lambda h, i: (i, h))
    sspec = pl.BlockSpec((1, ATT_G * BLK, 1), lambda h, i: (h, 0, 0))
    return qspec, [ctx] + win, sspec


def _attn_fwd(q, k, v, sink_col, *, M, name):
    T, QW = q.shape
    HD = QW // ATT_HEADS
    nblk = T // BLK
    nbc = M // BLK

    def body(q_ref, kc, kp, kn, kx, vc, vp, vn, vx, sink_ref, o_ref):
        i = pl.program_id(1)
        _, _, p, _ = _attn_scores(q_ref, kc, kp, kn, kx, sink_ref, i, M=M, HD=HD, nblk=nblk, nbc=nbc)
        vall = jnp.concatenate([vc[...], vp[...], vn[...], vx[...]], axis=0)
        o = _dg(p.astype(CDT), vall, NN)
        for g in range(ATT_G):
            o_ref[:, g * HD:(g + 1) * HD] = o[g * BLK:(g + 1) * BLK, :].astype(o_ref.dtype)

    qspec, kvs, sspec = _attn_specs(M, HD, nblk)
    return pl.pallas_call(
        body, name=name, grid=(ATT_KV, nblk),
        in_specs=[qspec] + kvs + kvs + [sspec],
        out_specs=qspec,
        out_shape=jax.ShapeDtypeStruct((T, QW), CDT),
        compiler_params=_cparams(("parallel", "parallel")),
    )(q, k, k, k, k, v, v, v, v, sink_col)


def _attn_bwd(q, k, v, sink_col, do, *, M, name):
    T, QW = q.shape
    HD = QW // ATT_HEADS
    KW = ATT_KV * HD
    nblk = T // BLK
    nbc = M // BLK

    def body(q_ref, kc, kp, kn, kx, vc, vp, vn, vx, sink_ref, do_ref, dq_ref, dkc_ref, dvc_ref, dkw_ref, dvw_ref, ds_ref):
        i = pl.program_id(1)

        @pl.when(i == 0)
        def _():
            dkc_ref[...] = jnp.zeros_like(dkc_ref)
            dvc_ref[...] = jnp.zeros_like(dvc_ref)
            ds_ref[...] = jnp.zeros_like(ds_ref)

        qs, kall, p, p_sink = _attn_scores(q_ref, kc, kp, kn, kx, sink_ref, i, M=M, HD=HD, nblk=nblk, nbc=nbc)
        vall = jnp.concatenate([vc[...], vp[...], vn[...], vx[...]], axis=0)
        dos = jnp.concatenate([do_ref[:, g * HD:(g + 1) * HD] for g in range(ATT_G)], axis=0)
        dp = _dg(dos, vall, NT)
        dsum = jnp.sum(p * dp, axis=-1, keepdims=True)
        dsc = (p * (dp - dsum) * (HD ** -0.5)).astype(CDT)
        dq = _dg(dsc, kall, NN)
        dkall = _dg(dsc, qs, TN)
        dvall = _dg(p.astype(CDT), dos, TN)
        for g in range(ATT_G):
            dq_ref[:, g * HD:(g + 1) * HD] = dq[g * BLK:(g + 1) * BLK, :].astype(dq_ref.dtype)
        dkc_ref[...] += dkall[0:M]
        dvc_ref[...] += dvall[0:M]
        dkw_ref[...] = dkall[M:]
        dvw_ref[...] = dvall[M:]
        ds_ref[0] += -(p_sink * dsum)

    qspec, kvs, sspec = _attn_specs(M, HD, nblk)
    ctx_out = pl.BlockSpec((M, HD), lambda h, i: (0, h))
    win_out = pl.BlockSpec((3 * BLK, HD), lambda h, i: (i, h))
    return pl.pallas_call(
        body, name=name, grid=(ATT_KV, nblk),
        in_specs=[qspec] + kvs + kvs + [sspec, qspec],
        out_specs=(qspec, ctx_out, ctx_out, win_out, win_out, sspec),
        out_shape=(jax.ShapeDtypeStruct((T, QW), CDT), jax.ShapeDtypeStruct((M, KW), F32), jax.ShapeDtypeStruct((M, KW), F32),
                   jax.ShapeDtypeStruct((nblk * 3 * BLK, KW), F32), jax.ShapeDtypeStruct((nblk * 3 * BLK, KW), F32),
                   jax.ShapeDtypeStruct((ATT_KV, ATT_G * BLK, 1), F32)),
        compiler_params=_cparams(("parallel", "arbitrary")),
    )(q, k, k, k, k, v, v, v, v, sink_col, do)


def _window_combine(part, *, nbc, name):
    rows, KW = part.shape
    nblk = rows // (3 * BLK)
    nbl = nblk - nbc

    def body(a_ref, b_ref, c_ref, o_ref):
        j = pl.program_id(0)
        o_ref[...] = (a_ref[...] * jnp.where(j + 1 < nbl, 1.0, 0.0) + b_ref[...]
                      + c_ref[...] * jnp.where(j >= 1, 1.0, 0.0))

    return pl.pallas_call(
        body, name=name, grid=(nbl,),
        in_specs=[pl.BlockSpec((BLK, KW), lambda j: (3 * jnp.minimum(nbc + j + 1, nblk - 1), 0)),
                  pl.BlockSpec((BLK, KW), lambda j: (3 * (nbc + j) + 1, 0)),
                  pl.BlockSpec((BLK, KW), lambda j: (3 * jnp.maximum(nbc + j - 1, 0) + 2, 0))],
        out_specs=pl.BlockSpec((BLK, KW), lambda j: (j, 0)),
        out_shape=jax.ShapeDtypeStruct((nbl * BLK, KW), F32),
        compiler_params=_cparams(("parallel",)),
    )(part, part, part)


def _split3(x):
    hi = x.astype(CDT)
    r1 = x - hi.astype(F32)
    mid = r1.astype(CDT)
    lo = (r1 - mid.astype(F32)).astype(CDT)
    return hi, mid, lo


def _tri_sum(tri, x, terms):
    parts = _split3(x)[:terms]
    out = _dg(tri, parts[0], NN)
    for p in parts[1:]:
        out = out + _dg(tri, p, NN)
    return out


def _gla_dims(D):
    dk = D // 2 // GLA_H
    dv = D // GLA_H
    return dk, dv


def _chunk_of(s, rev, ncc, ns):
    if not rev:
        return s
    return jnp.where(s < ncc, ncc - 1 - s, ns - 1 - (s - ncc))


def _gla_chunk(q_ref, k_ref, g_ref, rev, dk):
    C = q_ref.shape[0]
    r = lax.broadcasted_iota(jnp.int32, (C, C), 0)
    c = lax.broadcasted_iota(jnp.int32, (C, C), 1)
    causal = (r <= c) if rev else (r >= c)
    b = _tri_sum(causal.astype(CDT), g_ref[...], 3)
    B = b[0:1, :] if rev else b[C - 1:C, :]
    q = q_ref[...].astype(F32) * (dk ** -0.5)
    k = k_ref[...].astype(F32)
    return causal, b, B, q * jnp.exp(b), k * jnp.exp(-b), k * jnp.exp(B - b)


def _gla_scan_fwd(proj, g, *, rev, M, D, name):
    T = proj.shape[0]
    dk, dv = _gla_dims(D)
    C = CHUNK
    ns = T // C
    ncc = M // C
    koff = (D // 2) // dk
    voff = D // dv
    cm = lambda s: _chunk_of(s, rev, ncc, ns)

    def body(q_ref, k_ref, v_ref, g_ref, o_ref, st_ref, S):
        s = pl.program_id(1)

        @pl.when(s == 0)
        def _():
            S[...] = jnp.zeros_like(S)

        causal, b, B, qt, kt, kh = _gla_chunk(q_ref, k_ref, g_ref, rev, dk)
        v = v_ref[...]
        A = jnp.where(causal, _dg(qt.astype(CDT), kt.astype(CDT), NT), 0.0)
        Sin = S[...]
        st_ref[0] = Sin
        o_ref[...] = _dg(A.astype(CDT), v, NN) + _dg(qt.astype(CDT), Sin.astype(CDT), NT)
        S[...] = Sin * jnp.exp(B) + _dg(v, kh.astype(CDT), TN)

    return pl.pallas_call(
        body, name=name, grid=(GLA_H, ns),
        in_specs=[pl.BlockSpec((C, dk), lambda h, s: (cm(s), h)),
                  pl.BlockSpec((C, dk), lambda h, s: (cm(s), koff + h)),
                  pl.BlockSpec((C, dv), lambda h, s: (cm(s), voff + h)),
                  pl.BlockSpec((C, dk), lambda h, s: (cm(s), h))],
        out_specs=(pl.BlockSpec((C, dv), lambda h, s: (cm(s), h)),
                   pl.BlockSpec((1, dv, dk), lambda h, s: (h * ns + s, 0, 0))),
        out_shape=(jax.ShapeDtypeStruct((T, D), F32), jax.ShapeDtypeStruct((GLA_H * ns, dv, dk), F32)),
        scratch_shapes=[pltpu.VMEM((dv, dk), F32)],
        compiler_params=_cparams(("parallel", "arbitrary")),
    )(proj, proj, proj, g)


def _gla_scan_bwd(proj, g, st, do, *, rev, M, D, name):
    T = proj.shape[0]
    dk, dv = _gla_dims(D)
    C = CHUNK
    ns = T // C
    ncc = M // C
    koff = (D // 2) // dk
    voff = D // dv
    cm = lambda j: _chunk_of(ns - 1 - j, rev, ncc, ns)

    def body(q_ref, k_ref, v_ref, g_ref, st_ref, do_ref, dq_ref, dk_ref, dv_ref, dg_ref, dS):
        j = pl.program_id(1)

        @pl.when(j == 0)
        def _():
            dS[...] = jnp.zeros_like(dS)

        causal, b, B, qt, kt, kh = _gla_chunk(q_ref, k_ref, g_ref, rev, dk)
        v = v_ref[...]
        dov = do_ref[...]
        ST = st_ref[0]
        dSo = dS[...]
        qtb, ktb, khb = qt.astype(CDT), kt.astype(CDT), kh.astype(CDT)
        dSb = dSo.astype(CDT)
        A = jnp.where(causal, _dg(qtb, ktb, NT), 0.0).astype(CDT)
        dA = jnp.where(causal, _dg(dov, v, NT), 0.0).astype(CDT)
        dqt = _dg(dA, ktb, NN) + _dg(dov, ST.astype(CDT), NN)
        dkt = _dg(dA, qtb, TN)
        dvv = _dg(A, dov, TN) + _dg(khb, dSb, NT)
        dkh = _dg(v, dSb, NN)
        eB = jnp.exp(B)
        dB = eB * jnp.sum(ST * dSo, axis=0, keepdims=True) + jnp.sum(dkh * kh, axis=0, keepdims=True)
        rows = lax.broadcasted_iota(jnp.int32, (C, dk), 0)
        db = dqt * qt - dkt * kt - dkh * kh + jnp.where(rows == (0 if rev else C - 1), dB, 0.0)
        anti = jnp.logical_not(causal) | (lax.broadcasted_iota(jnp.int32, (C, C), 0) == lax.broadcasted_iota(jnp.int32, (C, C), 1))
        dg_ref[...] = _tri_sum(anti.astype(CDT), db, 2)
        dq_ref[...] = dqt * jnp.exp(b) * (dk ** -0.5)
        dk_ref[...] = dkt * jnp.exp(-b) + dkh * jnp.exp(B - b)
        dv_ref[...] = dvv
        dS[...] = dSo * eB + _dg(dov, qtb, TN)

    return pl.pallas_call(
        body, name=name, grid=(GLA_H, ns),
        in_specs=[pl.BlockSpec((C, dk), lambda h, j: (cm(j), h)),
                  pl.BlockSpec((C, dk), lambda h, j: (cm(j), koff + h)),
                  pl.BlockSpec((C, dv), lambda h, j: (cm(j), voff + h)),
                  pl.BlockSpec((C, dk), lambda h, j: (cm(j), h)),
                  pl.BlockSpec((1, dv, dk), lambda h, j: (h * ns + ns - 1 - j, 0, 0)),
                  pl.BlockSpec((C, dv), lambda h, j: (cm(j), h))],
        out_specs=(pl.BlockSpec((C, dk), lambda h, j: (cm(j), h)), pl.BlockSpec((C, dk), lambda h, j: (cm(j), h)),
                   pl.BlockSpec((C, dv), lambda h, j: (cm(j), h)), pl.BlockSpec((C, dk), lambda h, j: (cm(j), h))),
        out_shape=(jax.ShapeDtypeStruct((T, D // 2), F32), jax.ShapeDtypeStruct((T, D // 2), F32),
                   jax.ShapeDtypeStruct((T, D), F32), jax.ShapeDtypeStruct((T, D // 2), F32)),
        scratch_shapes=[pltpu.VMEM((dv, dk), F32)],
        compiler_params=_cparams(("parallel", "arbitrary")),
    )(proj, proj, proj, g, st, do)


def _log_sigmoid_parts(z):
    t = jnp.exp(-jnp.abs(z))
    return jnp.minimum(z, 0.0) - jnp.log(1.0 + t), jnp.where(z >= 0, t / (1.0 + t), 1.0 / (1.0 + t))


def _gla_gate_fwd(lr, w2, bias, *, D, name):
    T = lr.shape[0]
    tm = _pick(T, (256,))
    Dh = D // 2

    def body(lr_ref, w_ref, b_ref, gf_ref, gb_ref):
        z = _dg(lr_ref[...], w_ref[...], NN) + b_ref[...]
        g, _ = _log_sigmoid_parts(z)
        g = g * (1.0 / GATE_NORM)
        gf_ref[...] = g[:, 0:Dh]
        gb_ref[...] = g[:, Dh:D]

    half = pl.BlockSpec((tm, Dh), lambda i: (i, 0))
    return pl.pallas_call(
        body, name=name, grid=(T // tm,),
        in_specs=[pl.BlockSpec((tm, 128), lambda i: (i, 0)), pl.BlockSpec((128, D), lambda i: (0, 0)),
                  pl.BlockSpec((1, D), lambda i: (0, 0))],
        out_specs=(half, half),
        out_shape=(jax.ShapeDtypeStruct((T, Dh), F32), jax.ShapeDtypeStruct((T, Dh), F32)),
        compiler_params=_cparams(("parallel",)),
    )(lr, w2, bias)


def _gla_proj_bwd(lr, w2, bias, dqf, dkf, dvf, dgf, dqb, dkb, dvb, dgb, dr, *, D, name):
    T = lr.shape[0]
    tm = _pick(T, (256,))
    Dh = D // 2

    def body(lr_ref, w_ref, b_ref, dqf_r, dkf_r, dvf_r, dgf_r, dqb_r, dkb_r, dvb_r, dgb_r, dr_ref, dp_ref, dl_ref, dw_ref, db_ref):
        @pl.when(pl.program_id(0) == 0)
        def _():
            dw_ref[...] = jnp.zeros_like(dw_ref)
            db_ref[...] = jnp.zeros_like(db_ref)

        lr = lr_ref[...]
        z = _dg(lr, w_ref[...], NN) + b_ref[...]
        _, sneg = _log_sigmoid_parts(z)
        dz = jnp.concatenate([dgf_r[...], dgb_r[...]], axis=1) * sneg * (1.0 / GATE_NORM)
        dzb = dz.astype(CDT)
        dp_ref[:, 0:Dh] = (dqf_r[...] + dqb_r[...]).astype(dp_ref.dtype)
        dp_ref[:, Dh:D] = (dkf_r[...] + dkb_r[...]).astype(dp_ref.dtype)
        dp_ref[:, D:2 * D] = (dvf_r[...] + dvb_r[...]).astype(dp_ref.dtype)
        dp_ref[:, 2 * D:3 * D] = dr_ref[...]
        dl_ref[...] = _dg(dzb, w_ref[...], NT).astype(dl_ref.dtype)
        dw_ref[...] += _dg(lr, dzb, TN)
        db_ref[0:1, :] += jnp.sum(dz, axis=0, keepdims=True)

    half = pl.BlockSpec((tm, Dh), lambda i: (i, 0))
    full = pl.BlockSpec((tm, D), lambda i: (i, 0))
    return pl.pallas_call(
        body, name=name, grid=(T // tm,),
        in_specs=[pl.BlockSpec((tm, 128), lambda i: (i, 0)), pl.BlockSpec((128, D), lambda i: (0, 0)),
                  pl.BlockSpec((1, D), lambda i: (0, 0)), half, half, full, half, half, half, full, half, full],
        out_specs=(pl.BlockSpec((tm, 3 * D), lambda i: (i, 0)), pl.BlockSpec((tm, 128), lambda i: (i, 0)),
                   pl.BlockSpec((128, D), lambda i: (0, 0)), pl.BlockSpec((8, D), lambda i: (0, 0))),
        out_shape=(jax.ShapeDtypeStruct((T, 3 * D), CDT), jax.ShapeDtypeStruct((T, 128), CDT),
                   jax.ShapeDtypeStruct((128, D), F32), jax.ShapeDtypeStruct((8, D), F32)),
        compiler_params=_cparams(("arbitrary",)),
    )(lr, w2, bias, dqf, dkf, dvf, dgf, dqb, dkb, dvb, dgb, dr)


def _gla_out_fwd(of, ob, proj, gn, *, D, name):
    T = of.shape[0]
    tm = _pick(T, (256,))
    dv = D // GLA_H

    def body(of_ref, ob_ref, r_ref, g_ref, y_ref):
        for h in range(GLA_H):
            sl = slice(h * dv, (h + 1) * dv)
            o = of_ref[:, sl] + ob_ref[:, sl]
            rstd = lax.rsqrt(jnp.mean(o * o, axis=-1, keepdims=True) + EPS)
            y_ref[:, sl] = (o * rstd * g_ref[...] * _silu(r_ref[:, sl].astype(F32))).astype(y_ref.dtype)

    full = pl.BlockSpec((tm, D), lambda i: (i, 0))
    return pl.pallas_call(
        body, name=name, grid=(T // tm,),
        in_specs=[full, full, pl.BlockSpec((tm, D), lambda i: (i, 2)), pl.BlockSpec((1, dv), lambda i: (0, 0))],
        out_specs=full, out_shape=jax.ShapeDtypeStruct((T, D), CDT),
        compiler_params=_cparams(("parallel",)),
    )(of, ob, proj, gn)


def _gla_out_bwd(of, ob, proj, gn, dy, *, D, name):
    T = of.shape[0]
    tm = _pick(T, (256,))
    dv = D // GLA_H

    def body(of_ref, ob_ref, r_ref, g_ref, dy_ref, do_ref, dr_ref, dg_ref):
        @pl.when(pl.program_id(0) == 0)
        def _():
            dg_ref[...] = jnp.zeros_like(dg_ref)

        gv = g_ref[...]
        for h in range(GLA_H):
            sl = slice(h * dv, (h + 1) * dv)
            o = of_ref[:, sl] + ob_ref[:, sl]
            rstd = lax.rsqrt(jnp.mean(o * o, axis=-1, keepdims=True) + EPS)
            oh = o * rstd
            r = r_ref[:, sl].astype(F32)
            dyv = dy_ref[:, sl].astype(F32)
            don = dyv * _silu(r)
            dr_ref[:, sl] = (dyv * oh * gv * _dsilu(r)).astype(dr_ref.dtype)
            dg_ref[0:1, :] += jnp.sum(don * oh, axis=0, keepdims=True)
            dn = don * gv
            do_ref[:, sl] = (rstd * (dn - oh * jnp.mean(dn * oh, axis=-1, keepdims=True))).astype(do_ref.dtype)

    full = pl.BlockSpec((tm, D), lambda i: (i, 0))
    return pl.pallas_call(
        body, name=name, grid=(T // tm,),
        in_specs=[full, full, pl.BlockSpec((tm, D), lambda i: (i, 2)), pl.BlockSpec((1, dv), lambda i: (0, 0)), full],
        out_specs=(full, full, pl.BlockSpec((8, dv), lambda i: (0, 0))),
        out_shape=(jax.ShapeDtypeStruct((T, D), CDT), jax.ShapeDtypeStruct((T, D), CDT), jax.ShapeDtypeStruct((8, dv), F32)),
        compiler_params=_cparams(("arbitrary",)),
    )(of, ob, proj, gn, dy)


def _adamw(w, g, m, v, *, name):
    R, Cc = w.shape
    tr = R
    for cand in (512, 256, 128, 64, 32, 16, 8):
        if R % cand == 0 and cand * Cc * 4 <= 2 * 1024 * 1024:
            tr = cand
            break

    def body(w_ref, g_ref, m_ref, v_ref, d_ref, mo_ref, vo_ref):
        gv = g_ref[...]
        mn = B1 * m_ref[...] + (1.0 - B1) * gv
        vn = B2 * v_ref[...] + (1.0 - B2) * (gv * gv)
        mh = mn / (1.0 - B1 ** STEP)
        vh = vn / (1.0 - B2 ** STEP)
        d_ref[...] = -LR * (mh / (jnp.sqrt(vh) + AEPS) + WD * w_ref[...])
        mo_ref[...] = mn
        vo_ref[...] = vn

    spec = pl.BlockSpec((tr, Cc), lambda i: (i, 0))
    sh = jax.ShapeDtypeStruct((R, Cc), F32)
    return pl.pallas_call(
        body, name=name, grid=(R // tr,), in_specs=[spec] * 4, out_specs=(spec,) * 3, out_shape=(sh,) * 3,
        compiler_params=_cparams(("parallel",)),
    )(w, g, m, v)


def _attn_layer_fwd(h, w, tabs, M, tag, ride=None):
    cos, sin = tabs
    QW = 4 * w["w_o"].shape[1]
    HD = QW // ATT_HEADS
    KW = ATT_KV * HD
    qkv = _mm(h, w["w_qkv"], b_chip=1, out_dtype=F32, ride=ride, name=f"{tag}_qkv")
    q, k, v = _rope_fwd(qkv, cos, sin, QW=QW, KW=KW, HD=HD, name=f"{tag}_rope")
    sink_col = jnp.repeat(w["sink"].astype(F32), BLK).reshape(ATT_KV, ATT_G * BLK, 1)
    o = _attn_fwd(q, k, v, sink_col, M=M, name=f"{tag}_attn")
    y = _mm(o, w["w_o"], b_chip=0, out_dtype=CDT, name=f"{tag}_wo")
    return y, dict(h=h, q=q, k=k, v=v, o=o, sink_col=sink_col)


def _attn_layer_bwd(dy, sv, w, tabs, M, tag):
    cos, sin = tabs
    QW = 4 * w["w_o"].shape[1]
    HD = QW // ATT_HEADS
    do = _mm(dy, w["w_o"], tb=True, b_chip=0, out_dtype=CDT, name=f"{tag}_dwo_x")
    g = {"w_o": _mm(sv["o"], dy, ta=True, out_chip=0, out_dtype=CDT, name=f"{tag}_dwo_w")}
    dq, dkc, dvc, dkw, dvw, dsink = _attn_bwd(sv["q"], sv["k"], sv["v"], sv["sink_col"], do, M=M, name=f"{tag}_attn_bwd")
    nbc = M // BLK
    dk = jnp.concatenate([dkc, _window_combine(dkw, nbc=nbc, name=f"{tag}_dk_comb")], axis=0)
    dv = jnp.concatenate([dvc, _window_combine(dvw, nbc=nbc, name=f"{tag}_dv_comb")], axis=0)
    dqkv = _rope_bwd(dq, dk, dv, cos, sin, HD=HD, name=f"{tag}_rope_bwd")
    dh = _mm(dqkv, w["w_qkv"], tb=True, b_chip=1, out_dtype=CDT, name=f"{tag}_dqkv_x")
    g["w_qkv"] = _mm(sv["h"], dqkv, ta=True, out_chip=1, out_dtype=CDT, name=f"{tag}_dqkv_w")
    g["sink"] = jnp.sum(dsink.reshape(ATT_HEADS, BLK), axis=1)
    return dh, g


def _gla_layer_fwd(h, w, M, tag, ride=None):
    D = h.shape[1]
    proj = _mm(h, w["w_in"], b_chip=1, out_dtype=CDT, ride=ride, name=f"{tag}_in")
    lr = _mm(h, w["w1x"], out_dtype=CDT, name=f"{tag}_in_gate")
    gf, gb = _gla_gate_fwd(lr, w["w2"], w["gbias"], D=D, name=f"{tag}_gate")
    of, stf = _gla_scan_fwd(proj, gf, rev=False, M=M, D=D, name=f"{tag}_scan_f")
    ob, stb = _gla_scan_fwd(proj, gb, rev=True, M=M, D=D, name=f"{tag}_scan_b")
    yg = _gla_out_fwd(of, ob, proj, w["onorm"], D=D, name=f"{tag}_out")
    y = _mm(yg, w["w_o"], b_chip=0, out_dtype=CDT, name=f"{tag}_wo")
    return y, dict(h=h, proj=proj, lr=lr, gf=gf, gb=gb, of=of, ob=ob, stf=stf, stb=stb, yg=yg)


def _gla_layer_bwd(dy, sv, w, M, tag):
    D = dy.shape[1]
    dyg = _mm(dy, w["w_o"], tb=True, b_chip=0, out_dtype=CDT, name=f"{tag}_dwo_x")
    g = {"w_o": _mm(sv["yg"], dy, ta=True, out_chip=0, out_dtype=CDT, name=f"{tag}_dwo_w")}
    do, dr, dgn = _gla_out_bwd(sv["of"], sv["ob"], sv["proj"], w["onorm"], dyg, D=D, name=f"{tag}_out_bwd")
    df = _gla_scan_bwd(sv["proj"], sv["gf"], sv["stf"], do, rev=False, M=M, D=D, name=f"{tag}_scan_f_bwd")
    db = _gla_scan_bwd(sv["proj"], sv["gb"], sv["stb"], do, rev=True, M=M, D=D, name=f"{tag}_scan_b_bwd")
    dproj, dlr, dw2, dbias = _gla_proj_bwd(sv["lr"], w["w2"], w["gbias"], *df, *db, dr, D=D, name=f"{tag}_proj_bwd")
    dh = _mm(dproj, w["w_in"], tb=True, b_chip=1, extra=(dlr, w["w1x"]), out_dtype=CDT, name=f"{tag}_din_x")
    g["w_in"] = _mm(sv["h"], dproj, ta=True, out_chip=1, out_dtype=CDT, name=f"{tag}_din_w")
    g["w1x"] = _mm(sv["h"], dlr, ta=True, out_dtype=F32, name=f"{tag}_din_gate_w")
    g["w2"] = dw2
    g["gbias"] = dbias[0]
    g["onorm"] = dgn[0]
    return dh, g


def _ffn_fwd(h2, w, M, tag, rides):
    u = _mm(h2, w["w_up"], b_chip=1, out_dtype=CDT, ride=rides.get("up"), name=f"{tag}_up")
    act = _conv_gate_fwd(u, w["conv_w"], w["conv_b"], M=M, name=f"{tag}_conv")
    f = _mm(act, w["w_down"], b_chip=0, out_dtype=CDT, ride=rides.get("down"), name=f"{tag}_down")
    return f, dict(h2=h2, u=u, act=act)


def _ffn_bwd(dyf, sv, w, M, tag, rides):
    dact = _mm(dyf, w["w_down"], tb=True, b_chip=0, out_dtype=CDT, ride=rides.get("ddown_x"), name=f"{tag}_ddown_x")
    g = {"w_down": _mm(sv["act"], dyf, ta=True, out_chip=0, out_dtype=CDT, ride=rides.get("ddown_w"), name=f"{tag}_ddown_w")}
    duc, cacc = _conv_gate_bwd(sv["u"], dact, w["conv_w"], w["conv_b"], M=M, name=f"{tag}_conv_bwd")
    du = _conv_t(duc, w["conv_w"], M=M, name=f"{tag}_conv_t")
    dh2 = _mm(du, w["w_up"], tb=True, a_split=True, b_chip=1, out_dtype=CDT, ride=rides.get("dup_x"), name=f"{tag}_dup_x")
    g["w_up"] = _mm(sv["h2"], du, ta=True, b_chip=1, out_chip=1, out_dtype=CDT, name=f"{tag}_dup_w")
    g["conv_w"] = jnp.concatenate([cacc[0, 0:3], cacc[1, 0:3]], axis=1)
    g["conv_b"] = jnp.concatenate([cacc[0, 3], cacc[1, 3]], axis=0)
    return dh2, g


def _norm_grads(acc, gain, scale):
    p = acc[:, 1]
    return acc[:, 0], p * gain, jnp.sum(p * (1.0 + scale[:, 0]), axis=0)


def _local_step(x, tgt, mods, weights_of, final_g, *, M, fwd_rides=None, bwd_rides=None, on_grads=None):
    T, D = x.shape
    L = mods.shape[0]
    HD = D // ATT_HEADS
    tabs = _rope_tables(T - M, M, HD)
    sel = lambda i, k: mods[i][:, k:k + 1, :]
    saved = []
    xs, y_prev, gate_prev = x, None, None
    for i in range(L):
        w = weights_of(i)
        rides = {} if fwd_rides is None else fwd_rides(i)
        x_in, h = _norm_fwd(xs, y_prev, gate_prev, w["g_mix"], sel(i, 0), sel(i, 1), M=M, name=f"l{i}_norm_mix")
        if "w_qkv" in w:
            y_mix, sm = _attn_layer_fwd(h, w, tabs, M, f"l{i}", rides.get("mix"))
        else:
            y_mix, sm = _gla_layer_fwd(h, w, M, f"l{i}", rides.get("mix"))
        x_mid, h2 = _norm_fwd(x_in, y_mix, sel(i, 2), w["g_ffn"], sel(i, 3), sel(i, 4), M=M, name=f"l{i}_norm_ffn")
        f, sf = _ffn_fwd(h2, w, M, f"l{i}", rides)
        saved.append(dict(x_in=x_in, x_mid=x_mid, y_mix=y_mix, f=f, sm=sm, sf=sf, w=w))
        xs, y_prev, gate_prev = x_mid, f, sel(i, 5)

    loss_parts, dx, dyf, acc = _final_loss(xs, y_prev, gate_prev, tgt, final_g, M=M, name="final_loss")
    loss = jnp.sum(loss_parts[:, 0, 0])
    d_final_g = acc[0, 0] + acc[1, 0]
    dmods = [None] * L
    grads = [None] * L
    dgate_ffn = acc[:, 2]
    for i in reversed(range(L)):
        sv = saved[i]
        w = sv["w"]
        dh2, g = _ffn_bwd(dyf, sv["sf"], w, M, f"l{i}", {} if bwd_rides is None else bwd_rides(i))
        dx, dy_mix, acc = _norm_bwd(sv["x_mid"], dh2, dx, w["g_ffn"], sel(i, 4), sv["y_mix"], sel(i, 2), M=M, name=f"l{i}_norm_ffn_bwd")
        dsh_f, dsc_f, g["g_ffn"] = _norm_grads(acc, w["g_ffn"], sel(i, 4))
        dgate_mix = acc[:, 2]
        if "w_qkv" in w:
            dh, gm = _attn_layer_bwd(dy_mix, sv["sm"], w, tabs, M, f"l{i}")
        else:
            dh, gm = _gla_layer_bwd(dy_mix, sv["sm"], w, M, f"l{i}")
        g.update(gm)
        if i > 0:
            dx, dyf, acc = _norm_bwd(sv["x_in"], dh, dx, w["g_mix"], sel(i, 1), saved[i - 1]["f"], sel(i - 1, 5), M=M, name=f"l{i}_norm_mix_bwd")
        else:
            dx, dyf, acc = _norm_bwd(sv["x_in"], dh, dx, w["g_mix"], sel(i, 1), None, None, M=M, name=f"l{i}_norm_mix_bwd")
        dsh_m, dsc_m, g["g_mix"] = _norm_grads(acc, w["g_mix"], sel(i, 1))
        dmods[i] = jnp.stack([dsh_m, dsc_m, dgate_mix, dsh_f, dsc_f, dgate_ffn], axis=1)
        dgate_ffn = acc[:, 2]
        grads[i] = g if on_grads is None else on_grads(i, g)
    return loss, dx, jnp.stack(dmods, axis=0), grads, d_final_g


ANY = pl.BlockSpec(memory_space=pl.ANY)


def _me():
    return lax.axis_index("x"), lax.axis_index("y"), lax.axis_index("c")


def _other_chips(mx, my):
    return [(1 - mx, my), (mx, 1 - my), (1 - mx, 1 - my)]


def _rcopy(src, dst, sems, k, dev):
    send_sems, recv_sems = sems
    return pltpu.make_async_remote_copy(src_ref=src, dst_ref=dst, send_sem=send_sems.at[k], recv_sem=recv_sems.at[k],
                                        device_id=dev, device_id_type=MESH)


def _all_gather8(x, *, name):
    m, n = x.shape

    def body(x_ref, out_ref, send_sems, recv_sems, local_sem):
        mx, my, mc = _me()
        sems = (send_sems, recv_sems)
        me, sib = (mx, my, mc), (mx, my, 1 - mc)
        chips = _other_chips(mx, my)
        blk = lambda d: out_ref.at[4 * d[0] + 2 * d[1] + d[2]]
        mine = pltpu.make_async_copy(x_ref, blk(me), local_sem)
        mine.start()
        first = [_rcopy(x_ref, blk(me), sems, 0, sib)]
        first += [_rcopy(x_ref, blk(me), sems, 1 + j, (*ch, mc)) for j, ch in enumerate(chips)]
        for cp in first:
            cp.start()
        passed = [_rcopy(blk((*ch, mc)), blk((*ch, mc)), sems, 4 + j, sib) for j, ch in enumerate(chips)]
        for j, ch in enumerate(chips):
            _rcopy(x_ref, blk((*ch, mc)), sems, 1 + j, me).wait_recv()
            passed[j].start()
        _rcopy(x_ref, blk(sib), sems, 0, me).wait_recv()
        for j, ch in enumerate(chips):
            _rcopy(x_ref, blk((*ch, 1 - mc)), sems, 4 + j, me).wait_recv()
        for cp in first + passed:
            cp.wait_send()
        mine.wait()

    return pl.pallas_call(
        body, name=name, out_shape=jax.ShapeDtypeStruct((8, m, n), x.dtype), in_specs=[ANY], out_specs=ANY,
        scratch_shapes=[pltpu.SemaphoreType.DMA((7,)), pltpu.SemaphoreType.DMA((7,)), pltpu.SemaphoreType.DMA],
    )(x)


ROW_TILES = (512, 352, 256, 128)


def _sem_pairs(n):
    return [pltpu.SemaphoreType.DMA((n,)), pltpu.SemaphoreType.DMA((n,))]


def _place(w, layer, pos, *, name):
    _, a, b = w.shape
    tr = _pick(a, ROW_TILES)

    def body(pos_ref, w_ref, o_ref):
        o_ref[...] = w_ref[...].astype(o_ref.dtype)

    return pl.pallas_call(
        body, name=name, out_shape=jax.ShapeDtypeStruct((4, a, b), CDT),
        grid_spec=pltpu.PrefetchScalarGridSpec(
            num_scalar_prefetch=1, grid=(a // tr,),
            in_specs=[pl.BlockSpec((None, tr, b), lambda i, pos: (layer, i, 0))],
            out_specs=pl.BlockSpec((None, tr, b), lambda i, pos: (pos[0], i, 0))),
        compiler_params=_cparams(("parallel",)),
    )(pos, w)


def _gather_layer(bufs, *, name):
    n = len(bufs)

    def body(*refs):
        outs = refs[n:2 * n]
        sems = (refs[2 * n], refs[2 * n + 1])
        mx, my, mc = _me()
        me, sib = (mx, my, mc), (mx, my, 1 - mc)
        chips = _other_chips(mx, my)
        p = 2 * mx + my
        qs = [2 * ch[0] + ch[1] for ch in chips]
        halves = [(pl.ds(mc * (o.shape[1] // 2), o.shape[1] // 2), pl.ds((1 - mc) * (o.shape[1] // 2), o.shape[1] // 2)) for o in outs]
        first = []
        for t, o in enumerate(outs):
            mine = halves[t][0]
            first += [_rcopy(o.at[p, mine], o.at[p, mine], sems, 6 * t + j, (*ch, mc)) for j, ch in enumerate(chips)]
        for cp in first:
            cp.start()
        passed = []
        for j in range(3):
            for t, o in enumerate(outs):
                mine = halves[t][0]
                _rcopy(o.at[qs[j], mine], o.at[qs[j], mine], sems, 6 * t + j, me).wait_recv()
                fwd = _rcopy(o.at[qs[j], mine], o.at[qs[j], mine], sems, 6 * t + 3 + j, sib)
                fwd.start()
                passed.append(fwd)
        for j in range(3):
            for t, o in enumerate(outs):
                theirs = halves[t][1]
                _rcopy(o.at[qs[j], theirs], o.at[qs[j], theirs], sems, 6 * t + 3 + j, me).wait_recv()
        for cp in first + passed:
            cp.wait_send()

    return pl.pallas_call(
        body, name=name, out_shape=[jax.ShapeDtypeStruct(b.shape, b.dtype) for b in bufs],
        in_specs=[ANY] * n, out_specs=[ANY] * n, input_output_aliases={t: t for t in range(n)},
        scratch_shapes=_sem_pairs(6 * n),
    )(*bufs)


def _gather_ici_plan(outs):
    mx, my, mc = _me()
    p = 2 * mx + my
    for t, o in enumerate(outs):
        ah = o.shape[1] // 2
        mine = pl.ds(mc * ah, ah)
        for j, ch in enumerate(_other_chips(mx, my)):
            yield 3 * t + j, o.at[p, mine], o.at[2 * ch[0] + ch[1], mine], (*ch, mc)


def _ride_gather(bufs):
    def start(r_in, r_out, sems):
        for k, src, _, dev in _gather_ici_plan(r_out):
            _rcopy(src, src, sems, k, dev).start()

    def finish(r_in, r_out, sems):
        for k, _, land, _ in _gather_ici_plan(r_out):
            _rcopy(land, land, sems, k, _me()).wait_recv()
        for k, src, _, dev in _gather_ici_plan(r_out):
            _rcopy(src, src, sems, k, dev).wait_send()

    return dict(ins=list(bufs), outs=[jax.ShapeDtypeStruct(b.shape, b.dtype) for b in bufs],
                alias={t: t for t in range(len(bufs))}, nsem=3 * len(bufs), start=start, finish=finish)


def _gather_d2d(bufs, *, name):
    n = len(bufs)

    def body(*refs):
        outs = refs[n:2 * n]
        sems = (refs[2 * n], refs[2 * n + 1])
        mx, my, mc = _me()
        sib = (mx, my, 1 - mc)
        qs = [2 * ch[0] + ch[1] for ch in _other_chips(mx, my)]
        cps = []
        for t, o in enumerate(outs):
            ah = o.shape[1] // 2
            for j, q in enumerate(qs):
                mine = o.at[q, pl.ds(mc * ah, ah)]
                cps.append((_rcopy(mine, mine, sems, 3 * t + j, sib), o.at[q, pl.ds((1 - mc) * ah, ah)], 3 * t + j))
        for cp, _, _ in cps:
            cp.start()
        for _, theirs, k in cps:
            _rcopy(theirs, theirs, sems, k, sib).wait_recv()
        for cp, _, _ in cps:
            cp.wait_send()

    return pl.pallas_call(
        body, name=name, out_shape=[jax.ShapeDtypeStruct(b.shape, b.dtype) for b in bufs],
        in_specs=[ANY] * n, out_specs=[ANY] * n, input_output_aliases={t: t for t in range(n)},
        scratch_shapes=_sem_pairs(3 * n),
    )(*bufs)


def _scatter_plan(ins, outs):
    mx, my, mc = _me()
    p = 2 * mx + my
    for t, (s, o) in enumerate(zip(ins, outs)):
        for j, ch in enumerate(_other_chips(mx, my)):
            q = 2 * ch[0] + ch[1]
            yield 3 * t + j, s.at[q], o.at[p], o.at[q], (*ch, mc)


def _ride_scatter(parts):
    def start(r_in, r_out, sems):
        for k, src, dst, _, dev in _scatter_plan(r_in, r_out):
            _rcopy(src, dst, sems, k, dev).start()

    def finish(r_in, r_out, sems):
        for k, _, _, land, _ in _scatter_plan(r_in, r_out):
            _rcopy(land, land, sems, k, _me()).wait_recv()
        for k, src, dst, _, dev in _scatter_plan(r_in, r_out):
            _rcopy(src, dst, sems, k, dev).wait_send()

    return dict(ins=list(parts), outs=[jax.ShapeDtypeStruct(s.shape, s.dtype) for s in parts], alias={},
                nsem=3 * len(parts), start=start, finish=finish)


def _rs_split(gs, *, name):
    n = len(gs)

    def body(*refs):
        ins, outs = refs[:n], refs[n:2 * n]
        sems = (refs[2 * n], refs[2 * n + 1])
        mx, my, mc = _me()
        sib = (mx, my, 1 - mc)
        cps = []
        for t, (g, o) in enumerate(zip(ins, outs)):
            ah = g.shape[1] // 2
            cps.append(_rcopy(g.at[:, pl.ds((1 - mc) * ah, ah), :], o, sems, t, sib))
        for cp in cps:
            cp.start()
        for cp in cps:
            cp.wait_recv()
        for cp in cps:
            cp.wait_send()

    return pl.pallas_call(
        body, name=name, out_shape=[jax.ShapeDtypeStruct((4, g.shape[1] // 2, g.shape[2]), g.dtype) for g in gs],
        in_specs=[ANY] * n, out_specs=[ANY] * n, scratch_shapes=_sem_pairs(n),
    )(*gs)


def _rs_add(g, got, pos, *, name):
    _, a, b = g.shape
    ah = a // 2
    tr = _pick(ah, ROW_TILES)
    nb = ah // tr

    def body(pos_ref, g_ref, r_ref, o_ref):
        o_ref[...] = (g_ref[...].astype(F32) + r_ref[...].astype(F32)).astype(o_ref.dtype)

    blk = pl.BlockSpec((None, tr, b), lambda q, i, pos: (q, i, 0))
    return pl.pallas_call(
        body, name=name, out_shape=jax.ShapeDtypeStruct((4, ah, b), g.dtype),
        grid_spec=pltpu.PrefetchScalarGridSpec(
            num_scalar_prefetch=1, grid=(4, nb),
            in_specs=[pl.BlockSpec((None, tr, b), lambda q, i, pos: (q, pos[1] * nb + i, 0)), blk], out_specs=blk),
        compiler_params=_cparams(("parallel", "parallel")),
    )(pos, g, got)


def _rs_scatter(ps, *, name):
    n = len(ps)

    def body(*refs):
        ins, outs = refs[:n], refs[n:2 * n]
        sems = (refs[2 * n], refs[2 * n + 1])
        mx, my, mc = _me()
        me = (mx, my, mc)
        chips = _other_chips(mx, my)
        p = 2 * mx + my
        sends = []
        for t, (s, o) in enumerate(zip(ins, outs)):
            sends += [_rcopy(s.at[2 * ch[0] + ch[1]], o.at[p], sems, 3 * t + j, (*ch, mc)) for j, ch in enumerate(chips)]
        for cp in sends:
            cp.start()
        for t, (s, o) in enumerate(zip(ins, outs)):
            for j, ch in enumerate(chips):
                _rcopy(s.at[p], o.at[2 * ch[0] + ch[1]], sems, 3 * t + j, me).wait_recv()
        for cp in sends:
            cp.wait_send()

    return pl.pallas_call(
        body, name=name, out_shape=[jax.ShapeDtypeStruct(s.shape, s.dtype) for s in ps],
        in_specs=[ANY] * n, out_specs=[ANY] * n, scratch_shapes=_sem_pairs(3 * n),
    )(*ps)


def _rs_sum(part, recv, buf, layer, pos, *, name):
    _, ah, b = part.shape
    tr = _pick(ah, ROW_TILES)
    nb = ah // tr

    def body(pos_ref, p_ref, r0, r1, r2, buf_ref, o_ref):
        o_ref[...] = ((p_ref[...].astype(F32) + r0[...].astype(F32)) + r1[...].astype(F32)) + r2[...].astype(F32)

    other = lambda k: pl.BlockSpec((None, tr, b), lambda i, pos: (jnp.where(pos[0] <= k, k + 1, k), i, 0))
    return pl.pallas_call(
        body, name=name, out_shape=jax.ShapeDtypeStruct(buf.shape, buf.dtype),
        grid_spec=pltpu.PrefetchScalarGridSpec(
            num_scalar_prefetch=1, grid=(nb,),
            in_specs=[pl.BlockSpec((None, tr, b), lambda i, pos: (pos[0], i, 0)), other(0), other(1), other(2), ANY],
            out_specs=pl.BlockSpec((None, tr, b), lambda i, pos: (layer, pos[1] * nb + i, 0))),
        input_output_aliases={5: 0},
        compiler_params=_cparams(("parallel",)),
    )(pos, part, recv, recv, recv, buf)


def _rs_share(bufs, layers, *, name):
    n = len(bufs)

    def body(*refs):
        outs = refs[n:2 * n]
        sems = (refs[2 * n], refs[2 * n + 1])
        mx, my, mc = _me()
        sib = (mx, my, 1 - mc)
        cps = []
        for t, o in enumerate(outs):
            ah = o.shape[1] // 2
            mine = o.at[layers[t], pl.ds(mc * ah, ah)]
            cps.append((_rcopy(mine, mine, sems, t, sib), o.at[layers[t], pl.ds((1 - mc) * ah, ah)]))
        for cp, _ in cps:
            cp.start()
        for t, (cp, theirs) in enumerate(cps):
            _rcopy(theirs, theirs, sems, t, sib).wait_recv()
        for cp, _ in cps:
            cp.wait_send()

    return pl.pallas_call(
        body, name=name, out_shape=[jax.ShapeDtypeStruct(b.shape, b.dtype) for b in bufs],
        in_specs=[ANY] * n, out_specs=[ANY] * n, input_output_aliases={t: t for t in range(n)},
        scratch_shapes=_sem_pairs(n),
    )(*bufs)


def _sum_lead(a, *, name):
    n, R, W = a.shape
    tr = _pick(R, (PACK_ROWS,))
    specs = [pl.BlockSpec((1, tr, W), functools.partial(lambda i, q: (q, i, 0), q=q)) for q in range(n)]

    def body(*refs):
        acc = refs[0][0].astype(F32)
        for r in refs[1:n]:
            acc = acc + r[0].astype(F32)
        refs[n][...] = acc

    return pl.pallas_call(
        body, name=name, grid=(R // tr,), in_specs=specs, out_specs=pl.BlockSpec((tr, W), lambda i: (i, 0)),
        out_shape=jax.ShapeDtypeStruct((R, W), F32), compiler_params=_cparams(("parallel",)),
    )(*([a] * n))


SMALL_SHARDED = (("ffn_conv_w", 2), ("gla_gf_w1", 1), ("gla_gf_w2", 2), ("gla_gf_b", 1), ("gla_gb_w1", 1), ("gla_gb_w2", 2),
                 ("gla_gb_b", 1), ("gla_onorm_g", 1))


def _rows_of(flat, width):
    rows = -(-flat.shape[0] // (8 * width)) * 8
    return jnp.pad(flat, (0, rows * width - flat.shape[0])).reshape(rows, width)


def _size(shape):
    n = 1
    for s in shape:
        n *= s
    return n


def _gather_small(shards):
    flat = jnp.concatenate([shards[name].astype(F32).reshape(-1) for name, _ in SMALL_SHARDED])
    got = _all_gather8(_rows_of(flat, SMALL_W), name="gather_small_w")[0::2].reshape(4, -1)
    full, off = {}, 0
    for name, ax in SMALL_SHARDED:
        shape = shards[name].shape
        n = _size(shape)
        seg = jnp.moveaxis(got[:, off:off + n].reshape((4,) + shape), 0, ax)
        full[name] = seg.reshape(shape[:ax] + (4 * shape[ax],) + shape[ax + 1:])
        off += n
    return full


WEIGHTS = ("c_ctx", "ada_w", "ada_b", "norm_mix_g", "norm_ffn_g", "ffn_w_up", "ffn_conv_w", "ffn_conv_b", "ffn_w_down",
           "attn_w_qkv", "attn_sink", "attn_w_o", "gla_w_in", "gla_gf_w1", "gla_gf_w2", "gla_gf_b", "gla_gb_w1", "gla_gb_w2",
           "gla_gb_b", "gla_onorm_g", "gla_w_o", "final_norm_g")
REPLICATED = ("norm_mix_g", "norm_ffn_g", "ffn_conv_b", "attn_sink", "final_norm_g", "c_ctx")
SMALL_W = 2048
ROWS16 = 16


def _layer_big(i):
    j = i // 2
    mixer = [("w_qkv", "attn_w_qkv", j), ("w_o", "attn_w_o", j)] if i % 2 == 0 else [("w_in", "gla_w_in", j), ("w_o", "gla_w_o", j)]
    return [("w_up", "ffn_w_up", i), ("w_down", "ffn_w_down", i)] + mixer


def _layer_weights(i, big, small, rep):
    D = rep["norm_mix_g"].shape[1]
    j = i // 2
    w = dict(g_mix=rep["norm_mix_g"][i][None], g_ffn=rep["norm_ffn_g"][i][None], conv_w=small["ffn_conv_w"][i],
             conv_b=rep["ffn_conv_b"][i][None], **big)
    if i % 2 == 0:
        w["sink"] = rep["attn_sink"][j]
    else:
        r = GATE_RANK
        w2 = jnp.zeros((128, D), F32)
        w2 = w2.at[0:r, 0:D // 2].set(small["gla_gf_w2"][j]).at[r:2 * r, D // 2:].set(small["gla_gb_w2"][j])
        w1x = jnp.concatenate([small["gla_gf_w1"][j], small["gla_gb_w1"][j], jnp.zeros((D, 128 - 2 * r), F32)], axis=1)
        w.update(w1x=w1x.astype(CDT), w2=w2.astype(CDT), gbias=jnp.concatenate([small["gla_gf_b"][j], small["gla_gb_b"][j]])[None],
                 onorm=small["gla_onorm_g"][j][None])
    return w


def _small_grads(grads, D):
    att = [g for g in grads if "sink" in g]
    gla = [g for g in grads if "w1x" in g]
    st = lambda xs: jnp.stack(xs, axis=0)
    r = GATE_RANK
    return {
        "ffn_conv_w": st([g["conv_w"] for g in grads]),
        "gla_gf_w1": st([g["w1x"][:, 0:r] for g in gla]), "gla_gb_w1": st([g["w1x"][:, r:2 * r] for g in gla]),
        "gla_gf_w2": st([g["w2"][0:r, :D // 2] for g in gla]), "gla_gb_w2": st([g["w2"][r:2 * r, D // 2:] for g in gla]),
        "gla_gf_b": st([g["gbias"][:D // 2] for g in gla]), "gla_gb_b": st([g["gbias"][D // 2:] for g in gla]),
        "gla_onorm_g": st([g["onorm"] for g in gla]),
        "norm_mix_g": st([g["g_mix"] for g in grads]), "norm_ffn_g": st([g["g_ffn"] for g in grads]),
        "ffn_conv_b": st([g["conv_b"] for g in grads]), "attn_sink": st([g["sink"] for g in att]),
    }


def kernel(x, c, ctx, c_ctx, ada_w, ada_b, norm_mix_g, norm_ffn_g, ffn_w_up, ffn_conv_w, ffn_conv_b, ffn_w_down, attn_w_qkv, attn_sink, attn_w_o, gla_w_in, gla_gf_w1, gla_gf_w2, gla_gf_b, gla_gb_w1, gla_gb_w2, gla_gb_b, gla_onorm_g, gla_w_o, final_norm_g, loss_target, m_c_ctx, m_ada_w, m_ada_b, m_norm_mix_g, m_norm_ffn_g, m_ffn_w_up, m_ffn_conv_w, m_ffn_conv_b, m_ffn_w_down, m_attn_w_qkv, m_attn_sink, m_attn_w_o, m_gla_w_in, m_gla_gf_w1, m_gla_gf_w2, m_gla_gf_b, m_gla_gb_w1, m_gla_gb_w2, m_gla_gb_b, m_gla_onorm_g, m_gla_w_o, m_final_norm_g, v_c_ctx, v_ada_w, v_ada_b, v_norm_mix_g, v_norm_ffn_g, v_ffn_w_up, v_ffn_conv_w, v_ffn_conv_b, v_ffn_w_down, v_attn_w_qkv, v_attn_sink, v_attn_w_o, v_gla_w_in, v_gla_gf_w1, v_gla_gf_w2, v_gla_gf_b, v_gla_gb_w1, v_gla_gb_w2, v_gla_gb_b, v_gla_onorm_g, v_gla_w_o, v_final_norm_g):
    wts = dict(c_ctx=c_ctx, ada_w=ada_w, ada_b=ada_b, norm_mix_g=norm_mix_g, norm_ffn_g=norm_ffn_g, ffn_w_up=ffn_w_up,
               ffn_conv_w=ffn_conv_w, ffn_conv_b=ffn_conv_b, ffn_w_down=ffn_w_down, attn_w_qkv=attn_w_qkv, attn_sink=attn_sink,
               attn_w_o=attn_w_o, gla_w_in=gla_w_in, gla_gf_w1=gla_gf_w1, gla_gf_w2=gla_gf_w2, gla_gf_b=gla_gf_b,
               gla_gb_w1=gla_gb_w1, gla_gb_w2=gla_gb_w2, gla_gb_b=gla_gb_b, gla_onorm_g=gla_onorm_g, gla_w_o=gla_w_o,
               final_norm_g=final_norm_g)
    mom_m = dict(zip(WEIGHTS, (m_c_ctx, m_ada_w, m_ada_b, m_norm_mix_g, m_norm_ffn_g, m_ffn_w_up, m_ffn_conv_w, m_ffn_conv_b,
                               m_ffn_w_down, m_attn_w_qkv, m_attn_sink, m_attn_w_o, m_gla_w_in, m_gla_gf_w1, m_gla_gf_w2,
                               m_gla_gf_b, m_gla_gb_w1, m_gla_gb_w2, m_gla_gb_b, m_gla_onorm_g, m_gla_w_o, m_final_norm_g)))
    mom_v = dict(zip(WEIGHTS, (v_c_ctx, v_ada_w, v_ada_b, v_norm_mix_g, v_norm_ffn_g, v_ffn_w_up, v_ffn_conv_w, v_ffn_conv_b,
                               v_ffn_w_down, v_attn_w_qkv, v_attn_sink, v_attn_w_o, v_gla_w_in, v_gla_gf_w1, v_gla_gf_w2,
                               v_gla_gf_b, v_gla_gb_w1, v_gla_gb_w2, v_gla_gb_b, v_gla_onorm_g, v_gla_w_o, v_final_norm_g)))
    L, D, W6 = ada_w.shape[0], ada_w.shape[1], ada_w.shape[2]
    M = ctx.shape[1]
    mx, my, mc = _me()
    chip = 2 * mx + my
    batch = 4 * mx + 2 * my + mc

    crow = jnp.concatenate([c.astype(F32), jnp.zeros((7, D), F32)], axis=0)
    call = _all_gather8(crow, name="gather_c")[:, 0, :]
    s16 = jnp.concatenate([jax.nn.silu(call), jax.nn.silu(c_ctx)[None], jnp.zeros((ROWS16 - 9, D), F32)], axis=0)
    s16c = s16.astype(CDT)
    ada_c = ada_w.astype(CDT)
    mod_cols = jnp.concatenate([_mm(s16c, ada_c[i], out_dtype=F32, name=f"mods_l{i}") for i in range(L)], axis=0)
    mod_all = _all_gather8(mod_cols, name="gather_mods")
    mod_all = mod_all.reshape(4, 2, L, ROWS16, W6)[:, 0]
    mod_all = jnp.moveaxis(mod_all, 0, 2).reshape(L, ROWS16, 4 * W6) + ada_b[:, None, :]
    mod_mine = jnp.stack([mod_all[:, 8], lax.dynamic_index_in_dim(mod_all, batch, axis=1, keepdims=False)], axis=1)
    mods = mod_mine.reshape(L, 2, N_MOD, D)

    pos = jnp.stack([chip, mc]).astype(jnp.int32)
    small_w = _gather_small({name: wts[name] for name, _ in SMALL_SHARDED})
    def place(i):
        return [_place(wts[name], j, pos, name=f"l{i}_place_{key}") for key, name, j in _layer_big(i)]

    gathered = {0: _gather_layer(place(0), name="l0_gather_w")}
    carried = {}

    def weights_of(i):
        if i not in gathered:
            r = carried.pop(i)
            gathered[i] = _gather_d2d(r["up"]["result"] + r["down"]["result"] + r["mix"]["result"], name=f"l{i}_gather_d2d")
        return _layer_weights(i, {key: buf for (key, _, _), buf in zip(_layer_big(i), gathered[i])}, small_w, wts)

    def fwd_rides(i):
        if i + 1 >= L:
            return {}
        bufs = place(i + 1)
        carried[i + 1] = dict(up=_ride_gather(bufs[0:1]), down=_ride_gather(bufs[1:2]), mix=_ride_gather(bufs[2:4]))
        return carried[i + 1]

    red = {name: jnp.zeros(wts[name].shape, F32) for name in ("ffn_w_up", "ffn_w_down", "attn_w_qkv", "attn_w_o", "gla_w_in", "gla_w_o")}
    pending = {}

    def bwd_rides(i):
        if i + 1 not in pending:
            return {}
        parts = pending[i + 1]["parts"]
        r = dict(dup_x=_ride_scatter(parts[0:1]), ddown_x=_ride_scatter(parts[1:2]), ddown_w=_ride_scatter(parts[2:4]))
        pending[i + 1]["rides"] = r
        return r

    def finish_reduce(i):
        keys = _layer_big(i)
        parts, r = pending[i]["parts"], pending.pop(i).get("rides")
        if r is None:
            recvs = _rs_scatter(parts, name=f"l{i}_rs_scatter")
        else:
            recvs = r["dup_x"]["result"] + r["ddown_x"]["result"] + r["ddown_w"]["result"]
        outs = [_rs_sum(part, recv, red[name], j, pos, name=f"l{i}_rs_sum_{key}") for (key, name, j), part, recv in zip(keys, parts, recvs)]
        outs = _rs_share(outs, [j for _, _, j in keys], name=f"l{i}_rs_share")
        for (_, name, _), out in zip(keys, outs):
            red[name] = out

    def reduce_layer(i, g):
        if i + 1 in pending:
            finish_reduce(i + 1)
        keys = _layer_big(i)
        gs = [g.pop(key) for key, _, _ in keys]
        gots = _rs_split(gs, name=f"l{i}_rs_split")
        pending[i] = dict(parts=[_rs_add(gv, got, pos, name=f"l{i}_rs_add_{key}") for (key, _, _), gv, got in zip(keys, gs, gots)])
        return g

    xcat = jnp.concatenate([ctx[0], x[0]], axis=0)
    loss, dx, dmods, grads, d_final_g = _local_step(xcat, loss_target[0], mods, weights_of, final_norm_g[None], M=M,
                                                    fwd_rides=fwd_rides, bwd_rides=bwd_rides, on_grads=reduce_layer)
    finish_reduce(0)
    loss = lax.psum(loss, ("x", "y", "c"))
    grad_x = dx[M:][None]

    dm_all = _all_gather8(dmods.reshape(L * 2, N_MOD * D), name="gather_dmods")
    dm_sum = _sum_lead(dm_all, name="dmods_sum").reshape(L, 2, N_MOD * D)
    dm_rows = dm_all.reshape(8, L, 2, N_MOD * D)[:, :, 1]
    dm16 = jnp.concatenate([jnp.moveaxis(dm_rows, 0, 1), dm_sum[:, 0:1], jnp.zeros((L, ROWS16 - 9, N_MOD * D), F32)], axis=1)
    dm16 = lax.dynamic_slice_in_dim(dm16, chip * W6, W6, axis=2).astype(CDT)
    g_ada_w = jnp.stack([_mm(s16c, dm16[i], ta=True, out_dtype=F32, name=f"dada_w_l{i}") for i in range(L)], axis=0)
    ds16 = _mm(dm16[0], ada_c[0], tb=True, out_dtype=F32, name="dcond_l0")
    for i in range(1, L):
        ds16 = ds16 + _mm(dm16[i], ada_c[i], tb=True, out_dtype=F32, name=f"dcond_l{i}")
    d_sctx = ds16[8] * jnp.where(mc == 0, 1.0, 0.0)

    gfull = _small_grads(grads, D)
    gfull["final_norm_g"] = d_final_g
    gfull["c_ctx"] = d_sctx

    small_names = list(REPLICATED) + [name for name, _ in SMALL_SHARDED]
    flat = jnp.concatenate([gfull[name].astype(F32).reshape(-1) for name in small_names])
    small = _sum_lead(_all_gather8(_rows_of(flat, SMALL_W), name="gather_small_g"), name="small_sum").reshape(-1)
    off = 0
    for name in small_names:
        shape = gfull[name].shape
        red[name] = small[off:off + _size(shape)].reshape(shape)
        off += _size(shape)
    for name, ax in SMALL_SHARDED:
        shape = wts[name].shape
        g4 = red[name].reshape(shape[:ax] + (4, shape[ax]) + shape[ax + 1:])
        red[name] = lax.dynamic_index_in_dim(g4, chip, axis=ax, keepdims=False)
    red["c_ctx"] = red["c_ctx"] * _dsilu(c_ctx)
    red["ada_w"] = g_ada_w
    red["ada_b"] = dm_sum[:, 0] + dm_sum[:, 1]

    deltas, new_m, new_v = {}, {}, {}
    for name in WEIGHTS:
        w = wts[name]
        view = (lambda a: a.reshape(-1, a.shape[-1])) if w.ndim > 1 else (lambda a: a.reshape(1, -1))
        d, m2, v2 = _adamw(view(w), view(red[name]), view(mom_m[name]), view(mom_v[name]), name=f"adamw_{name}")
        deltas[name], new_m[name], new_v[name] = d.reshape(w.shape), m2.reshape(w.shape), v2.reshape(w.shape)
    return (loss, grad_x, *[red[n] for n in WEIGHTS], *[deltas[n] for n in WEIGHTS], *[new_m[n] for n in WEIGHTS],
            *[new_v[n] for n in WEIGHTS])
```

```python
import functools

import jax
import jax.numpy as jnp
from jax import lax
from jax.experimental import pallas as pl
from jax.experimental.pallas import tpu as pltpu

F32 = jnp.float32
CDT = jnp.bfloat16
VMEM_LIMIT = 56 * 1024 * 1024
MESH = pl.DeviceIdType.MESH

ATT_HEADS = 16
ATT_KV = 4
ATT_G = ATT_HEADS // ATT_KV
BLK = 128
GRID_W = 64
ROPE_BASE = 10000.0
GLA_H = 4
GATE_RANK = 16
GATE_NORM = 16.0
CHUNK = 64
EPS = 1e-6
N_MOD = 6
LR, B1, B2, AEPS, WD, STEP = 0.001, 0.9, 0.999, 1e-08, 0.01, 10
PACK_ROWS = 512

NN = (((1,), (0,)), ((), ()))
NT = (((1,), (1,)), ((), ()))
TN = (((0,), (0,)), ((), ()))


def _dg(a, b, dims):
    return lax.dot_general(a, b, dims, preferred_element_type=F32)


def _pick(dim, cands):
    for c in cands:
        if dim % c == 0:
            return c
    return dim


def _cparams(sem):
    return pltpu.CompilerParams(dimension_semantics=sem, vmem_limit_bytes=VMEM_LIMIT)


def _pcall(body, *, name, grid, in_specs, out_specs, out_shape, scratch_shapes=(), sem, ride=None):
    if ride is None:
        return pl.pallas_call(body, name=name, grid=grid, in_specs=list(in_specs), out_specs=out_specs, out_shape=out_shape,
                              scratch_shapes=list(scratch_shapes), compiler_params=_cparams(sem))
    multi = isinstance(out_shape, (tuple, list))
    o_specs = list(out_specs) if multi else [out_specs]
    o_shapes = list(out_shape) if multi else [out_shape]
    n_in, n_out, n_scr = len(in_specs), len(o_specs), len(scratch_shapes)
    n_rin, n_rout = len(ride["ins"]), len(ride["outs"])
    total = 1
    for extent in grid:
        total *= extent

    def carrying(*refs):
        o0 = n_in + n_rin
        s0 = o0 + n_out + n_rout
        r_in, r_out, sems = refs[n_in:o0], refs[o0 + n_out:s0], (refs[-2], refs[-1])
        step = 0
        for axis, extent in enumerate(grid):
            step = step * extent + pl.program_id(axis)

        @pl.when(step == 0)
        def _():
            ride["start"](r_in, r_out, sems)

        body(*refs[:n_in], *refs[o0:o0 + n_out], *refs[s0:s0 + n_scr])

        @pl.when(step == total - 1)
        def _():
            ride["finish"](r_in, r_out, sems)

    hbm = pl.BlockSpec(memory_space=pl.ANY)
    call = pl.pallas_call(
        carrying, name=name, grid=grid, in_specs=list(in_specs) + [hbm] * n_rin, out_specs=o_specs + [hbm] * n_rout,
        out_shape=o_shapes + list(ride["outs"]), input_output_aliases={n_in + i: n_out + o for i, o in ride["alias"].items()},
        scratch_shapes=list(scratch_shapes) + [pltpu.SemaphoreType.DMA((ride["nsem"],)), pltpu.SemaphoreType.DMA((ride["nsem"],))],
        compiler_params=_cparams(("arbitrary",) * len(grid)))

    def run(*args):
        res = call(*args, *ride["ins"])
        ride["result"] = list(res[n_out:])
        return tuple(res[:n_out]) if multi else res[0]

    return run


def _sigmoid(x):
    return 1.0 / (1.0 + jnp.exp(-x))


def _silu(x):
    return x * _sigmoid(x)


def _silu_pair(x):
    s = _sigmoid(x)
    return x * s, s * (1.0 + x * (1.0 - s))


MM_VMEM_BUDGET = 40 * 1024 * 1024
TILE_M = (2048, 1408, 1088, 1024, 544, 512, 256, 128)
TILE_N = (2048, 1536, 1408, 1024, 768, 512, 256, 128)
TILE_K = (2176, 2048, 1408, 1088, 1024, 768, 512)


def _mm_tiles(m_unit, n_unit, k_unit, out_bytes):
    best = None
    for tm in [c for c in TILE_M if m_unit % c == 0] or [m_unit]:
        for tn in [c for c in TILE_N if n_unit % c == 0] or [n_unit]:
            for tk in [c for c in TILE_K if k_unit % c == 0] or [k_unit]:
                vmem = 4 * tk * (tm + tn) + tm * tn * (2 * out_bytes + 4)
                if vmem > MM_VMEM_BUDGET:
                    continue
                key = (tm * tn / (tm + tn), tk)
                if best is None or key > best[0]:
                    best = (key, (tm, tn, tk))
    assert best is not None, (m_unit, n_unit, k_unit)
    return best[1]


def _mm(a, b, *, ta=False, tb=False, out_dtype=F32, name, a_split=False, b_chip=None, out_chip=None, extra=None, ride=None):
    if a_split:
        assert not ta
        M, K = a.shape[1], a.shape[0] * a.shape[2]
    elif ta:
        K, M = a.shape
    else:
        M, K = a.shape
    bs = list(b.shape) if b_chip is None else list(b.shape[1:])
    if b_chip is not None:
        bs[b_chip] *= b.shape[0]
    N, K2 = bs if tb else bs[::-1]
    assert K == K2, (a.shape, b.shape, ta, tb, b_chip)
    m_unit, n_unit, k_unit = M, N, K
    if a_split:
        k_unit = a.shape[2]
    if b_chip is not None:
        if (b_chip == 0) == tb:
            n_unit = N // b.shape[0]
        else:
            k_unit = min(k_unit, K // b.shape[0])
    if out_chip == 0:
        m_unit = M // 4
    elif out_chip == 1:
        n_unit = min(n_unit, N // 4)
    tm, tn, tk = _mm_tiles(m_unit, n_unit, k_unit, jnp.dtype(out_dtype).itemsize)
    nk = K // tk
    dims = TN if ta else (NT if tb else NN)

    def body(*refs):
        a_ref, b_ref, o_ref, acc_ref = refs[0], refs[1], refs[-2], refs[-1]
        k = pl.program_id(2)

        @pl.when(k == 0)
        def _():
            if extra is None:
                acc_ref[...] = jnp.zeros_like(acc_ref)
            else:
                acc_ref[...] = _dg(refs[2][...], refs[3][...], dims)

        acc_ref[...] += _dg(a_ref[...], b_ref[...], dims)

        @pl.when(k == nk - 1)
        def _():
            o_ref[...] = acc_ref[...].astype(o_ref.dtype)

    def b_index(n, m, k):
        i0, i1 = (n, k) if tb else (k, n)
        if b_chip is None:
            return (i0, i1)
        if b_chip == 0:
            nb = b.shape[1] // b_block[0]
            return (i0 // nb, i0 % nb, i1)
        nb = b.shape[2] // b_block[1]
        return (i1 // nb, i0, i1 % nb)

    def o_index(n, m, k):
        if out_chip is None:
            return (m, n)
        if out_chip == 0:
            mb = m_unit // tm
            return (m // mb, m % mb, n)
        nb = n_unit // tn
        return (n // nb, m, n % nb)

    b_block = (tn, tk) if tb else (tk, tn)
    lead = lambda blk, on: ((None,) + blk) if on else blk
    if a_split:
        kb = a.shape[2] // tk
        a_spec = pl.BlockSpec((None, tm, tk), lambda n, m, k: (k // kb, m, k % kb))
    elif ta:
        a_spec = pl.BlockSpec((tk, tm), lambda n, m, k: (k, m))
    else:
        a_spec = pl.BlockSpec((tm, tk), lambda n, m, k: (m, k))
    in_specs = [a_spec, pl.BlockSpec(lead(b_block, b_chip is not None), b_index)]
    args = [a, b]
    if extra is not None:
        assert not ta
        a2, b2 = extra
        E = a2.shape[1]
        in_specs += [pl.BlockSpec((tm, E), lambda n, m, k: (m, 0)),
                     pl.BlockSpec((tn, E), lambda n, m, k: (n, 0)) if tb else pl.BlockSpec((E, tn), lambda n, m, k: (0, n))]
        args += [a2, b2]
    out_full = (M, N) if out_chip is None else ((4, M // 4, N) if out_chip == 0 else (4, M, N // 4))
    return _pcall(
        body, name=name, grid=(N // tn, M // tm, nk), in_specs=in_specs,
        out_specs=pl.BlockSpec(lead((tm, tn), out_chip is not None), o_index),
        out_shape=jax.ShapeDtypeStruct(out_full, out_dtype), scratch_shapes=[pltpu.VMEM((tm, tn), F32)],
        sem=("parallel", "parallel", "arbitrary"), ride=ride,
    )(*args)


def _seg_spec(D, first_lat):
    return pl.BlockSpec((1, 1, D), lambda i: (jnp.where(i >= first_lat, 1, 0), 0, 0))


def _norm_fwd(x, y, gate, g, shift, scale, *, M, name):
    T, D = x.shape
    tm = _pick(T, (256,))
    first_lat = M // tm
    has_res = y is not None
    seg = _seg_spec(D, first_lat)
    row = pl.BlockSpec((tm, D), lambda i: (i, 0))

    def body(*refs):
        if has_res:
            x_ref, y_ref, gate_ref, g_ref, sh_ref, sc_ref, xo_ref, h_ref = refs
            xv = x_ref[...] + gate_ref[0] * y_ref[...].astype(F32)
            xo_ref[...] = xv
        else:
            x_ref, g_ref, sh_ref, sc_ref, h_ref = refs
            xv = x_ref[...]
        rstd = lax.rsqrt(jnp.mean(xv * xv, axis=-1, keepdims=True) + EPS)
        h = xv * rstd * g_ref[...] * (1.0 + sc_ref[0]) + sh_ref[0]
        h_ref[...] = h.astype(h_ref.dtype)

    gspec = pl.BlockSpec((1, D), lambda i: (0, 0))
    if has_res:
        ins = [x, y, gate, g, shift, scale]
        in_specs = [row, row, seg, gspec, seg, seg]
        out_shape = (jax.ShapeDtypeStruct((T, D), F32), jax.ShapeDtypeStruct((T, D), CDT))
        out_specs = (row, row)
    else:
        ins = [x, g, shift, scale]
        in_specs = [row, gspec, seg, seg]
        out_shape = jax.ShapeDtypeStruct((T, D), CDT)
        out_specs = row
    out = pl.pallas_call(
        body, name=name, grid=(T // tm,), in_specs=in_specs, out_specs=out_specs, out_shape=out_shape,
        compiler_params=_cparams(("parallel",)),
    )(*ins)
    return out if has_res else (x, out)


def _norm_bwd(x, dh, dx_in, g, scale, y_prev, gate_prev, *, M, name):
    T, D = x.shape
    tm = _pick(T, (256,))
    first_lat = M // tm
    has_prev = y_prev is not None
    seg = _seg_spec(D, first_lat)
    row = pl.BlockSpec((tm, D), lambda i: (i, 0))
    gspec = pl.BlockSpec((1, D), lambda i: (0, 0))

    def body(*refs):
        if has_prev:
            x_ref, dh_ref, dxi_ref, g_ref, sc_ref, yp_ref, gp_ref, dx_ref, dy_ref, acc_ref = refs
        else:
            x_ref, dh_ref, dxi_ref, g_ref, sc_ref, dx_ref, acc_ref = refs
        i = pl.program_id(0)

        @pl.when(jnp.logical_or(i == 0, i == first_lat))
        def _():
            acc_ref[...] = jnp.zeros_like(acc_ref)

        xv = x_ref[...]
        rstd = lax.rsqrt(jnp.mean(xv * xv, axis=-1, keepdims=True) + EPS)
        xn = xv * rstd
        dh = dh_ref[...].astype(F32)
        dxn = dh * (g_ref[...] * (1.0 + sc_ref[0]))
        dx = dxi_ref[...] + rstd * (dxn - xn * jnp.mean(dxn * xn, axis=-1, keepdims=True))
        dx_ref[...] = dx
        acc_ref[0, 0:1, :] += jnp.sum(dh, axis=0, keepdims=True)
        acc_ref[0, 1:2, :] += jnp.sum(dh * xn, axis=0, keepdims=True)
        if has_prev:
            dy_ref[...] = (dx * gp_ref[0]).astype(dy_ref.dtype)
            acc_ref[0, 2:3, :] += jnp.sum(dx * yp_ref[...].astype(F32), axis=0, keepdims=True)

    acc_spec = pl.BlockSpec((1, 8, D), lambda i: (jnp.where(i >= first_lat, 1, 0), 0, 0))
    acc_shape = jax.ShapeDtypeStruct((2, 8, D), F32)
    if has_prev:
        ins = [x, dh, dx_in, g, scale, y_prev, gate_prev]
        in_specs = [row, row, row, gspec, seg, row, seg]
        out_shape = (jax.ShapeDtypeStruct((T, D), F32), jax.ShapeDtypeStruct((T, D), CDT), acc_shape)
        out_specs = (row, row, acc_spec)
    else:
        ins = [x, dh, dx_in, g, scale]
        in_specs = [row, row, row, gspec, seg]
        out_shape = (jax.ShapeDtypeStruct((T, D), F32), acc_shape)
        out_specs = (row, acc_spec)
    out = pl.pallas_call(
        body, name=name, grid=(T // tm,), in_specs=in_specs, out_specs=out_specs, out_shape=out_shape,
        compiler_params=_cparams(("arbitrary",)),
    )(*ins)
    if has_prev:
        return out
    return out[0], None, out[1]


def _final_loss(x, y_prev, gate_prev, tgt, g, *, M, name):
    T, D = x.shape
    tm = _pick(T, (256,))
    first_lat = M // tm
    nt = T // tm
    seg = _seg_spec(D, first_lat)
    row = pl.BlockSpec((tm, D), lambda i: (i, 0))
    gspec = pl.BlockSpec((1, D), lambda i: (0, 0))
    tspec = pl.BlockSpec((tm, D), lambda i: (jnp.maximum(i - first_lat, 0), 0))

    def body(x_ref, yp_ref, gp_ref, t_ref, g_ref, loss_ref, dx_ref, dy_ref, acc_ref):
        i = pl.program_id(0)

        @pl.when(jnp.logical_or(i == 0, i == first_lat))
        def _():
            acc_ref[...] = jnp.zeros_like(acc_ref)

        lat = jnp.where(i >= first_lat, 1.0, 0.0)
        yp = yp_ref[...].astype(F32)
        xv = x_ref[...] + gp_ref[0] * yp
        rstd = lax.rsqrt(jnp.mean(xv * xv, axis=-1, keepdims=True) + EPS)
        xn = xv * rstd
        diff = (xn * g_ref[...] - t_ref[...]) * lat
        part = 0.5 * jnp.sum(jnp.sum(diff * diff, axis=-1, keepdims=True), axis=0, keepdims=True) * (1.0 / D)
        loss_ref[0] = jnp.broadcast_to(part, (8, 128))
        dyv = diff * (1.0 / D)
        dxn = dyv * g_ref[...]
        dx = rstd * (dxn - xn * jnp.mean(dxn * xn, axis=-1, keepdims=True))
        dx_ref[...] = dx
        dy_ref[...] = (dx * gp_ref[0]).astype(dy_ref.dtype)
        acc_ref[0, 0:1, :] += jnp.sum(dyv * xn, axis=0, keepdims=True)
        acc_ref[0, 2:3, :] += jnp.sum(dx * yp, axis=0, keepdims=True)

    return pl.pallas_call(
        body, name=name, grid=(nt,),
        in_specs=[row, row, seg, tspec, gspec],
        out_specs=(pl.BlockSpec((1, 8, 128), lambda i: (i, 0, 0)), row, row,
                   pl.BlockSpec((1, 8, D), lambda i: (jnp.where(i >= first_lat, 1, 0), 0, 0))),
        out_shape=(jax.ShapeDtypeStruct((nt, 8, 128), F32), jax.ShapeDtypeStruct((T, D), F32),
                   jax.ShapeDtypeStruct((T, D), CDT), jax.ShapeDtypeStruct((2, 8, D), F32)),
        compiler_params=_cparams(("arbitrary",)),
    )(x, y_prev, gate_prev, tgt, g)


HALO = 16
CONV_TC = (1408, 512)


def _taps(uc, prev16, next16, keep_prev, keep_next):
    tm = uc.shape[0]
    u = uc.astype(F32)
    rows = lax.broadcasted_iota(jnp.int32, u.shape, 0)
    pr = prev16[HALO - 1:HALO, :].astype(F32) * keep_prev
    nx = next16[0:1, :].astype(F32) * keep_next
    um = jnp.where(rows == 0, pr, pltpu.roll(u, 1, 0))
    up = jnp.where(rows == tm - 1, nx, pltpu.roll(u, tm - 1, 0))
    return um, u, up


def _conv3(uc, prev16, next16, w, bias, keep_prev, keep_next):
    um, u, up = _taps(uc, prev16, next16, keep_prev, keep_next)
    out = w[0:1, :] * um + w[1:2, :] * u + w[2:3, :] * up
    return out if bias is None else out + bias


def _conv_specs(tm, tc, T, col):
    hb = tm // HALO
    last = T // HALO - 1
    return [
        pl.BlockSpec((tm, tc), lambda j, i: (i, col(j))),
        pl.BlockSpec((HALO, tc), lambda j, i: (jnp.maximum(i * hb - 1, 0), col(j))),
        pl.BlockSpec((HALO, tc), lambda j, i: (jnp.minimum((i + 1) * hb, last), col(j))),
    ]


def _seg_keep(i, first_lat, nt):
    keep_prev = jnp.where(jnp.logical_or(i == 0, i == first_lat), 0.0, 1.0)
    keep_next = jnp.where(jnp.logical_or(i == first_lat - 1, i == nt - 1), 0.0, 1.0)
    return keep_prev, keep_next


def _conv_gate_fwd(u, cw, cb, *, M, name, ride=None):
    T, F2 = u.shape
    Fh = F2 // 2
    tm = _pick(T, (256,))
    tc = _pick(Fh, CONV_TC)
    nf = Fh // tc
    nt = T // tm
    first_lat = M // tm

    def body(ug, ugp, ugn, uv, uvp, uvn, wg, wv, bg, bv, o_ref):
        kp, kn = _seg_keep(pl.program_id(1), first_lat, nt)
        gc = _conv3(ug[...], ugp[...], ugn[...], wg[...], bg[...], kp, kn)
        vc = _conv3(uv[...], uvp[...], uvn[...], wv[...], bv[...], kp, kn)
        o_ref[...] = (_silu(gc) * vc).astype(o_ref.dtype)

    wspec = lambda off: pl.BlockSpec((3, tc), lambda j, i: (0, j + off))
    bspec = lambda off: pl.BlockSpec((1, tc), lambda j, i: (0, j + off))
    return _pcall(
        body, name=name, grid=(nf, nt),
        in_specs=_conv_specs(tm, tc, T, lambda j: j) + _conv_specs(tm, tc, T, lambda j: j + nf)
        + [wspec(0), wspec(nf), bspec(0), bspec(nf)],
        out_specs=pl.BlockSpec((tm, tc), lambda j, i: (i, j)),
        out_shape=jax.ShapeDtypeStruct((T, Fh), CDT), sem=("parallel", "parallel"), ride=ride,
    )(u, u, u, u, u, u, cw, cw, cb, cb)


def _conv_gate_bwd(u, dact, cw, cb, *, M, name):
    T, F2 = u.shape
    Fh = F2 // 2
    tm = _pick(T, (256,))
    tc = _pick(Fh, (512,))
    nf = Fh // tc
    nt = T // tm
    first_lat = M // tm

    def body(ug, ugp, ugn, uv, uvp, uvn, da, wg, wv, bg, bv, d_ref, acc_ref):
        i = pl.program_id(1)

        @pl.when(i == 0)
        def _():
            acc_ref[...] = jnp.zeros_like(acc_ref)

        kp, kn = _seg_keep(i, first_lat, nt)
        tg = _taps(ug[...], ugp[...], ugn[...], kp, kn)
        tv = _taps(uv[...], uvp[...], uvn[...], kp, kn)
        w = wg[...]
        gc = w[0:1, :] * tg[0] + w[1:2, :] * tg[1] + w[2:3, :] * tg[2] + bg[...]
        w = wv[...]
        vc = w[0:1, :] * tv[0] + w[1:2, :] * tv[1] + w[2:3, :] * tv[2] + bv[...]
        dav = da[...].astype(F32)
        act, dact_dg = _silu_pair(gc)
        for half, d, taps in ((0, dav * vc * dact_dg, tg), (1, dav * act, tv)):
            d_ref[half] = d.astype(d_ref.dtype)
            acc_ref[half, 0:1, :] += jnp.sum(d * taps[0], axis=0, keepdims=True)
            acc_ref[half, 1:2, :] += jnp.sum(d * taps[1], axis=0, keepdims=True)
            acc_ref[half, 2:3, :] += jnp.sum(d * taps[2], axis=0, keepdims=True)
            acc_ref[half, 3:4, :] += jnp.sum(d, axis=0, keepdims=True)

    wspec = lambda off: pl.BlockSpec((3, tc), lambda j, i: (0, j + off))
    bspec = lambda off: pl.BlockSpec((1, tc), lambda j, i: (0, j + off))
    return pl.pallas_call(
        body, name=name, grid=(nf, nt),
        in_specs=_conv_specs(tm, tc, T, lambda j: j) + _conv_specs(tm, tc, T, lambda j: j + nf)
        + [pl.BlockSpec((tm, tc), lambda j, i: (i, j)), wspec(0), wspec(nf), bspec(0), bspec(nf)],
        out_specs=(pl.BlockSpec((2, tm, tc), lambda j, i: (0, i, j)), pl.BlockSpec((2, 8, tc), lambda j, i: (0, 0, j))),
        out_shape=(jax.ShapeDtypeStruct((2, T, Fh), CDT), jax.ShapeDtypeStruct((2, 8, Fh), F32)),
        compiler_params=_cparams(("parallel", "arbitrary")),
    )(u, u, u, u, u, u, dact, cw, cw, cb, cb)


def _conv_t(d, cw, *, M, name):
    _, T, Fh = d.shape
    tm = _pick(T, (256,))
    tc = _pick(Fh, CONV_TC)
    nf = Fh // tc
    nt = T // tm
    first_lat = M // tm
    hb = tm // HALO
    last = T // HALO - 1

    def body(dc, dp, dn, w, o_ref):
        kp, kn = _seg_keep(pl.program_id(2), first_lat, nt)
        dm, d0, dp1 = _taps(dc[...], dp[...], dn[...], kp, kn)
        wv = w[...]
        o_ref[...] = (wv[2:3, :] * dm + wv[1:2, :] * d0 + wv[0:1, :] * dp1).astype(o_ref.dtype)

    return pl.pallas_call(
        body, name=name, grid=(2, nf, nt),
        in_specs=[pl.BlockSpec((None, tm, tc), lambda g, j, i: (g, i, j)),
                  pl.BlockSpec((None, HALO, tc), lambda g, j, i: (g, jnp.maximum(i * hb - 1, 0), j)),
                  pl.BlockSpec((None, HALO, tc), lambda g, j, i: (g, jnp.minimum((i + 1) * hb, last), j)),
                  pl.BlockSpec((3, tc), lambda g, j, i: (0, g * nf + j))],
        out_specs=pl.BlockSpec((None, tm, tc), lambda g, j, i: (g, i, j)),
        out_shape=jax.ShapeDtypeStruct((2, T, Fh), CDT),
        compiler_params=_cparams(("parallel", "parallel", "parallel")),
    )(d, d, d, cw)


def _rope_tables(N, M, HD):
    ax = HD // 2
    pos = jnp.arange(N, dtype=jnp.int32)
    row = (pos // GRID_W).astype(F32)
    col = (pos % GRID_W).astype(F32)
    inv = ROPE_BASE ** (-jnp.arange(0, ax, 2, dtype=F32) / ax)
    ar = row[:, None] * inv[None, :]
    ac = col[:, None] * inv[None, :]
    cos = jnp.concatenate([jnp.cos(ar), jnp.cos(ar), jnp.cos(ac), jnp.cos(ac)], axis=1)
    sin = jnp.concatenate([-jnp.sin(ar), jnp.sin(ar), -jnp.sin(ac), jnp.sin(ac)], axis=1)
    cos = jnp.concatenate([jnp.ones((M, HD), F32), cos], axis=0)
    sin = jnp.concatenate([jnp.zeros((M, HD), F32), sin], axis=0)
    return cos, sin


def _pair_swap(x, nf):
    w = x.shape[1]
    lane = lax.broadcasted_iota(jnp.int32, x.shape, 1)
    first = (lane % (2 * nf)) < nf
    return jnp.where(first, pltpu.roll(x, w - nf, 1), pltpu.roll(x, nf, 1))


def _rope_fwd(qkv, cos, sin, *, QW, KW, HD, name):
    T = qkv.shape[0]
    tm = _pick(T, (256,))
    nf = HD // 4

    def body(qkv_ref, c_ref, s_ref, q_ref, k_ref, v_ref):
        c = c_ref[...]
        s = s_ref[...]
        for ref, off, w in ((q_ref, 0, QW), (k_ref, QW, KW)):
            xv = qkv_ref[:, off:off + w]
            ct = jnp.tile(c, (1, w // HD))
            st = jnp.tile(s, (1, w // HD))
            ref[...] = (xv * ct + _pair_swap(xv, nf) * st).astype(ref.dtype)
        v_ref[...] = qkv_ref[:, QW + KW:QW + 2 * KW].astype(v_ref.dtype)

    tspec = pl.BlockSpec((tm, HD), lambda i: (i, 0))
    return pl.pallas_call(
        body, name=name, grid=(T // tm,),
        in_specs=[pl.BlockSpec((tm, QW + 2 * KW), lambda i: (i, 0)), tspec, tspec],
        out_specs=(pl.BlockSpec((tm, QW), lambda i: (i, 0)), pl.BlockSpec((tm, KW), lambda i: (i, 0)),
                   pl.BlockSpec((tm, KW), lambda i: (i, 0))),
        out_shape=(jax.ShapeDtypeStruct((T, QW), CDT), jax.ShapeDtypeStruct((T, KW), CDT),
                   jax.ShapeDtypeStruct((T, KW), CDT)),
        compiler_params=_cparams(("parallel",)),
    )(qkv, cos, sin)


def _rope_bwd(dq, dk, dv, cos, sin, *, HD, name):
    T, QW = dq.shape
    KW = dk.shape[1]
    tm = _pick(T, (256,))
    nf = HD // 4

    def body(dq_ref, dk_ref, dv_ref, c_ref, s_ref, o_ref):
        c = c_ref[...]
        s = s_ref[...]
        for ref, off, w in ((dq_ref, 0, QW), (dk_ref, QW, KW)):
            g = ref[...].astype(F32)
            ct = jnp.tile(c, (1, w // HD))
            st = jnp.tile(s, (1, w // HD))
            o_ref[:, off:off + w] = (g * ct + _pair_swap(g * st, nf)).astype(o_ref.dtype)
        o_ref[:, QW + KW:QW + 2 * KW] = dv_ref[...].astype(o_ref.dtype)

    tspec = pl.BlockSpec((tm, HD), lambda i: (i, 0))
    return pl.pallas_call(
        body, name=name, grid=(T // tm,),
        in_specs=[pl.BlockSpec((tm, QW), lambda i: (i, 0)), pl.BlockSpec((tm, KW), lambda i: (i, 0)),
                  pl.BlockSpec((tm, KW), lambda i: (i, 0)), tspec, tspec],
        out_specs=pl.BlockSpec((tm, QW + 2 * KW), lambda i: (i, 0)),
        out_shape=jax.ShapeDtypeStruct((T, QW + 2 * KW), CDT),
        compiler_params=_cparams(("parallel",)),
    )(dq, dk, dv, cos, sin)


def _attn_scores(q_ref, kc_ref, kp_ref, kn_ref, kx_ref, sink_ref, i, *, M, HD, nblk, nbc):
    qs = jnp.concatenate([q_ref[:, g * HD:(g + 1) * HD] for g in range(ATT_G)], axis=0)
    kall = jnp.concatenate([kc_ref[...], kp_ref[...], kn_ref[...], kx_ref[...]], axis=0)
    s = _dg(qs, kall, NT) * (HD ** -0.5)
    shape = s.shape
    r = lax.broadcasted_iota(jnp.int32, shape, 0) % BLK
    c = lax.broadcasted_iota(jnp.int32, shape, 1) - M
    far = 4 * BLK
    lat_off = jnp.where(i >= nbc, 0, far)
    lo = jnp.maximum(r, jnp.where(i - 1 >= nbc, 0, BLK)) + lat_off
    hi = jnp.minimum(r + 2 * BLK, jnp.where(i + 1 < nblk, 3 * BLK - 1, 2 * BLK - 1))
    allowed = jnp.logical_or(c < 0, jnp.logical_and(c >= lo, c <= hi))
    s = jnp.where(allowed, s, -1e30)
    sink = sink_ref[0]
    m = jnp.maximum(jnp.max(s, axis=-1, keepdims=True), sink)
    e = jnp.exp(s - m)
    es = jnp.exp(sink - m)
    inv = 1.0 / (jnp.sum(e, axis=-1, keepdims=True) + es)
    return qs, kall, e * inv, es * inv


def _attn_specs(M, HD, nblk):
    kv_blk = lambda f: pl.BlockSpec((BLK, HD), lambda h, i: (f(i), h))
    ctx = pl.BlockSpec((M, HD), lambda h, i: (0, h))
    win = [kv_blk(lambda i: jnp.maximum(i - 1, 0)), kv_blk(lambda i: i), kv_blk(lambda i: jnp.minimum(i + 1, nblk - 1))]
    qspec = pl.BlockSpec((BLK, ATT_G * HD), lambda h, i: (i, h))
    sspec = pl.BlockSpec((1, ATT_G * BLK, 1), lambda h, i: (h, 0, 0))
    return qspec, [ctx] + win, sspec


def _attn_fwd(q, k, v, sink_col, *, M, name, ride=None):
    T, QW = q.shape
    HD = QW // ATT_HEADS
    nblk = T // BLK
    nbc = M // BLK

    def body(q_ref, kc, kp, kn, kx, vc, vp, vn, vx, sink_ref, o_ref):
        i = pl.program_id(1)
        _, _, p, _ = _attn_scores(q_ref, kc, kp, kn, kx, sink_ref, i, M=M, HD=HD, nblk=nblk, nbc=nbc)
        vall = jnp.concatenate([vc[...], vp[...], vn[...], vx[...]], axis=0)
        o = _dg(p.astype(CDT), vall, NN)
        for g in range(ATT_G):
            o_ref[:, g * HD:(g + 1) * HD] = o[g * BLK:(g + 1) * BLK, :].astype(o_ref.dtype)

    qspec, kvs, sspec = _attn_specs(M, HD, nblk)
    return _pcall(
        body, name=name, grid=(ATT_KV, nblk), in_specs=[qspec] + kvs + kvs + [sspec], out_specs=qspec,
        out_shape=jax.ShapeDtypeStruct((T, QW), CDT), sem=("parallel", "parallel"), ride=ride,
    )(q, k, k, k, k, v, v, v, v, sink_col)


def _attn_bwd(q, k, v, sink_col, do, *, M, name):
    T, QW = q.shape
    HD = QW // ATT_HEADS
    KW = ATT_KV * HD
    nblk = T // BLK
    nbc = M // BLK

    def body(q_ref, kc, kp, kn, kx, vc, vp, vn, vx, sink_ref, do_ref, dq_ref, dkc_ref, dvc_ref, dkw_ref, dvw_ref, ds_ref):
        i = pl.program_id(1)

        @pl.when(i == 0)
        def _():
            dkc_ref[...] = jnp.zeros_like(dkc_ref)
            dvc_ref[...] = jnp.zeros_like(dvc_ref)
            ds_ref[...] = jnp.zeros_like(ds_ref)

        qs, kall, p, p_sink = _attn_scores(q_ref, kc, kp, kn, kx, sink_ref, i, M=M, HD=HD, nblk=nblk, nbc=nbc)
        vall = jnp.concatenate([vc[...], vp[...], vn[...], vx[...]], axis=0)
        dos = jnp.concatenate([do_ref[:, g * HD:(g + 1) * HD] for g in range(ATT_G)], axis=0)
        dp = _dg(dos, vall, NT)
        dsum = jnp.sum(p * dp, axis=-1, keepdims=True)
        dsc = (p * (dp - dsum) * (HD ** -0.5)).astype(CDT)
        dq = _dg(dsc, kall, NN)
        dkall = _dg(dsc, qs, TN)
        dvall = _dg(p.astype(CDT), dos, TN)
        for g in range(ATT_G):
            dq_ref[:, g * HD:(g + 1) * HD] = dq[g * BLK:(g + 1) * BLK, :].astype(dq_ref.dtype)
        dkc_ref[...] += dkall[0:M]
        dvc_ref[...] += dvall[0:M]
        dkw_ref[...] = dkall[M:]
        dvw_ref[...] = dvall[M:]
        ds_ref[0] += -(p_sink * dsum)

    qspec, kvs, sspec = _attn_specs(M, HD, nblk)
    ctx_out = pl.BlockSpec((M, HD), lambda h, i: (0, h))
    win_out = pl.BlockSpec((3 * BLK, HD), lambda h, i: (i, h))
    return pl.pallas_call(
        body, name=name, grid=(ATT_KV, nblk),
        in_specs=[qspec] + kvs + kvs + [sspec, qspec],
        out_specs=(qspec, ctx_out, ctx_out, win_out, win_out, sspec),
        out_shape=(jax.ShapeDtypeStruct((T, QW), CDT), jax.ShapeDtypeStruct((M, KW), F32), jax.ShapeDtypeStruct((M, KW), F32),
                   jax.ShapeDtypeStruct((nblk * 3 * BLK, KW), F32), jax.ShapeDtypeStruct((nblk * 3 * BLK, KW), F32),
                   jax.ShapeDtypeStruct((ATT_KV, ATT_G * BLK, 1), F32)),
        compiler_params=_cparams(("parallel", "arbitrary")),
    )(q, k, k, k, k, v, v, v, v, sink_col, do)


def _window_combine(part, *, nbc, name):
    rows, KW = part.shape
    nblk = rows // (3 * BLK)
    nbl = nblk - nbc

    def body(a_ref, b_ref, c_ref, o_ref):
        j = pl.program_id(0)
        o_ref[...] = (a_ref[...] * jnp.where(j + 1 < nbl, 1.0, 0.0) + b_ref[...]
                      + c_ref[...] * jnp.where(j >= 1, 1.0, 0.0))

    return pl.pallas_call(
        body, name=name, grid=(nbl,),
        in_specs=[pl.BlockSpec((BLK, KW), lambda j: (3 * jnp.minimum(nbc + j + 1, nblk - 1), 0)),
                  pl.BlockSpec((BLK, KW), lambda j: (3 * (nbc + j) + 1, 0)),
                  pl.BlockSpec((BLK, KW), lambda j: (3 * jnp.maximum(nbc + j - 1, 0) + 2, 0))],
        out_specs=pl.BlockSpec((BLK, KW), lambda j: (j, 0)),
        out_shape=jax.ShapeDtypeStruct((nbl * BLK, KW), F32),
        compiler_params=_cparams(("parallel",)),
    )(part, part, part)


def _split3(x):
    hi = x.astype(CDT)
    r1 = x - hi.astype(F32)
    mid = r1.astype(CDT)
    lo = (r1 - mid.astype(F32)).astype(CDT)
    return hi, mid, lo


def _tri_sum(tri, x, terms):
    parts = _split3(x)[:terms]
    out = _dg(tri, parts[0], NN)
    for p in parts[1:]:
        out = out + _dg(tri, p, NN)
    return out


def _gla_dims(D):
    dk = D // 2 // GLA_H
    dv = D // GLA_H
    return dk, dv


def _chunk_of(s, rev, ncc, ns):
    if not rev:
        return s
    return jnp.where(s < ncc, ncc - 1 - s, ns - 1 - (s - ncc))


def _gla_chunk(q, k, g, rev, dk):
    C = q.shape[0]
    r = lax.broadcasted_iota(jnp.int32, (C, C), 0)
    c = lax.broadcasted_iota(jnp.int32, (C, C), 1)
    causal = (r <= c) if rev else (r >= c)
    b = _tri_sum(causal.astype(CDT), g, 3)
    B = b[0:1, :] if rev else b[C - 1:C, :]
    q = q.astype(F32) * (dk ** -0.5)
    k = k.astype(F32)
    return causal, b, B, q * jnp.exp(b), k * jnp.exp(-b), k * jnp.exp(B - b)


def _gla_scan_fwd(proj, g, *, rev, M, D, name):
    T = proj.shape[0]
    dk, dv = _gla_dims(D)
    C, H, Dh = CHUNK, GLA_H, D // 2
    ns = T // C
    ncc = M // C
    cm = lambda s: _chunk_of(s, rev, ncc, ns)

    def body(q_ref, k_ref, v_ref, g_ref, o_ref, st_ref, S):
        @pl.when(pl.program_id(0) == 0)
        def _():
            S[...] = jnp.zeros_like(S)

        for h in range(H):
            ks, vs = slice(h * dk, (h + 1) * dk), slice(h * dv, (h + 1) * dv)
            causal, b, B, qt, kt, kh = _gla_chunk(q_ref[:, ks], k_ref[:, ks], g_ref[:, ks], rev, dk)
            v = v_ref[:, vs]
            A = jnp.where(causal, _dg(qt.astype(CDT), kt.astype(CDT), NT), 0.0)
            Sin = S[h]
            st_ref[h] = Sin
            o_ref[:, vs] = _dg(A.astype(CDT), v, NN) + _dg(qt.astype(CDT), Sin.astype(CDT), NT)
            S[h] = Sin * jnp.exp(B) + _dg(v, kh.astype(CDT), TN)

    return pl.pallas_call(
        body, name=name, grid=(ns,),
        in_specs=[pl.BlockSpec((C, Dh), lambda s: (cm(s), 0)), pl.BlockSpec((C, Dh), lambda s: (cm(s), 1)),
                  pl.BlockSpec((C, D), lambda s: (cm(s), 1)), pl.BlockSpec((C, Dh), lambda s: (cm(s), 0))],
        out_specs=(pl.BlockSpec((C, D), lambda s: (cm(s), 0)), pl.BlockSpec((H, dv, dk), lambda s: (s, 0, 0))),
        out_shape=(jax.ShapeDtypeStruct((T, D), F32), jax.ShapeDtypeStruct((ns * H, dv, dk), F32)),
        scratch_shapes=[pltpu.VMEM((H, dv, dk), F32)],
        compiler_params=_cparams(("arbitrary",)),
    )(proj, proj, proj, g)


def _gla_scan_bwd(proj, g, st, do, *, rev, M, D, name):
    T = proj.shape[0]
    dk, dv = _gla_dims(D)
    C, H, Dh = CHUNK, GLA_H, D // 2
    ns = T // C
    ncc = M // C
    cm = lambda j: _chunk_of(ns - 1 - j, rev, ncc, ns)

    def body(q_ref, k_ref, v_ref, g_ref, st_ref, do_ref, dq_ref, dk_ref, dv_ref, dg_ref, dS):
        @pl.when(pl.program_id(0) == 0)
        def _():
            dS[...] = jnp.zeros_like(dS)

        rows = lax.broadcasted_iota(jnp.int32, (C, dk), 0)
        eye = lax.broadcasted_iota(jnp.int32, (C, C), 0) == lax.broadcasted_iota(jnp.int32, (C, C), 1)
        for h in range(H):
            ks, vs = slice(h * dk, (h + 1) * dk), slice(h * dv, (h + 1) * dv)
            causal, b, B, qt, kt, kh = _gla_chunk(q_ref[:, ks], k_ref[:, ks], g_ref[:, ks], rev, dk)
            v = v_ref[:, vs]
            dov = do_ref[:, vs]
            ST = st_ref[h]
            dSo = dS[h]
            qtb, ktb, khb = qt.astype(CDT), kt.astype(CDT), kh.astype(CDT)
            dSb = dSo.astype(CDT)
            A = jnp.where(causal, _dg(qtb, ktb, NT), 0.0).astype(CDT)
            dA = jnp.where(causal, _dg(dov, v, NT), 0.0).astype(CDT)
            dqt = _dg(dA, ktb, NN) + _dg(dov, ST.astype(CDT), NN)
            dkt = _dg(dA, qtb, TN)
            dvv = _dg(A, dov, TN) + _dg(khb, dSb, NT)
            dkh = _dg(v, dSb, NN)
            eB = jnp.exp(B)
            dB = eB * jnp.sum(ST * dSo, axis=0, keepdims=True) + jnp.sum(dkh * kh, axis=0, keepdims=True)
            db = dqt * qt - dkt * kt - dkh * kh + jnp.where(rows == (0 if rev else C - 1), dB, 0.0)
            anti = jnp.logical_not(causal) | eye
            dg_ref[:, ks] = _tri_sum(anti.astype(CDT), db, 2)
            dq_ref[:, ks] = dqt * jnp.exp(b) * (dk ** -0.5)
            dk_ref[:, ks] = dkt * jnp.exp(-b) + dkh * jnp.exp(B - b)
            dv_ref[:, vs] = dvv
            dS[h] = dSo * eB + _dg(dov, qtb, TN)

    half = pl.BlockSpec((C, Dh), lambda j: (cm(j), 0))
    full = pl.BlockSpec((C, D), lambda j: (cm(j), 0))
    return pl.pallas_call(
        body, name=name, grid=(ns,),
        in_specs=[half, pl.BlockSpec((C, Dh), lambda j: (cm(j), 1)), pl.BlockSpec((C, D), lambda j: (cm(j), 1)), half,
                  pl.BlockSpec((H, dv, dk), lambda j: (ns - 1 - j, 0, 0)), full],
        out_specs=(half, half, full, half),
        out_shape=(jax.ShapeDtypeStruct((T, Dh), F32), jax.ShapeDtypeStruct((T, Dh), F32),
                   jax.ShapeDtypeStruct((T, D), F32), jax.ShapeDtypeStruct((T, Dh), F32)),
        scratch_shapes=[pltpu.VMEM((H, dv, dk), F32)],
        compiler_params=_cparams(("arbitrary",)),
    )(proj, proj, proj, g, st, do)


def _log_sigmoid_parts(z):
    t = jnp.exp(-jnp.abs(z))
    return jnp.minimum(z, 0.0) - jnp.log(1.0 + t), jnp.where(z >= 0, t / (1.0 + t), 1.0 / (1.0 + t))


def _gla_gate_fwd(lr, w2, bias, *, D, name):
    T = lr.shape[0]
    tm = _pick(T, (256,))
    Dh = D // 2

    def body(lr_ref, w_ref, b_ref, gf_ref, gb_ref):
        z = _dg(lr_ref[...], w_ref[...], NN) + b_ref[...]
        g, _ = _log_sigmoid_parts(z)
        g = g * (1.0 / GATE_NORM)
        gf_ref[...] = g[:, 0:Dh]
        gb_ref[...] = g[:, Dh:D]

    half = pl.BlockSpec((tm, Dh), lambda i: (i, 0))
    return pl.pallas_call(
        body, name=name, grid=(T // tm,),
        in_specs=[pl.BlockSpec((tm, 128), lambda i: (i, 0)), pl.BlockSpec((128, D), lambda i: (0, 0)),
                  pl.BlockSpec((1, D), lambda i: (0, 0))],
        out_specs=(half, half),
        out_shape=(jax.ShapeDtypeStruct((T, Dh), F32), jax.ShapeDtypeStruct((T, Dh), F32)),
        compiler_params=_cparams(("parallel",)),
    )(lr, w2, bias)


def _gla_proj_bwd(lr, w2, bias, dqf, dkf, dvf, dgf, dqb, dkb, dvb, dgb, dr, *, D, name):
    T = lr.shape[0]
    tm = _pick(T, (256,))
    Dh = D // 2

    def body(lr_ref, w_ref, b_ref, dqf_r, dkf_r, dvf_r, dgf_r, dqb_r, dkb_r, dvb_r, dgb_r, dr_ref, dp_ref, dl_ref, dw_ref, db_ref):
        @pl.when(pl.program_id(0) == 0)
        def _():
            dw_ref[...] = jnp.zeros_like(dw_ref)
            db_ref[...] = jnp.zeros_like(db_ref)

        lr = lr_ref[...]
        z = _dg(lr, w_ref[...], NN) + b_ref[...]
        _, sneg = _log_sigmoid_parts(z)
        dz = jnp.concatenate([dgf_r[...], dgb_r[...]], axis=1) * sneg * (1.0 / GATE_NORM)
        dzb = dz.astype(CDT)
        dp_ref[:, 0:Dh] = (dqf_r[...] + dqb_r[...]).astype(dp_ref.dtype)
        dp_ref[:, Dh:D] = (dkf_r[...] + dkb_r[...]).astype(dp_ref.dtype)
        dp_ref[:, D:2 * D] = (dvf_r[...] + dvb_r[...]).astype(dp_ref.dtype)
        dp_ref[:, 2 * D:3 * D] = dr_ref[...]
        dl_ref[...] = _dg(dzb, w_ref[...], NT).astype(dl_ref.dtype)
        dw_ref[...] += _dg(lr, dzb, TN)
        db_ref[0:1, :] += jnp.sum(dz, axis=0, keepdims=True)

    half = pl.BlockSpec((tm, Dh), lambda i: (i, 0))
    full = pl.BlockSpec((tm, D), lambda i: (i, 0))
    return pl.pallas_call(
        body, name=name, grid=(T // tm,),
        in_specs=[pl.BlockSpec((tm, 128), lambda i: (i, 0)), pl.BlockSpec((128, D), lambda i: (0, 0)),
                  pl.BlockSpec((1, D), lambda i: (0, 0)), half, half, full, half, half, half, full, half, full],
        out_specs=(pl.BlockSpec((tm, 3 * D), lambda i: (i, 0)), pl.BlockSpec((tm, 128), lambda i: (i, 0)),
                   pl.BlockSpec((128, D), lambda i: (0, 0)), pl.BlockSpec((8, D), lambda i: (0, 0))),
        out_shape=(jax.ShapeDtypeStruct((T, 3 * D), CDT), jax.ShapeDtypeStruct((T, 128), CDT),
                   jax.ShapeDtypeStruct((128, D), F32), jax.ShapeDtypeStruct((8, D), F32)),
        compiler_params=_cparams(("arbitrary",)),
    )(lr, w2, bias, dqf, dkf, dvf, dgf, dqb, dkb, dvb, dgb, dr)


def _gla_out_fwd(of, ob, proj, gn, *, D, name):
    T = of.shape[0]
    tm = _pick(T, (256,))
    dv = D // GLA_H

    def body(of_ref, ob_ref, r_ref, g_ref, y_ref):
        for h in range(GLA_H):
            sl = slice(h * dv, (h + 1) * dv)
            o = of_ref[:, sl] + ob_ref[:, sl]
            rstd = lax.rsqrt(jnp.mean(o * o, axis=-1, keepdims=True) + EPS)
            y_ref[:, sl] = (o * rstd * g_ref[...] * _silu(r_ref[:, sl].astype(F32))).astype(y_ref.dtype)

    full = pl.BlockSpec((tm, D), lambda i: (i, 0))
    return pl.pallas_call(
        body, name=name, grid=(T // tm,),
        in_specs=[full, full, pl.BlockSpec((tm, D), lambda i: (i, 2)), pl.BlockSpec((1, dv), lambda i: (0, 0))],
        out_specs=full, out_shape=jax.ShapeDtypeStruct((T, D), CDT),
        compiler_params=_cparams(("parallel",)),
    )(of, ob, proj, gn)


def _gla_out_bwd(of, ob, proj, gn, dy, *, D, name):
    T = of.shape[0]
    tm = _pick(T, (256,))
    dv = D // GLA_H

    def body(of_ref, ob_ref, r_ref, g_ref, dy_ref, do_ref, dr_ref, dg_ref):
        @pl.when(pl.program_id(0) == 0)
        def _():
            dg_ref[...] = jnp.zeros_like(dg_ref)

        gv = g_ref[...]
        for h in range(GLA_H):
            sl = slice(h * dv, (h + 1) * dv)
            o = of_ref[:, sl] + ob_ref[:, sl]
            rstd = lax.rsqrt(jnp.mean(o * o, axis=-1, keepdims=True) + EPS)
            oh = o * rstd
            r = r_ref[:, sl].astype(F32)
            dyv = dy_ref[:, sl].astype(F32)
            act, dact_dr = _silu_pair(r)
            don = dyv * act
            dr_ref[:, sl] = (dyv * oh * gv * dact_dr).astype(dr_ref.dtype)
            dg_ref[0:1, :] += jnp.sum(don * oh, axis=0, keepdims=True)
            dn = don * gv
            do_ref[:, sl] = (rstd * (dn - oh * jnp.mean(dn * oh, axis=-1, keepdims=True))).astype(do_ref.dtype)

    full = pl.BlockSpec((tm, D), lambda i: (i, 0))
    return pl.pallas_call(
        body, name=name, grid=(T // tm,),
        in_specs=[full, full, pl.BlockSpec((tm, D), lambda i: (i, 2)), pl.BlockSpec((1, dv), lambda i: (0, 0)), full],
        out_specs=(full, full, pl.BlockSpec((8, dv), lambda i: (0, 0))),
        out_shape=(jax.ShapeDtypeStruct((T, D), CDT), jax.ShapeDtypeStruct((T, D), CDT), jax.ShapeDtypeStruct((8, dv), F32)),
        compiler_params=_cparams(("arbitrary",)),
    )(of, ob, proj, gn, dy)


def _adamw(w, g, m, v, *, name, ride=None):
    R, Cc = w.shape
    tr = R
    for cand in (512, 256, 128, 64, 32, 16, 8):
        if R % cand == 0 and cand * Cc * 4 <= 2 * 1024 * 1024:
            tr = cand
            break

    def body(w_ref, g_ref, m_ref, v_ref, d_ref, mo_ref, vo_ref):
        gv = g_ref[...]
        mn = B1 * m_ref[...] + (1.0 - B1) * gv
        vn = B2 * v_ref[...] + (1.0 - B2) * (gv * gv)
        mh = mn / (1.0 - B1 ** STEP)
        vh = vn / (1.0 - B2 ** STEP)
        d_ref[...] = -LR * (mh / (jnp.sqrt(vh) + AEPS) + WD * w_ref[...])
        mo_ref[...] = mn
        vo_ref[...] = vn

    spec = pl.BlockSpec((tr, Cc), lambda i: (i, 0))
    sh = jax.ShapeDtypeStruct((R, Cc), F32)
    return _pcall(
        body, name=name, grid=(R // tr,), in_specs=[spec] * 4, out_specs=(spec,) * 3, out_shape=(sh,) * 3,
        sem=("parallel",), ride=ride,
    )(w, g, m, v)


def _attn_layer_fwd(h, w, tabs, M, tag, rides):
    cos, sin = tabs
    QW = 4 * w["w_o"].shape[1]
    HD = QW // ATT_HEADS
    KW = ATT_KV * HD
    qkv = _mm(h, w["w_qkv"], b_chip=1, out_dtype=F32, ride=rides.get("mix"), name=f"{tag}_qkv")
    q, k, v = _rope_fwd(qkv, cos, sin, QW=QW, KW=KW, HD=HD, name=f"{tag}_rope")
    sink_col = jnp.repeat(w["sink"].astype(F32), BLK).reshape(ATT_KV, ATT_G * BLK, 1)
    o = _attn_fwd(q, k, v, sink_col, M=M, ride=rides.get("attn"), name=f"{tag}_attn")
    y = _mm(o, w["w_o"], b_chip=0, out_dtype=CDT, name=f"{tag}_wo")
    return y, dict(h=h, q=q, k=k, v=v, o=o, sink_col=sink_col)


def _attn_layer_bwd(dy, sv, w, tabs, M, tag):
    cos, sin = tabs
    QW = 4 * w["w_o"].shape[1]
    HD = QW // ATT_HEADS
    do = _mm(dy, w["w_o"], tb=True, b_chip=0, out_dtype=CDT, name=f"{tag}_dwo_x")
    g = {"w_o": _mm(sv["o"], dy, ta=True, out_chip=0, out_dtype=CDT, name=f"{tag}_dwo_w")}
    dq, dkc, dvc, dkw, dvw, dsink = _attn_bwd(sv["q"], sv["k"], sv["v"], sv["sink_col"], do, M=M, name=f"{tag}_attn_bwd")
    nbc = M // BLK
    dk = jnp.concatenate([dkc, _window_combine(dkw, nbc=nbc, name=f"{tag}_dk_comb")], axis=0)
    dv = jnp.concatenate([dvc, _window_combine(dvw, nbc=nbc, name=f"{tag}_dv_comb")], axis=0)
    dqkv = _rope_bwd(dq, dk, dv, cos, sin, HD=HD, name=f"{tag}_rope_bwd")
    dh = _mm(dqkv, w["w_qkv"], tb=True, b_chip=1, out_dtype=CDT, name=f"{tag}_dqkv_x")
    g["w_qkv"] = _mm(sv["h"], dqkv, ta=True, out_chip=1, out_dtype=CDT, name=f"{tag}_dqkv_w")
    g["sink"] = jnp.sum(dsink.reshape(ATT_HEADS, BLK), axis=1)
    return dh, g


def _gla_layer_fwd(h, w, M, tag, rides):
    D = h.shape[1]
    proj = _mm(h, w["w_in"], b_chip=1, out_dtype=CDT, ride=rides.get("mix"), name=f"{tag}_in")
    lr = _mm(h, w["w1x"], out_dtype=CDT, name=f"{tag}_in_gate")
    gf, gb = _gla_gate_fwd(lr, w["w2"], w["gbias"], D=D, name=f"{tag}_gate")
    of, stf = _gla_scan_fwd(proj, gf, rev=False, M=M, D=D, name=f"{tag}_scan_f")
    ob, stb = _gla_scan_fwd(proj, gb, rev=True, M=M, D=D, name=f"{tag}_scan_b")
    yg = _gla_out_fwd(of, ob, proj, w["onorm"], D=D, name=f"{tag}_out")
    y = _mm(yg, w["w_o"], b_chip=0, out_dtype=CDT, name=f"{tag}_wo")
    return y, dict(h=h, proj=proj, lr=lr, gf=gf, gb=gb, of=of, ob=ob, stf=stf, stb=stb, yg=yg)


def _gla_layer_bwd(dy, sv, w, M, tag):
    D = dy.shape[1]
    dyg = _mm(dy, w["w_o"], tb=True, b_chip=0, out_dtype=CDT, name=f"{tag}_dwo_x")
    g = {"w_o": _mm(sv["yg"], dy, ta=True, out_chip=0, out_dtype=CDT, name=f"{tag}_dwo_w")}
    do, dr, dgn = _gla_out_bwd(sv["of"], sv["ob"], sv["proj"], w["onorm"], dyg, D=D, name=f"{tag}_out_bwd")
    df = _gla_scan_bwd(sv["proj"], sv["gf"], sv["stf"], do, rev=False, M=M, D=D, name=f"{tag}_scan_f_bwd")
    db = _gla_scan_bwd(sv["proj"], sv["gb"], sv["stb"], do, rev=True, M=M, D=D, name=f"{tag}_scan_b_bwd")
    dproj, dlr, dw2, dbias = _gla_proj_bwd(sv["lr"], w["w2"], w["gbias"], *df, *db, dr, D=D, name=f"{tag}_proj_bwd")
    dh = _mm(dproj, w["w_in"], tb=True, b_chip=1, extra=(dlr, w["w1x"]), out_dtype=CDT, name=f"{tag}_din_x")
    g["w_in"] = _mm(sv["h"], dproj, ta=True, out_chip=1, out_dtype=CDT, name=f"{tag}_din_w")
    g["w1x"] = _mm(sv["h"], dlr, ta=True, out_dtype=F32, name=f"{tag}_din_gate_w")
    g["w2"] = dw2
    g["gbias"] = dbias[0]
    g["onorm"] = dgn[0]
    return dh, g


def _ffn_fwd(h2, w, M, tag, rides):
    u = _mm(h2, w["w_up"], b_chip=1, out_dtype=CDT, ride=rides.get("up"), name=f"{tag}_up")
    act = _conv_gate_fwd(u, w["conv_w"], w["conv_b"], M=M, ride=rides.get("conv"), name=f"{tag}_conv")
    f = _mm(act, w["w_down"], b_chip=0, out_dtype=CDT, ride=rides.get("down"), name=f"{tag}_down")
    return f, dict(h2=h2, u=u, act=act)


def _ffn_bwd(dyf, sv, w, M, tag, rides):
    dact = _mm(dyf, w["w_down"], tb=True, b_chip=0, out_dtype=CDT, ride=rides.get("ddown_x"), name=f"{tag}_ddown_x")
    g = {"w_down": _mm(sv["act"], dyf, ta=True, out_chip=0, out_dtype=CDT, ride=rides.get("ddown_w"), name=f"{tag}_ddown_w")}
    duc, cacc = _conv_gate_bwd(sv["u"], dact, w["conv_w"], w["conv_b"], M=M, name=f"{tag}_conv_bwd")
    du = _conv_t(duc, w["conv_w"], M=M, name=f"{tag}_conv_t")
    dh2 = _mm(du, w["w_up"], tb=True, a_split=True, b_chip=1, out_dtype=CDT, ride=rides.get("dup_x"), name=f"{tag}_dup_x")
    g["w_up"] = _mm(sv["h2"], du, ta=True, b_chip=1, out_chip=1, out_dtype=CDT, name=f"{tag}_dup_w")
    g["conv_w"] = jnp.concatenate([cacc[0, 0:3], cacc[1, 0:3]], axis=1)
    g["conv_b"] = jnp.concatenate([cacc[0, 3], cacc[1, 3]], axis=0)
    return dh2, g


def _norm_grads(acc, gain, scale):
    p = acc[:, 1]
    return acc[:, 0], p * gain, jnp.sum(p * (1.0 + scale[:, 0]), axis=0)


def _local_step(x, tgt, mods, weights_of, final_g, *, M, fwd_rides=None, bwd_rides=None, on_grads=None):
    T, D = x.shape
    L = mods.shape[0]
    HD = D // ATT_HEADS
    tabs = _rope_tables(T - M, M, HD)
    sel = lambda i, k: mods[i][:, k:k + 1, :]
    saved = []
    xs, y_prev, gate_prev = x, None, None
    for i in range(L):
        w = weights_of(i, "mixer")
        rides = {} if fwd_rides is None else fwd_rides(i)
        x_in, h = _norm_fwd(xs, y_prev, gate_prev, w["g_mix"], sel(i, 0), sel(i, 1), M=M, name=f"l{i}_norm_mix")
        if "w_qkv" in w:
            y_mix, sm = _attn_layer_fwd(h, w, tabs, M, f"l{i}", rides)
        else:
            y_mix, sm = _gla_layer_fwd(h, w, M, f"l{i}", rides)
        w = {**w, **weights_of(i, "ffn")}
        x_mid, h2 = _norm_fwd(x_in, y_mix, sel(i, 2), w["g_ffn"], sel(i, 3), sel(i, 4), M=M, name=f"l{i}_norm_ffn")
        f, sf = _ffn_fwd(h2, w, M, f"l{i}", rides)
        saved.append(dict(x_in=x_in, x_mid=x_mid, y_mix=y_mix, f=f, sm=sm, sf=sf, w=w))
        xs, y_prev, gate_prev = x_mid, f, sel(i, 5)

    loss_parts, dx, dyf, acc = _final_loss(xs, y_prev, gate_prev, tgt, final_g, M=M, name="final_loss")
    loss = jnp.sum(loss_parts[:, 0, 0])
    d_final_g = acc[0, 0] + acc[1, 0]
    dmods = [None] * L
    grads = [None] * L
    dgate_ffn = acc[:, 2]
    for i in reversed(range(L)):
        sv = saved[i]
        w = sv["w"]
        dh2, g = _ffn_bwd(dyf, sv["sf"], w, M, f"l{i}", {} if bwd_rides is None else bwd_rides(i))
        dx, dy_mix, acc = _norm_bwd(sv["x_mid"], dh2, dx, w["g_ffn"], sel(i, 4), sv["y_mix"], sel(i, 2), M=M, name=f"l{i}_norm_ffn_bwd")
        dsh_f, dsc_f, g["g_ffn"] = _norm_grads(acc, w["g_ffn"], sel(i, 4))
        dgate_mix = acc[:, 2]
        if "w_qkv" in w:
            dh, gm = _attn_layer_bwd(dy_mix, sv["sm"], w, tabs, M, f"l{i}")
        else:
            dh, gm = _gla_layer_bwd(dy_mix, sv["sm"], w, M, f"l{i}")
        g.update(gm)
        if i > 0:
            dx, dyf, acc = _norm_bwd(sv["x_in"], dh, dx, w["g_mix"], sel(i, 1), saved[i - 1]["f"], sel(i - 1, 5), M=M, name=f"l{i}_norm_mix_bwd")
        else:
            dx, dyf, acc = _norm_bwd(sv["x_in"], dh, dx, w["g_mix"], sel(i, 1), None, None, M=M, name=f"l{i}_norm_mix_bwd")
        dsh_m, dsc_m, g["g_mix"] = _norm_grads(acc, w["g_mix"], sel(i, 1))
        dmods[i] = jnp.stack([dsh_m, dsc_m, dgate_mix, dsh_f, dsc_f, dgate_ffn], axis=1)
        dgate_ffn = acc[:, 2]
        grads[i] = g if on_grads is None else on_grads(i, g)
    return loss, dx, jnp.stack(dmods, axis=0), grads, d_final_g


ANY = pl.BlockSpec(memory_space=pl.ANY)


def _me():
    return lax.axis_index("x"), lax.axis_index("y"), lax.axis_index("c")


def _other_chips(mx, my):
    return [(1 - mx, my), (mx, 1 - my), (1 - mx, 1 - my)]


def _rcopy(src, dst, sems, k, dev):
    send_sems, recv_sems = sems
    return pltpu.make_async_remote_copy(src_ref=src, dst_ref=dst, send_sem=send_sems.at[k], recv_sem=recv_sems.at[k],
                                        device_id=dev, device_id_type=MESH)


def _all_gather8(x, *, name):
    m, n = x.shape

    def body(x_ref, out_ref, send_sems, recv_sems, local_sem):
        mx, my, mc = _me()
        sems = (send_sems, recv_sems)
        me, sib = (mx, my, mc), (mx, my, 1 - mc)
        chips = _other_chips(mx, my)
        blk = lambda d: out_ref.at[4 * d[0] + 2 * d[1] + d[2]]
        mine = pltpu.make_async_copy(x_ref, blk(me), local_sem)
        mine.start()
        first = [_rcopy(x_ref, blk(me), sems, 0, sib)]
        first += [_rcopy(x_ref, blk(me), sems, 1 + j, (*ch, mc)) for j, ch in enumerate(chips)]
        for cp in first:
            cp.start()
        passed = [_rcopy(blk((*ch, mc)), blk((*ch, mc)), sems, 4 + j, sib) for j, ch in enumerate(chips)]
        for j, ch in enumerate(chips):
            _rcopy(x_ref, blk((*ch, mc)), sems, 1 + j, me).wait_recv()
            passed[j].start()
        _rcopy(x_ref, blk(sib), sems, 0, me).wait_recv()
        for j, ch in enumerate(chips):
            _rcopy(x_ref, blk((*ch, 1 - mc)), sems, 4 + j, me).wait_recv()
        for cp in first + passed:
            cp.wait_send()
        mine.wait()

    return pl.pallas_call(
        body, name=name, out_shape=jax.ShapeDtypeStruct((8, m, n), x.dtype), in_specs=[ANY], out_specs=ANY,
        scratch_shapes=[pltpu.SemaphoreType.DMA((7,)), pltpu.SemaphoreType.DMA((7,)), pltpu.SemaphoreType.DMA],
    )(x)


ROW_TILES = (512, 352, 256, 128)


def _sem_pairs(n):
    return [pltpu.SemaphoreType.DMA((n,)), pltpu.SemaphoreType.DMA((n,))]


def _place(w, layer, pos, *, name):
    _, a, b = w.shape
    tr = _pick(a, ROW_TILES)

    def body(pos_ref, w_ref, o_ref):
        o_ref[...] = w_ref[...].astype(o_ref.dtype)

    return pl.pallas_call(
        body, name=name, out_shape=jax.ShapeDtypeStruct((4, a, b), CDT),
        grid_spec=pltpu.PrefetchScalarGridSpec(
            num_scalar_prefetch=1, grid=(a // tr,),
            in_specs=[pl.BlockSpec((None, tr, b), lambda i, pos: (layer, i, 0))],
            out_specs=pl.BlockSpec((None, tr, b), lambda i, pos: (pos[0], i, 0))),
        compiler_params=_cparams(("parallel",)),
    )(pos, w)


def _gather_layer(bufs, *, name):
    n = len(bufs)

    def body(*refs):
        outs = refs[n:2 * n]
        sems = (refs[2 * n], refs[2 * n + 1])
        mx, my, mc = _me()
        me, sib = (mx, my, mc), (mx, my, 1 - mc)
        chips = _other_chips(mx, my)
        p = 2 * mx + my
        qs = [2 * ch[0] + ch[1] for ch in chips]
        halves = [(pl.ds(mc * (o.shape[1] // 2), o.shape[1] // 2), pl.ds((1 - mc) * (o.shape[1] // 2), o.shape[1] // 2)) for o in outs]
        first = []
        for t, o in enumerate(outs):
            mine = halves[t][0]
            first += [_rcopy(o.at[p, mine], o.at[p, mine], sems, 6 * t + j, (*ch, mc)) for j, ch in enumerate(chips)]
        for cp in first:
            cp.start()
        passed = []
        for j in range(3):
            for t, o in enumerate(outs):
                mine = halves[t][0]
                _rcopy(o.at[qs[j], mine], o.at[qs[j], mine], sems, 6 * t + j, me).wait_recv()
                fwd = _rcopy(o.at[qs[j], mine], o.at[qs[j], mine], sems, 6 * t + 3 + j, sib)
                fwd.start()
                passed.append(fwd)
        for j in range(3):
            for t, o in enumerate(outs):
                theirs = halves[t][1]
                _rcopy(o.at[qs[j], theirs], o.at[qs[j], theirs], sems, 6 * t + 3 + j, me).wait_recv()
        for cp in first + passed:
            cp.wait_send()

    return pl.pallas_call(
        body, name=name, out_shape=[jax.ShapeDtypeStruct(b.shape, b.dtype) for b in bufs],
        in_specs=[ANY] * n, out_specs=[ANY] * n, input_output_aliases={t: t for t in range(n)},
        scratch_shapes=_sem_pairs(6 * n),
    )(*bufs)


def _gather_ici_plan(outs):
    mx, my, mc = _me()
    p = 2 * mx + my
    for t, o in enumerate(outs):
        ah = o.shape[1] // 2
        mine = pl.ds(mc * ah, ah)
        for j, ch in enumerate(_other_chips(mx, my)):
            yield 3 * t + j, o.at[p, mine], o.at[2 * ch[0] + ch[1], mine], (*ch, mc)


def _ride_gather(bufs):
    def start(r_in, r_out, sems):
        for k, src, _, dev in _gather_ici_plan(r_out):
            _rcopy(src, src, sems, k, dev).start()

    def finish(r_in, r_out, sems):
        for k, _, land, _ in _gather_ici_plan(r_out):
            _rcopy(land, land, sems, k, _me()).wait_recv()
        for k, src, _, dev in _gather_ici_plan(r_out):
            _rcopy(src, src, sems, k, dev).wait_send()

    return dict(ins=list(bufs), outs=[jax.ShapeDtypeStruct(b.shape, b.dtype) for b in bufs],
                alias={t: t for t in range(len(bufs))}, nsem=3 * len(bufs), start=start, finish=finish)


def _gather_d2d(bufs, *, name):
    n = len(bufs)

    def body(*refs):
        outs = refs[n:2 * n]
        sems = (refs[2 * n], refs[2 * n + 1])
        mx, my, mc = _me()
        sib = (mx, my, 1 - mc)
        qs = [2 * ch[0] + ch[1] for ch in _other_chips(mx, my)]
        cps = []
        for t, o in enumerate(outs):
            ah = o.shape[1] // 2
            for j, q in enumerate(qs):
                mine = o.at[q, pl.ds(mc * ah, ah)]
                cps.append((_rcopy(mine, mine, sems, 3 * t + j, sib), o.at[q, pl.ds((1 - mc) * ah, ah)], 3 * t + j))
        for cp, _, _ in cps:
            cp.start()
        for _, theirs, k in cps:
            _rcopy(theirs, theirs, sems, k, sib).wait_recv()
        for cp, _, _ in cps:
            cp.wait_send()

    return pl.pallas_call(
        body, name=name, out_shape=[jax.ShapeDtypeStruct(b.shape, b.dtype) for b in bufs],
        in_specs=[ANY] * n, out_specs=[ANY] * n, input_output_aliases={t: t for t in range(n)},
        scratch_shapes=_sem_pairs(3 * n),
    )(*bufs)


def _scatter_plan(ins, outs):
    mx, my, mc = _me()
    p = 2 * mx + my
    for t, (s, o) in enumerate(zip(ins, outs)):
        for j, ch in enumerate(_other_chips(mx, my)):
            q = 2 * ch[0] + ch[1]
            yield 3 * t + j, s.at[q], o.at[p], o.at[q], (*ch, mc)


def _ride_scatter(parts):
    def start(r_in, r_out, sems):
        for k, src, dst, _, dev in _scatter_plan(r_in, r_out):
            _rcopy(src, dst, sems, k, dev).start()

    def finish(r_in, r_out, sems):
        for k, _, _, land, _ in _scatter_plan(r_in, r_out):
            _rcopy(land, land, sems, k, _me()).wait_recv()
        for k, src, dst, _, dev in _scatter_plan(r_in, r_out):
            _rcopy(src, dst, sems, k, dev).wait_send()

    return dict(ins=list(parts), outs=[jax.ShapeDtypeStruct(s.shape, s.dtype) for s in parts], alias={},
                nsem=3 * len(parts), start=start, finish=finish)


def _rs_split(gs, *, name):
    n = len(gs)

    def body(*refs):
        ins, outs = refs[:n], refs[n:2 * n]
        sems = (refs[2 * n], refs[2 * n + 1])
        mx, my, mc = _me()
        sib = (mx, my, 1 - mc)
        cps = []
        for t, (g, o) in enumerate(zip(ins, outs)):
            ah = g.shape[1] // 2
            cps.append(_rcopy(g.at[:, pl.ds((1 - mc) * ah, ah), :], o, sems, t, sib))
        for cp in cps:
            cp.start()
        for cp in cps:
            cp.wait_recv()
        for cp in cps:
            cp.wait_send()

    return pl.pallas_call(
        body, name=name, out_shape=[jax.ShapeDtypeStruct((4, g.shape[1] // 2, g.shape[2]), g.dtype) for g in gs],
        in_specs=[ANY] * n, out_specs=[ANY] * n, scratch_shapes=_sem_pairs(n),
    )(*gs)


def _rs_add(g, got, pos, *, name):
    _, a, b = g.shape
    ah = a // 2
    tr = _pick(ah, ROW_TILES)
    nb = ah // tr

    def body(pos_ref, g_ref, r_ref, o_ref):
        o_ref[...] = (g_ref[...].astype(F32) + r_ref[...].astype(F32)).astype(o_ref.dtype)

    blk = pl.BlockSpec((None, tr, b), lambda q, i, pos: (q, i, 0))
    return pl.pallas_call(
        body, name=name, out_shape=jax.ShapeDtypeStruct((4, ah, b), g.dtype),
        grid_spec=pltpu.PrefetchScalarGridSpec(
            num_scalar_prefetch=1, grid=(4, nb),
            in_specs=[pl.BlockSpec((None, tr, b), lambda q, i, pos: (q, pos[1] * nb + i, 0)), blk], out_specs=blk),
        compiler_params=_cparams(("parallel", "parallel")),
    )(pos, g, got)


def _rs_scatter(ps, *, name):
    n = len(ps)

    def body(*refs):
        ins, outs = refs[:n], refs[n:2 * n]
        sems = (refs[2 * n], refs[2 * n + 1])
        mx, my, mc = _me()
        me = (mx, my, mc)
        chips = _other_chips(mx, my)
        p = 2 * mx + my
        sends = []
        for t, (s, o) in enumerate(zip(ins, outs)):
            sends += [_rcopy(s.at[2 * ch[0] + ch[1]], o.at[p], sems, 3 * t + j, (*ch, mc)) for j, ch in enumerate(chips)]
        for cp in sends:
            cp.start()
        for t, (s, o) in enumerate(zip(ins, outs)):
            for j, ch in enumerate(chips):
                _rcopy(s.at[p], o.at[2 * ch[0] + ch[1]], sems, 3 * t + j, me).wait_recv()
        for cp in sends:
            cp.wait_send()

    return pl.pallas_call(
        body, name=name, out_shape=[jax.ShapeDtypeStruct(s.shape, s.dtype) for s in ps],
        in_specs=[ANY] * n, out_specs=[ANY] * n, scratch_shapes=_sem_pairs(3 * n),
    )(*ps)


def _rs_sum(part, recv, buf, layer, pos, *, name):
    _, ah, b = part.shape
    tr = _pick(ah, ROW_TILES)
    nb = ah // tr

    def body(pos_ref, p_ref, r0, r1, r2, buf_ref, o_ref):
        o_ref[...] = ((p_ref[...].astype(F32) + r0[...].astype(F32)) + r1[...].astype(F32)) + r2[...].astype(F32)

    other = lambda k: pl.BlockSpec((None, tr, b), lambda i, pos: (jnp.where(pos[0] <= k, k + 1, k), i, 0))
    return pl.pallas_call(
        body, name=name, out_shape=jax.ShapeDtypeStruct(buf.shape, buf.dtype),
        grid_spec=pltpu.PrefetchScalarGridSpec(
            num_scalar_prefetch=1, grid=(nb,),
            in_specs=[pl.BlockSpec((None, tr, b), lambda i, pos: (pos[0], i, 0)), other(0), other(1), other(2), ANY],
            out_specs=pl.BlockSpec((None, tr, b), lambda i, pos: (layer, pos[1] * nb + i, 0))),
        input_output_aliases={5: 0},
        compiler_params=_cparams(("parallel",)),
    )(pos, part, recv, recv, recv, buf)


def _rs_share(bufs, layers, *, name):
    n = len(bufs)

    def body(*refs):
        outs = refs[n:2 * n]
        sems = (refs[2 * n], refs[2 * n + 1])
        mx, my, mc = _me()
        sib = (mx, my, 1 - mc)
        cps = []
        for t, o in enumerate(outs):
            ah = o.shape[1] // 2
            mine = o.at[layers[t], pl.ds(mc * ah, ah)]
            cps.append((_rcopy(mine, mine, sems, t, sib), o.at[layers[t], pl.ds((1 - mc) * ah, ah)]))
        for cp, _ in cps:
            cp.start()
        for t, (cp, theirs) in enumerate(cps):
            _rcopy(theirs, theirs, sems, t, sib).wait_recv()
        for cp, _ in cps:
            cp.wait_send()

    return pl.pallas_call(
        body, name=name, out_shape=[jax.ShapeDtypeStruct(b.shape, b.dtype) for b in bufs],
        in_specs=[ANY] * n, out_specs=[ANY] * n, input_output_aliases={t: t for t in range(n)},
        scratch_shapes=_sem_pairs(n),
    )(*bufs)


def _sum_lead(a, *, name):
    n, R, W = a.shape
    tr = _pick(R, (PACK_ROWS,))
    specs = [pl.BlockSpec((1, tr, W), functools.partial(lambda i, q: (q, i, 0), q=q)) for q in range(n)]

    def body(*refs):
        acc = refs[0][0].astype(F32)
        for r in refs[1:n]:
            acc = acc + r[0].astype(F32)
        refs[n][...] = acc

    return pl.pallas_call(
        body, name=name, grid=(R // tr,), in_specs=specs, out_specs=pl.BlockSpec((tr, W), lambda i: (i, 0)),
        out_shape=jax.ShapeDtypeStruct((R, W), F32), compiler_params=_cparams(("parallel",)),
    )(*([a] * n))


SMALL_SHARDED = (("ffn_conv_w", 2), ("gla_gf_w1", 1), ("gla_gf_w2", 2), ("gla_gf_b", 1), ("gla_gb_w1", 1), ("gla_gb_w2", 2),
                 ("gla_gb_b", 1), ("gla_onorm_g", 1))


def _rows_of(flat, width):
    rows = -(-flat.shape[0] // (8 * width)) * 8
    return jnp.pad(flat, (0, rows * width - flat.shape[0])).reshape(rows, width)


def _size(shape):
    n = 1
    for s in shape:
        n *= s
    return n


def _gather_small(shards):
    flat = jnp.concatenate([shards[name].astype(F32).reshape(-1) for name, _ in SMALL_SHARDED])
    got = _all_gather8(_rows_of(flat, SMALL_W), name="gather_small_w")[0::2].reshape(4, -1)
    full, off = {}, 0
    for name, ax in SMALL_SHARDED:
        shape = shards[name].shape
        n = _size(shape)
        seg = jnp.moveaxis(got[:, off:off + n].reshape((4,) + shape), 0, ax)
        full[name] = seg.reshape(shape[:ax] + (4 * shape[ax],) + shape[ax + 1:])
        off += n
    return full


WEIGHTS = ("c_ctx", "ada_w", "ada_b", "norm_mix_g", "norm_ffn_g", "ffn_w_up", "ffn_conv_w", "ffn_conv_b", "ffn_w_down",
           "attn_w_qkv", "attn_sink", "attn_w_o", "gla_w_in", "gla_gf_w1", "gla_gf_w2", "gla_gf_b", "gla_gb_w1", "gla_gb_w2",
           "gla_gb_b", "gla_onorm_g", "gla_w_o", "final_norm_g")
REPLICATED = ("norm_mix_g", "norm_ffn_g", "ffn_conv_b", "attn_sink", "final_norm_g", "c_ctx")
SMALL_W = 2048
ROWS16 = 16


def _layer_big(i):
    j = i // 2
    mixer = [("w_qkv", "attn_w_qkv", j), ("w_o", "attn_w_o", j)] if i % 2 == 0 else [("w_in", "gla_w_in", j), ("w_o", "gla_w_o", j)]
    return [("w_up", "ffn_w_up", i), ("w_down", "ffn_w_down", i)] + mixer


def _layer_weights(i, big, small, rep):
    D = rep["norm_mix_g"].shape[1]
    j = i // 2
    w = dict(g_mix=rep["norm_mix_g"][i][None], g_ffn=rep["norm_ffn_g"][i][None], conv_w=small["ffn_conv_w"][i],
             conv_b=rep["ffn_conv_b"][i][None], **big)
    if i % 2 == 0:
        w["sink"] = rep["attn_sink"][j]
    else:
        r = GATE_RANK
        w2 = jnp.zeros((128, D), F32)
        w2 = w2.at[0:r, 0:D // 2].set(small["gla_gf_w2"][j]).at[r:2 * r, D // 2:].set(small["gla_gb_w2"][j])
        w1x = jnp.concatenate([small["gla_gf_w1"][j], small["gla_gb_w1"][j], jnp.zeros((D, 128 - 2 * r), F32)], axis=1)
        w.update(w1x=w1x.astype(CDT), w2=w2.astype(CDT), gbias=jnp.concatenate([small["gla_gf_b"][j], small["gla_gb_b"][j]])[None],
                 onorm=small["gla_onorm_g"][j][None])
    return w


def _small_grads(grads, D):
    att = [g for g in grads if "sink" in g]
    gla = [g for g in grads if "w1x" in g]
    st = lambda xs: jnp.stack(xs, axis=0)
    r = GATE_RANK
    return {
        "ffn_conv_w": st([g["conv_w"] for g in grads]),
        "gla_gf_w1": st([g["w1x"][:, 0:r] for g in gla]), "gla_gb_w1": st([g["w1x"][:, r:2 * r] for g in gla]),
        "gla_gf_w2": st([g["w2"][0:r, :D // 2] for g in gla]), "gla_gb_w2": st([g["w2"][r:2 * r, D // 2:] for g in gla]),
        "gla_gf_b": st([g["gbias"][:D // 2] for g in gla]), "gla_gb_b": st([g["gbias"][D // 2:] for g in gla]),
        "gla_onorm_g": st([g["onorm"] for g in gla]),
        "norm_mix_g": st([g["g_mix"] for g in grads]), "norm_ffn_g": st([g["g_ffn"] for g in grads]),
        "ffn_conv_b": st([g["conv_b"] for g in grads]), "attn_sink": st([g["sink"] for g in att]),
    }


def kernel(x, c, ctx, c_ctx, ada_w, ada_b, norm_mix_g, norm_ffn_g, ffn_w_up, ffn_conv_w, ffn_conv_b, ffn_w_down, attn_w_qkv, attn_sink, attn_w_o, gla_w_in, gla_gf_w1, gla_gf_w2, gla_gf_b, gla_gb_w1, gla_gb_w2, gla_gb_b, gla_onorm_g, gla_w_o, final_norm_g, loss_target, m_c_ctx, m_ada_w, m_ada_b, m_norm_mix_g, m_norm_ffn_g, m_ffn_w_up, m_ffn_conv_w, m_ffn_conv_b, m_ffn_w_down, m_attn_w_qkv, m_attn_sink, m_attn_w_o, m_gla_w_in, m_gla_gf_w1, m_gla_gf_w2, m_gla_gf_b, m_gla_gb_w1, m_gla_gb_w2, m_gla_gb_b, m_gla_onorm_g, m_gla_w_o, m_final_norm_g, v_c_ctx, v_ada_w, v_ada_b, v_norm_mix_g, v_norm_ffn_g, v_ffn_w_up, v_ffn_conv_w, v_ffn_conv_b, v_ffn_w_down, v_attn_w_qkv, v_attn_sink, v_attn_w_o, v_gla_w_in, v_gla_gf_w1, v_gla_gf_w2, v_gla_gf_b, v_gla_gb_w1, v_gla_gb_w2, v_gla_gb_b, v_gla_onorm_g, v_gla_w_o, v_final_norm_g):
    wts = dict(c_ctx=c_ctx, ada_w=ada_w, ada_b=ada_b, norm_mix_g=norm_mix_g, norm_ffn_g=norm_ffn_g, ffn_w_up=ffn_w_up,
               ffn_conv_w=ffn_conv_w, ffn_conv_b=ffn_conv_b, ffn_w_down=ffn_w_down, attn_w_qkv=attn_w_qkv, attn_sink=attn_sink,
               attn_w_o=attn_w_o, gla_w_in=gla_w_in, gla_gf_w1=gla_gf_w1, gla_gf_w2=gla_gf_w2, gla_gf_b=gla_gf_b,
               gla_gb_w1=gla_gb_w1, gla_gb_w2=gla_gb_w2, gla_gb_b=gla_gb_b, gla_onorm_g=gla_onorm_g, gla_w_o=gla_w_o,
               final_norm_g=final_norm_g)
    mom_m = dict(zip(WEIGHTS, (m_c_ctx, m_ada_w, m_ada_b, m_norm_mix_g, m_norm_ffn_g, m_ffn_w_up, m_ffn_conv_w, m_ffn_conv_b,
                               m_ffn_w_down, m_attn_w_qkv, m_attn_sink, m_attn_w_o, m_gla_w_in, m_gla_gf_w1, m_gla_gf_w2,
                               m_gla_gf_b, m_gla_gb_w1, m_gla_gb_w2, m_gla_gb_b, m_gla_onorm_g, m_gla_w_o, m_final_norm_g)))
    mom_v = dict(zip(WEIGHTS, (v_c_ctx, v_ada_w, v_ada_b, v_norm_mix_g, v_norm_ffn_g, v_ffn_w_up, v_ffn_conv_w, v_ffn_conv_b,
                               v_ffn_w_down, v_attn_w_qkv, v_attn_sink, v_attn_w_o, v_gla_w_in, v_gla_gf_w1, v_gla_gf_w2,
                               v_gla_gf_b, v_gla_gb_w1, v_gla_gb_w2, v_gla_gb_b, v_gla_onorm_g, v_gla_w_o, v_final_norm_g)))
    L, D, W6 = ada_w.shape[0], ada_w.shape[1], ada_w.shape[2]
    M = ctx.shape[1]
    mx, my, mc = _me()
    chip = 2 * mx + my
    batch = 4 * mx + 2 * my + mc

    crow = jnp.concatenate([c.astype(F32), jnp.zeros((7, D), F32)], axis=0)
    call = _all_gather8(crow, name="gather_c")[:, 0, :]
    s16 = jnp.concatenate([jax.nn.silu(call), jax.nn.silu(c_ctx)[None], jnp.zeros((ROWS16 - 9, D), F32)], axis=0)
    s16c = s16.astype(CDT)
    ada_c = ada_w.astype(CDT)
    mod_cols = jnp.concatenate([_mm(s16c, ada_c[i], out_dtype=F32, name=f"mods_l{i}") for i in range(L)], axis=0)
    mod_all = _all_gather8(mod_cols, name="gather_mods")
    mod_all = mod_all.reshape(4, 2, L, ROWS16, W6)[:, 0]
    mod_all = jnp.moveaxis(mod_all, 0, 2).reshape(L, ROWS16, 4 * W6) + ada_b[:, None, :]
    mod_mine = jnp.stack([mod_all[:, 8], lax.dynamic_index_in_dim(mod_all, batch, axis=1, keepdims=False)], axis=1)
    mods = mod_mine.reshape(L, 2, N_MOD, D)

    pos = jnp.stack([chip, mc]).astype(jnp.int32)
    small_w = _gather_small({name: wts[name] for name, _ in SMALL_SHARDED})
    def place(i):
        return [_place(wts[name], j, pos, name=f"l{i}_place_{key}") for key, name, j in _layer_big(i)]

    placed0 = place(0)
    gathered = {0: [None, None] + list(_gather_layer(placed0[2:4], name="l0_gather_mixer"))}
    carried = {0: dict(up=_ride_gather(placed0[0:1]), down=_ride_gather(placed0[1:2]))}

    def weights_of(i, part):
        keys = [key for key, _, _ in _layer_big(i)]
        if part == "mixer":
            if i not in gathered:
                r = carried.pop(i)
                gathered[i] = _gather_d2d(r["up"]["result"] + r["down"]["result"] + r["mixer"]["result"], name=f"l{i}_gather_d2d")
            return _layer_weights(i, dict(zip(keys[2:4], gathered[i][2:4])), small_w, wts)
        if i == 0:
            r = carried.pop(0)
            gathered[0][0:2] = _gather_d2d(r["up"]["result"] + r["down"]["result"], name="l0_gather_d2d")
        return dict(zip(keys[0:2], gathered[i][0:2]))

    def fwd_rides(i):
        hosts = dict(attn=carried[0]["up"], mix=carried[0]["down"]) if i == 0 else {}
        if i + 1 < L:
            bufs = place(i + 1)
            nxt = dict(up=_ride_gather(bufs[0:1]), down=_ride_gather(bufs[1:2]), mixer=_ride_gather(bufs[2:4]))
            carried[i + 1] = nxt
            hosts.update(up=nxt["up"], down=nxt["down"])
            hosts["conv" if i % 2 == 0 else "mix"] = nxt["mixer"]
        return hosts

    red = {name: jnp.zeros(wts[name].shape, F32) for name in ("ffn_w_up", "ffn_w_down", "attn_w_qkv", "attn_w_o", "gla_w_in", "gla_w_o")}
    pending = {}

    def bwd_rides(i):
        if i + 1 not in pending:
            return {}
        parts = pending[i + 1]["parts"]
        r = dict(dup_x=_ride_scatter(parts[0:1]), ddown_x=_ride_scatter(parts[1:2]), ddown_w=_ride_scatter(parts[2:4]))
        pending[i + 1]["rides"] = r
        return r

    def finish_reduce(i):
        keys = _layer_big(i)
        parts, r = pending[i]["parts"], pending.pop(i).get("rides")
        if "adam" in r:
            recvs = r["adam"]["result"] + list(_rs_scatter(parts[1:4], name=f"l{i}_rs_scatter"))
        else:
            recvs = r["dup_x"]["result"] + r["ddown_x"]["result"] + r["ddown_w"]["result"]
        outs = [_rs_sum(part, recv, red[name], j, pos, name=f"l{i}_rs_sum_{key}") for (key, name, j), part, recv in zip(keys, parts, recvs)]
        outs = _rs_share(outs, [j for _, _, j in keys], name=f"l{i}_rs_share")
        for (_, name, _), out in zip(keys, outs):
            red[name] = out

    def reduce_layer(i, g):
        if i + 1 in pending:
            finish_reduce(i + 1)
        keys = _layer_big(i)
        gs = [g.pop(key) for key, _, _ in keys]
        gots = _rs_split(gs, name=f"l{i}_rs_split")
        pending[i] = dict(parts=[_rs_add(gv, got, pos, name=f"l{i}_rs_add_{key}") for (key, _, _), gv, got in zip(keys, gs, gots)])
        return g

    xcat = jnp.concatenate([ctx[0], x[0]], axis=0)
    loss, dx, dmods, grads, d_final_g = _local_step(xcat, loss_target[0], mods, weights_of, final_norm_g[None], M=M,
                                                    fwd_rides=fwd_rides, bwd_rides=bwd_rides, on_grads=reduce_layer)
    loss = lax.psum(loss, ("x", "y", "c"))
    grad_x = dx[M:][None]

    dm_all = _all_gather8(dmods.reshape(L * 2, N_MOD * D), name="gather_dmods")
    dm_sum = _sum_lead(dm_all, name="dmods_sum").reshape(L, 2, N_MOD * D)
    dm_rows = dm_all.reshape(8, L, 2, N_MOD * D)[:, :, 1]
    dm16 = jnp.concatenate([jnp.moveaxis(dm_rows, 0, 1), dm_sum[:, 0:1], jnp.zeros((L, ROWS16 - 9, N_MOD * D), F32)], axis=1)
    dm16 = lax.dynamic_slice_in_dim(dm16, chip * W6, W6, axis=2).astype(CDT)
    g_ada_w = jnp.stack([_mm(s16c, dm16[i], ta=True, out_dtype=F32, name=f"dada_w_l{i}") for i in range(L)], axis=0)
    ds16 = _mm(dm16[0], ada_c[0], tb=True, out_dtype=F32, name="dcond_l0")
    for i in range(1, L):
        ds16 = ds16 + _mm(dm16[i], ada_c[i], tb=True, out_dtype=F32, name=f"dcond_l{i}")
    d_sctx = ds16[8] * jnp.where(mc == 0, 1.0, 0.0)

    gfull = _small_grads(grads, D)
    gfull["final_norm_g"] = d_final_g
    gfull["c_ctx"] = d_sctx

    small_names = list(REPLICATED) + [name for name, _ in SMALL_SHARDED]
    flat = jnp.concatenate([gfull[name].astype(F32).reshape(-1) for name in small_names])
    small = _sum_lead(_all_gather8(_rows_of(flat, SMALL_W), name="gather_small_g"), name="small_sum").reshape(-1)
    off = 0
    for name in small_names:
        shape = gfull[name].shape
        red[name] = small[off:off + _size(shape)].reshape(shape)
        off += _size(shape)
    for name, ax in SMALL_SHARDED:
        shape = wts[name].shape
        g4 = red[name].reshape(shape[:ax] + (4, shape[ax]) + shape[ax + 1:])
        red[name] = lax.dynamic_index_in_dim(g4, chip, axis=ax, keepdims=False)
    sig = jax.nn.sigmoid(c_ctx)
    red["c_ctx"] = red["c_ctx"] * (sig * (1.0 + c_ctx * (1.0 - sig)))
    red["ada_w"] = g_ada_w
    red["ada_b"] = dm_sum[:, 0] + dm_sum[:, 1]

    deltas, new_m, new_v = {}, {}, {}
    pending[0]["rides"] = dict(adam=_ride_scatter(pending[0]["parts"][0:1]))
    for name in ("ada_w",) + tuple(n for n in WEIGHTS if n != "ada_w"):
        w = wts[name]
        view = (lambda a: a.reshape(-1, a.shape[-1])) if w.ndim > 1 else (lambda a: a.reshape(1, -1))
        ride = pending[0]["rides"]["adam"] if name == "ada_w" else None
        d, m2, v2 = _adamw(view(w), view(red[name]), view(mom_m[name]), view(mom_v[name]), ride=ride, name=f"adamw_{name}")
        deltas[name], new_m[name], new_v[name] = d.reshape(w.shape), m2.reshape(w.shape), v2.reshape(w.shape)
        if name == "ada_w":
            finish_reduce(0)
    return (loss, grad_x, *[red[n] for n in WEIGHTS], *[deltas[n] for n in WEIGHTS], *[new_m[n] for n in WEIGHTS],
            *[new_v[n] for n in WEIGHTS])
```

```python
import functools

import jax
import jax.numpy as jnp
from jax import lax
from jax.experimental import pallas as pl
from jax.experimental.pallas import tpu as pltpu

F32 = jnp.float32
CDT = jnp.bfloat16
VMEM_LIMIT = 56 * 1024 * 1024
MESH = pl.DeviceIdType.MESH

ATT_HEADS = 16
ATT_KV = 4
ATT_G = ATT_HEADS // ATT_KV
BLK = 128
GRID_W = 64
ROPE_BASE = 10000.0
GLA_H = 4
GATE_RANK = 16
GATE_NORM = 16.0
CHUNK = 64
EPS = 1e-6
N_MOD = 6
LR, B1, B2, AEPS, WD, STEP = 0.001, 0.9, 0.999, 1e-08, 0.01, 10
PACK_ROWS = 512

NN = (((1,), (0,)), ((), ()))
NT = (((1,), (1,)), ((), ()))
TN = (((0,), (0,)), ((), ()))


def _dg(a, b, dims):
    return lax.dot_general(a, b, dims, preferred_element_type=F32)


def _pick(dim, cands):
    for c in cands:
        if dim % c == 0:
            return c
    return dim


def _cparams(sem):
    return pltpu.CompilerParams(dimension_semantics=sem, vmem_limit_bytes=VMEM_LIMIT)


def _pcall(body, *, name, grid, in_specs, out_specs, out_shape, scratch_shapes=(), sem, ride=None):
    if ride is None:
        return pl.pallas_call(body, name=name, grid=grid, in_specs=list(in_specs), out_specs=out_specs, out_shape=out_shape,
                              scratch_shapes=list(scratch_shapes), compiler_params=_cparams(sem))
    multi = isinstance(out_shape, (tuple, list))
    o_specs = list(out_specs) if multi else [out_specs]
    o_shapes = list(out_shape) if multi else [out_shape]
    n_in, n_out, n_scr = len(in_specs), len(o_specs), len(scratch_shapes)
    n_rin, n_rout = len(ride["ins"]), len(ride["outs"])
    total = 1
    for extent in grid:
        total *= extent

    def carrying(*refs):
        o0 = n_in + n_rin
        s0 = o0 + n_out + n_rout
        r_in, r_out, sems = refs[n_in:o0], refs[o0 + n_out:s0], (refs[-2], refs[-1])
        step = 0
        for axis, extent in enumerate(grid):
            step = step * extent + pl.program_id(axis)

        @pl.when(step == 0)
        def _():
            ride["start"](r_in, r_out, sems)

        body(*refs[:n_in], *refs[o0:o0 + n_out], *refs[s0:s0 + n_scr])

        @pl.when(step == total - 1)
        def _():
            ride["finish"](r_in, r_out, sems)

    hbm = pl.BlockSpec(memory_space=pl.ANY)
    call = pl.pallas_call(
        carrying, name=name, grid=grid, in_specs=list(in_specs) + [hbm] * n_rin, out_specs=o_specs + [hbm] * n_rout,
        out_shape=o_shapes + list(ride["outs"]), input_output_aliases={n_in + i: n_out + o for i, o in ride["alias"].items()},
        scratch_shapes=list(scratch_shapes) + [pltpu.SemaphoreType.DMA((ride["nsem"],)), pltpu.SemaphoreType.DMA((ride["nsem"],))],
        compiler_params=_cparams(("arbitrary",) * len(grid)))

    def run(*args):
        res = call(*args, *ride["ins"])
        ride["result"] = list(res[n_out:])
        return tuple(res[:n_out]) if multi else res[0]

    return run


def _sigmoid(x):
    return 1.0 / (1.0 + jnp.exp(-x))


def _silu(x):
    return x * _sigmoid(x)


def _silu_pair(x):
    s = _sigmoid(x)
    return x * s, s * (1.0 + x * (1.0 - s))


MM_VMEM_BUDGET = 40 * 1024 * 1024
TILE_M = (2048, 1408, 1088, 1024, 544, 512, 256, 128)
TILE_N = (2048, 1536, 1408, 1024, 768, 512, 256, 128)
TILE_K = (2176, 2048, 1408, 1088, 1024, 768, 512)


def _mm_tiles(m_unit, n_unit, k_unit, out_bytes):
    best = None
    for tm in [c for c in TILE_M if m_unit % c == 0] or [m_unit]:
        for tn in [c for c in TILE_N if n_unit % c == 0] or [n_unit]:
            for tk in [c for c in TILE_K if k_unit % c == 0] or [k_unit]:
                vmem = 4 * tk * (tm + tn) + tm * tn * (2 * out_bytes + 4)
                if vmem > MM_VMEM_BUDGET:
                    continue
                key = (tm * tn / (tm + tn), tk)
                if best is None or key > best[0]:
                    best = (key, (tm, tn, tk))
    assert best is not None, (m_unit, n_unit, k_unit)
    return best[1]


def _mm(a, b, *, ta=False, tb=False, out_dtype=F32, name, a_split=False, b_chip=None, out_chip=None, extra=None, ride=None):
    if a_split:
        assert not ta
        M, K = a.shape[1], a.shape[0] * a.shape[2]
    elif ta:
        K, M = a.shape
    else:
        M, K = a.shape
    bs = list(b.shape) if b_chip is None else list(b.shape[1:])
    if b_chip is not None:
        bs[b_chip] *= b.shape[0]
    N, K2 = bs if tb else bs[::-1]
    assert K == K2, (a.shape, b.shape, ta, tb, b_chip)
    m_unit, n_unit, k_unit = M, N, K
    if a_split:
        k_unit = a.shape[2]
    if b_chip is not None:
        if (b_chip == 0) == tb:
            n_unit = N // b.shape[0]
        else:
            k_unit = min(k_unit, K // b.shape[0])
    if out_chip == 0:
        m_unit = M // 4
    elif out_chip == 1:
        n_unit = min(n_unit, N // 4)
    tm, tn, tk = _mm_tiles(m_unit, n_unit, k_unit, jnp.dtype(out_dtype).itemsize)
    nk = K // tk
    dims = TN if ta else (NT if tb else NN)

    def body(*refs):
        a_ref, b_ref, o_ref, acc_ref = refs[0], refs[1], refs[-2], refs[-1]
        k = pl.program_id(2)

        @pl.when(k == 0)
        def _():
            if extra is None:
                acc_ref[...] = jnp.zeros_like(acc_ref)
            else:
                acc_ref[...] = _dg(refs[2][...], refs[3][...], dims)

        acc_ref[...] += _dg(a_ref[...], b_ref[...], dims)

        @pl.when(k == nk - 1)
        def _():
            o_ref[...] = acc_ref[...].astype(o_ref.dtype)

    def b_index(n, m, k):
        i0, i1 = (n, k) if tb else (k, n)
        if b_chip is None:
            return (i0, i1)
        if b_chip == 0:
            nb = b.shape[1] // b_block[0]
            return (i0 // nb, i0 % nb, i1)
        nb = b.shape[2] // b_block[1]
        return (i1 // nb, i0, i1 % nb)

    def o_index(n, m, k):
        if out_chip is None:
            return (m, n)
        if out_chip == 0:
            mb = m_unit // tm
            return (m // mb, m % mb, n)
        nb = n_unit // tn
        return (n // nb, m, n % nb)

    b_block = (tn, tk) if tb else (tk, tn)
    lead = lambda blk, on: ((None,) + blk) if on else blk
    if a_split:
        kb = a.shape[2] // tk
        a_spec = pl.BlockSpec((None, tm, tk), lambda n, m, k: (k // kb, m, k % kb))
    elif ta:
        a_spec = pl.BlockSpec((tk, tm), lambda n, m, k: (k, m))
    else:
        a_spec = pl.BlockSpec((tm, tk), lambda n, m, k: (m, k))
    in_specs = [a_spec, pl.BlockSpec(lead(b_block, b_chip is not None), b_index)]
    args = [a, b]
    if extra is not None:
        assert not ta
        a2, b2 = extra
        E = a2.shape[1]
        in_specs += [pl.BlockSpec((tm, E), lambda n, m, k: (m, 0)),
                     pl.BlockSpec((tn, E), lambda n, m, k: (n, 0)) if tb else pl.BlockSpec((E, tn), lambda n, m, k: (0, n))]
        args += [a2, b2]
    out_full = (M, N) if out_chip is None else ((4, M // 4, N) if out_chip == 0 else (4, M, N // 4))
    return _pcall(
        body, name=name, grid=(N // tn, M // tm, nk), in_specs=in_specs,
        out_specs=pl.BlockSpec(lead((tm, tn), out_chip is not None), o_index),
        out_shape=jax.ShapeDtypeStruct(out_full, out_dtype), scratch_shapes=[pltpu.VMEM((tm, tn), F32)],
        sem=("parallel", "parallel", "arbitrary"), ride=ride,
    )(*args)


def _seg_spec(D, first_lat):
    return pl.BlockSpec((1, 1, D), lambda i: (jnp.where(i >= first_lat, 1, 0), 0, 0))


def _norm_fwd(x, y, gate, g, shift, scale, *, M, name):
    T, D = x.shape
    tm = _pick(T, (256,))
    first_lat = M // tm
    has_res = y is not None
    seg = _seg_spec(D, first_lat)
    row = pl.BlockSpec((tm, D), lambda i: (i, 0))

    def body(*refs):
        if has_res:
            x_ref, y_ref, gate_ref, g_ref, sh_ref, sc_ref, xo_ref, h_ref = refs
            xv = x_ref[...] + gate_ref[0] * y_ref[...].astype(F32)
            xo_ref[...] = xv
        else:
            x_ref, g_ref, sh_ref, sc_ref, h_ref = refs
            xv = x_ref[...]
        rstd = lax.rsqrt(jnp.mean(xv * xv, axis=-1, keepdims=True) + EPS)
        h = xv * rstd * g_ref[...] * (1.0 + sc_ref[0]) + sh_ref[0]
        h_ref[...] = h.astype(h_ref.dtype)

    gspec = pl.BlockSpec((1, D), lambda i: (0, 0))
    if has_res:
        ins = [x, y, gate, g, shift, scale]
        in_specs = [row, row, seg, gspec, seg, seg]
        out_shape = (jax.ShapeDtypeStruct((T, D), F32), jax.ShapeDtypeStruct((T, D), CDT))
        out_specs = (row, row)
    else:
        ins = [x, g, shift, scale]
        in_specs = [row, gspec, seg, seg]
        out_shape = jax.ShapeDtypeStruct((T, D), CDT)
        out_specs = row
    out = pl.pallas_call(
        body, name=name, grid=(T // tm,), in_specs=in_specs, out_specs=out_specs, out_shape=out_shape,
        compiler_params=_cparams(("parallel",)),
    )(*ins)
    return out if has_res else (x, out)


def _norm_bwd(x, dh, dx_in, g, scale, y_prev, gate_prev, *, M, name):
    T, D = x.shape
    tm = _pick(T, (256,))
    first_lat = M // tm
    has_prev = y_prev is not None
    seg = _seg_spec(D, first_lat)
    row = pl.BlockSpec((tm, D), lambda i: (i, 0))
    gspec = pl.BlockSpec((1, D), lambda i: (0, 0))

    def body(*refs):
        if has_prev:
            x_ref, dh_ref, dxi_ref, g_ref, sc_ref, yp_ref, gp_ref, dx_ref, dy_ref, acc_ref = refs
        else:
            x_ref, dh_ref, dxi_ref, g_ref, sc_ref, dx_ref, acc_ref = refs
        i = pl.program_id(0)

        @pl.when(jnp.logical_or(i == 0, i == first_lat))
        def _():
            acc_ref[...] = jnp.zeros_like(acc_ref)

        xv = x_ref[...]
        rstd = lax.rsqrt(jnp.mean(xv * xv, axis=-1, keepdims=True) + EPS)
        xn = xv * rstd
        dh = dh_ref[...].astype(F32)
        dxn = dh * (g_ref[...] * (1.0 + sc_ref[0]))
        dx = dxi_ref[...] + rstd * (dxn - xn * jnp.mean(dxn * xn, axis=-1, keepdims=True))
        dx_ref[...] = dx
        acc_ref[0, 0:1, :] += jnp.sum(dh, axis=0, keepdims=True)
        acc_ref[0, 1:2, :] += jnp.sum(dh * xn, axis=0, keepdims=True)
        if has_prev:
            dy_ref[...] = (dx * gp_ref[0]).astype(dy_ref.dtype)
            acc_ref[0, 2:3, :] += jnp.sum(dx * yp_ref[...].astype(F32), axis=0, keepdims=True)

    acc_spec = pl.BlockSpec((1, 8, D), lambda i: (jnp.where(i >= first_lat, 1, 0), 0, 0))
    acc_shape = jax.ShapeDtypeStruct((2, 8, D), F32)
    if has_prev:
        ins = [x, dh, dx_in, g, scale, y_prev, gate_prev]
        in_specs = [row, row, row, gspec, seg, row, seg]
        out_shape = (jax.ShapeDtypeStruct((T, D), F32), jax.ShapeDtypeStruct((T, D), CDT), acc_shape)
        out_specs = (row, row, acc_spec)
    else:
        ins = [x, dh, dx_in, g, scale]
        in_specs = [row, row, row, gspec, seg]
        out_shape = (jax.ShapeDtypeStruct((T, D), F32), acc_shape)
        out_specs = (row, acc_spec)
    out = pl.pallas_call(
        body, name=name, grid=(T // tm,), in_specs=in_specs, out_specs=out_specs, out_shape=out_shape,
        compiler_params=_cparams(("arbitrary",)),
    )(*ins)
    if has_prev:
        return out
    return out[0], None, out[1]


def _final_loss(x, y_prev, gate_prev, tgt, g, *, M, name):
    T, D = x.shape
    tm = _pick(T, (256,))
    first_lat = M // tm
    nt = T // tm
    seg = _seg_spec(D, first_lat)
    row = pl.BlockSpec((tm, D), lambda i: (i, 0))
    gspec = pl.BlockSpec((1, D), lambda i: (0, 0))
    tspec = pl.BlockSpec((tm, D), lambda i: (jnp.maximum(i - first_lat, 0), 0))

    def body(x_ref, yp_ref, gp_ref, t_ref, g_ref, loss_ref, dx_ref, dy_ref, acc_ref):
        i = pl.program_id(0)

        @pl.when(jnp.logical_or(i == 0, i == first_lat))
        def _():
            acc_ref[...] = jnp.zeros_like(acc_ref)

        lat = jnp.where(i >= first_lat, 1.0, 0.0)
        yp = yp_ref[...].astype(F32)
        xv = x_ref[...] + gp_ref[0] * yp
        rstd = lax.rsqrt(jnp.mean(xv * xv, axis=-1, keepdims=True) + EPS)
        xn = xv * rstd
        diff = (xn * g_ref[...] - t_ref[...]) * lat
        part = 0.5 * jnp.sum(jnp.sum(diff * diff, axis=-1, keepdims=True), axis=0, keepdims=True) * (1.0 / D)
        loss_ref[0] = jnp.broadcast_to(part, (8, 128))
        dyv = diff * (1.0 / D)
        dxn = dyv * g_ref[...]
        dx = rstd * (dxn - xn * jnp.mean(dxn * xn, axis=-1, keepdims=True))
        dx_ref[...] = dx
        dy_ref[...] = (dx * gp_ref[0]).astype(dy_ref.dtype)
        acc_ref[0, 0:1, :] += jnp.sum(dyv * xn, axis=0, keepdims=True)
        acc_ref[0, 2:3, :] += jnp.sum(dx * yp, axis=0, keepdims=True)

    return pl.pallas_call(
        body, name=name, grid=(nt,),
        in_specs=[row, row, seg, tspec, gspec],
        out_specs=(pl.BlockSpec((1, 8, 128), lambda i: (i, 0, 0)), row, row,
                   pl.BlockSpec((1, 8, D), lambda i: (jnp.where(i >= first_lat, 1, 0), 0, 0))),
        out_shape=(jax.ShapeDtypeStruct((nt, 8, 128), F32), jax.ShapeDtypeStruct((T, D), F32),
                   jax.ShapeDtypeStruct((T, D), CDT), jax.ShapeDtypeStruct((2, 8, D), F32)),
        compiler_params=_cparams(("arbitrary",)),
    )(x, y_prev, gate_prev, tgt, g)


HALO = 16
CONV_TC = (1408, 512)


def _taps(uc, prev16, next16, keep_prev, keep_next):
    tm = uc.shape[0]
    u = uc.astype(F32)
    rows = lax.broadcasted_iota(jnp.int32, u.shape, 0)
    pr = prev16[HALO - 1:HALO, :].astype(F32) * keep_prev
    nx = next16[0:1, :].astype(F32) * keep_next
    um = jnp.where(rows == 0, pr, pltpu.roll(u, 1, 0))
    up = jnp.where(rows == tm - 1, nx, pltpu.roll(u, tm - 1, 0))
    return um, u, up


def _conv3(uc, prev16, next16, w, bias, keep_prev, keep_next):
    um, u, up = _taps(uc, prev16, next16, keep_prev, keep_next)
    out = w[0:1, :] * um + w[1:2, :] * u + w[2:3, :] * up
    return out if bias is None else out + bias


def _conv_specs(tm, tc, T, col):
    hb = tm // HALO
    last = T // HALO - 1
    return [
        pl.BlockSpec((tm, tc), lambda j, i: (i, col(j))),
        pl.BlockSpec((HALO, tc), lambda j, i: (jnp.maximum(i * hb - 1, 0), col(j))),
        pl.BlockSpec((HALO, tc), lambda j, i: (jnp.minimum((i + 1) * hb, last), col(j))),
    ]


def _seg_keep(i, first_lat, nt):
    keep_prev = jnp.where(jnp.logical_or(i == 0, i == first_lat), 0.0, 1.0)
    keep_next = jnp.where(jnp.logical_or(i == first_lat - 1, i == nt - 1), 0.0, 1.0)
    return keep_prev, keep_next


def _conv_gate_fwd(u, cw, cb, *, M, name, ride=None):
    T, F2 = u.shape
    Fh = F2 // 2
    tm = _pick(T, (256,))
    tc = _pick(Fh, CONV_TC)
    nf = Fh // tc
    nt = T // tm
    first_lat = M // tm

    def body(ug, ugp, ugn, uv, uvp, uvn, wg, wv, bg, bv, o_ref):
        kp, kn = _seg_keep(pl.program_id(1), first_lat, nt)
        gc = _conv3(ug[...], ugp[...], ugn[...], wg[...], bg[...], kp, kn)
        vc = _conv3(uv[...], uvp[...], uvn[...], wv[...], bv[...], kp, kn)
        o_ref[...] = (_silu(gc) * vc).astype(o_ref.dtype)

    wspec = lambda off: pl.BlockSpec((3, tc), lambda j, i: (0, j + off))
    bspec = lambda off: pl.BlockSpec((1, tc), lambda j, i: (0, j + off))
    return _pcall(
        body, name=name, grid=(nf, nt),
        in_specs=_conv_specs(tm, tc, T, lambda j: j) + _conv_specs(tm, tc, T, lambda j: j + nf)
        + [wspec(0), wspec(nf), bspec(0), bspec(nf)],
        out_specs=pl.BlockSpec((tm, tc), lambda j, i: (i, j)),
        out_shape=jax.ShapeDtypeStruct((T, Fh), CDT), sem=("parallel", "parallel"), ride=ride,
    )(u, u, u, u, u, u, cw, cw, cb, cb)


def _conv_gate_bwd(u, dact, cw, cb, *, M, name):
    T, F2 = u.shape
    Fh = F2 // 2
    tm = _pick(T, (256,))
    tc = _pick(Fh, (512,))
    nf = Fh // tc
    nt = T // tm
    first_lat = M // tm

    def body(ug, ugp, ugn, uv, uvp, uvn, da, wg, wv, bg, bv, d_ref, acc_ref):
        i = pl.program_id(1)

        @pl.when(i == 0)
        def _():
            acc_ref[...] = jnp.zeros_like(acc_ref)

        kp, kn = _seg_keep(i, first_lat, nt)
        tg = _taps(ug[...], ugp[...], ugn[...], kp, kn)
        tv = _taps(uv[...], uvp[...], uvn[...], kp, kn)
        w = wg[...]
        gc = w[0:1, :] * tg[0] + w[1:2, :] * tg[1] + w[2:3, :] * tg[2] + bg[...]
        w = wv[...]
        vc = w[0:1, :] * tv[0] + w[1:2, :] * tv[1] + w[2:3, :] * tv[2] + bv[...]
        dav = da[...].astype(F32)
        act, dact_dg = _silu_pair(gc)
        for half, d, taps in ((0, dav * vc * dact_dg, tg), (1, dav * act, tv)):
            d_ref[half] = d.astype(d_ref.dtype)
            acc_ref[half, 0:1, :] += jnp.sum(d * taps[0], axis=0, keepdims=True)
            acc_ref[half, 1:2, :] += jnp.sum(d * taps[1], axis=0, keepdims=True)
            acc_ref[half, 2:3, :] += jnp.sum(d * taps[2], axis=0, keepdims=True)
            acc_ref[half, 3:4, :] += jnp.sum(d, axis=0, keepdims=True)

    wspec = lambda off: pl.BlockSpec((3, tc), lambda j, i: (0, j + off))
    bspec = lambda off: pl.BlockSpec((1, tc), lambda j, i: (0, j + off))
    return pl.pallas_call(
        body, name=name, grid=(nf, nt),
        in_specs=_conv_specs(tm, tc, T, lambda j: j) + _conv_specs(tm, tc, T, lambda j: j + nf)
        + [pl.BlockSpec((tm, tc), lambda j, i: (i, j)), wspec(0), wspec(nf), bspec(0), bspec(nf)],
        out_specs=(pl.BlockSpec((2, tm, tc), lambda j, i: (0, i, j)), pl.BlockSpec((2, 8, tc), lambda j, i: (0, 0, j))),
        out_shape=(jax.ShapeDtypeStruct((2, T, Fh), CDT), jax.ShapeDtypeStruct((2, 8, Fh), F32)),
        compiler_params=_cparams(("parallel", "arbitrary")),
    )(u, u, u, u, u, u, dact, cw, cw, cb, cb)


def _conv_t(d, cw, *, M, name):
    _, T, Fh = d.shape
    tm = _pick(T, (256,))
    tc = _pick(Fh, CONV_TC)
    nf = Fh // tc
    nt = T // tm
    first_lat = M // tm
    hb = tm // HALO
    last = T // HALO - 1

    def body(dc, dp, dn, w, o_ref):
        kp, kn = _seg_keep(pl.program_id(2), first_lat, nt)
        dm, d0, dp1 = _taps(dc[...], dp[...], dn[...], kp, kn)
        wv = w[...]
        o_ref[...] = (wv[2:3, :] * dm + wv[1:2, :] * d0 + wv[0:1, :] * dp1).astype(o_ref.dtype)

    return pl.pallas_call(
        body, name=name, grid=(2, nf, nt),
        in_specs=[pl.BlockSpec((None, tm, tc), lambda g, j, i: (g, i, j)),
                  pl.BlockSpec((None, HALO, tc), lambda g, j, i: (g, jnp.maximum(i * hb - 1, 0), j)),
                  pl.BlockSpec((None, HALO, tc), lambda g, j, i: (g, jnp.minimum((i + 1) * hb, last), j)),
                  pl.BlockSpec((3, tc), lambda g, j, i: (0, g * nf + j))],
        out_specs=pl.BlockSpec((None, tm, tc), lambda g, j, i: (g, i, j)),
        out_shape=jax.ShapeDtypeStruct((2, T, Fh), CDT),
        compiler_params=_cparams(("parallel", "parallel", "parallel")),
    )(d, d, d, cw)


def _rope_tables(N, M, HD):
    ax = HD // 2
    pos = jnp.arange(N, dtype=jnp.int32)
    row = (pos // GRID_W).astype(F32)
    col = (pos % GRID_W).astype(F32)
    inv = ROPE_BASE ** (-jnp.arange(0, ax, 2, dtype=F32) / ax)
    ar = row[:, None] * inv[None, :]
    ac = col[:, None] * inv[None, :]
    cos = jnp.concatenate([jnp.cos(ar), jnp.cos(ar), jnp.cos(ac), jnp.cos(ac)], axis=1)
    sin = jnp.concatenate([-jnp.sin(ar), jnp.sin(ar), -jnp.sin(ac), jnp.sin(ac)], axis=1)
    cos = jnp.concatenate([jnp.ones((M, HD), F32), cos], axis=0)
    sin = jnp.concatenate([jnp.zeros((M, HD), F32), sin], axis=0)
    return cos, sin


def _pair_swap(x, nf):
    w = x.shape[1]
    lane = lax.broadcasted_iota(jnp.int32, x.shape, 1)
    first = (lane % (2 * nf)) < nf
    return jnp.where(first, pltpu.roll(x, w - nf, 1), pltpu.roll(x, nf, 1))


def _rope_fwd(qkv, cos, sin, *, QW, KW, HD, name):
    T = qkv.shape[0]
    tm = _pick(T, (256,))
    nf = HD // 4

    def body(qkv_ref, c_ref, s_ref, q_ref, k_ref, v_ref):
        c = c_ref[...]
        s = s_ref[...]
        for ref, off, w in ((q_ref, 0, QW), (k_ref, QW, KW)):
            xv = qkv_ref[:, off:off + w]
            ct = jnp.tile(c, (1, w // HD))
            st = jnp.tile(s, (1, w // HD))
            ref[...] = (xv * ct + _pair_swap(xv, nf) * st).astype(ref.dtype)
        v_ref[...] = qkv_ref[:, QW + KW:QW + 2 * KW].astype(v_ref.dtype)

    tspec = pl.BlockSpec((tm, HD), lambda i: (i, 0))
    return pl.pallas_call(
        body, name=name, grid=(T // tm,),
        in_specs=[pl.BlockSpec((tm, QW + 2 * KW), lambda i: (i, 0)), tspec, tspec],
        out_specs=(pl.BlockSpec((tm, QW), lambda i: (i, 0)), pl.BlockSpec((tm, KW), lambda i: (i, 0)),
                   pl.BlockSpec((tm, KW), lambda i: (i, 0))),
        out_shape=(jax.ShapeDtypeStruct((T, QW), CDT), jax.ShapeDtypeStruct((T, KW), CDT),
                   jax.ShapeDtypeStruct((T, KW), CDT)),
        compiler_params=_cparams(("parallel",)),
    )(qkv, cos, sin)


def _rope_bwd(dq, dk, dv, cos, sin, *, HD, name):
    T, QW = dq.shape
    KW = dk.shape[1]
    tm = _pick(T, (256,))
    nf = HD // 4

    def body(dq_ref, dk_ref, dv_ref, c_ref, s_ref, o_ref):
        c = c_ref[...]
        s = s_ref[...]
        for ref, off, w in ((dq_ref, 0, QW), (dk_ref, QW, KW)):
            g = ref[...].astype(F32)
            ct = jnp.tile(c, (1, w // HD))
            st = jnp.tile(s, (1, w // HD))
            o_ref[:, off:off + w] = (g * ct + _pair_swap(g * st, nf)).astype(o_ref.dtype)
        o_ref[:, QW + KW:QW + 2 * KW] = dv_ref[...].astype(o_ref.dtype)

    tspec = pl.BlockSpec((tm, HD), lambda i: (i, 0))
    return pl.pallas_call(
        body, name=name, grid=(T // tm,),
        in_specs=[pl.BlockSpec((tm, QW), lambda i: (i, 0)), pl.BlockSpec((tm, KW), lambda i: (i, 0)),
                  pl.BlockSpec((tm, KW), lambda i: (i, 0)), tspec, tspec],
        out_specs=pl.BlockSpec((tm, QW + 2 * KW), lambda i: (i, 0)),
        out_shape=jax.ShapeDtypeStruct((T, QW + 2 * KW), CDT),
        compiler_params=_cparams(("parallel",)),
    )(dq, dk, dv, cos, sin)


def _attn_scores(q_ref, kc_ref, kp_ref, kn_ref, kx_ref, sink_ref, i, *, M, HD, nblk, nbc):
    qs = jnp.concatenate([q_ref[:, g * HD:(g + 1) * HD] for g in range(ATT_G)], axis=0)
    kall = jnp.concatenate([kc_ref[...], kp_ref[...], kn_ref[...], kx_ref[...]], axis=0)
    s = _dg(qs, kall, NT) * (HD ** -0.5)
    shape = s.shape
    r = lax.broadcasted_iota(jnp.int32, shape, 0) % BLK
    c = lax.broadcasted_iota(jnp.int32, shape, 1) - M
    far = 4 * BLK
    lat_off = jnp.where(i >= nbc, 0, far)
    lo = jnp.maximum(r, jnp.where(i - 1 >= nbc, 0, BLK)) + lat_off
    hi = jnp.minimum(r + 2 * BLK, jnp.where(i + 1 < nblk, 3 * BLK - 1, 2 * BLK - 1))
    allowed = jnp.logical_or(c < 0, jnp.logical_and(c >= lo, c <= hi))
    s = jnp.where(allowed, s, -1e30)
    sink = sink_ref[0]
    m = jnp.maximum(jnp.max(s, axis=-1, keepdims=True), sink)
    e = jnp.exp(s - m)
    es = jnp.exp(sink - m)
    inv = 1.0 / (jnp.sum(e, axis=-1, keepdims=True) + es)
    return qs, kall, e * inv, es * inv


def _attn_specs(M, HD, nblk):
    kv_blk = lambda f: pl.BlockSpec((BLK, HD), lambda h, i: (f(i), h))
    ctx = pl.BlockSpec((M, HD), lambda h, i: (0, h))
    win = [kv_blk(lambda i: jnp.maximum(i - 1, 0)), kv_blk(lambda i: i), kv_blk(lambda i: jnp.minimum(i + 1, nblk - 1))]
    qspec = pl.BlockSpec((BLK, ATT_G * HD), lambda h, i: (i, h))
    sspec = pl.BlockSpec((1, ATT_G * BLK, 1), lambda h, i: (h, 0, 0))
    return qspec, [ctx] + win, sspec


def _attn_fwd(q, k, v, sink_col, *, M, name, ride=None):
    T, QW = q.shape
    HD = QW // ATT_HEADS
    nblk = T // BLK
    nbc = M // BLK

    def body(q_ref, kc, kp, kn, kx, vc, vp, vn, vx, sink_ref, o_ref):
        i = pl.program_id(1)
        _, _, p, _ = _attn_scores(q_ref, kc, kp, kn, kx, sink_ref, i, M=M, HD=HD, nblk=nblk, nbc=nbc)
        vall = jnp.concatenate([vc[...], vp[...], vn[...], vx[...]], axis=0)
        o = _dg(p.astype(CDT), vall, NN)
        for g in range(ATT_G):
            o_ref[:, g * HD:(g + 1) * HD] = o[g * BLK:(g + 1) * BLK, :].astype(o_ref.dtype)

    qspec, kvs, sspec = _attn_specs(M, HD, nblk)
    return _pcall(
        body, name=name, grid=(ATT_KV, nblk), in_specs=[qspec] + kvs + kvs + [sspec], out_specs=qspec,
        out_shape=jax.ShapeDtypeStruct((T, QW), CDT), sem=("parallel", "parallel"), ride=ride,
    )(q, k, k, k, k, v, v, v, v, sink_col)


def _attn_bwd(q, k, v, sink_col, do, *, M, name, ride=None):
    T, QW = q.shape
    HD = QW // ATT_HEADS
    KW = ATT_KV * HD
    nblk = T // BLK
    nbc = M // BLK

    def body(q_ref, kc, kp, kn, kx, vc, vp, vn, vx, sink_ref, do_ref, dq_ref, dkc_ref, dvc_ref, dkw_ref, dvw_ref, ds_ref):
        i = pl.program_id(1)

        @pl.when(i == 0)
        def _():
            dkc_ref[...] = jnp.zeros_like(dkc_ref)
            dvc_ref[...] = jnp.zeros_like(dvc_ref)
            ds_ref[...] = jnp.zeros_like(ds_ref)

        qs, kall, p, p_sink = _attn_scores(q_ref, kc, kp, kn, kx, sink_ref, i, M=M, HD=HD, nblk=nblk, nbc=nbc)
        vall = jnp.concatenate([vc[...], vp[...], vn[...], vx[...]], axis=0)
        dos = jnp.concatenate([do_ref[:, g * HD:(g + 1) * HD] for g in range(ATT_G)], axis=0)
        dp = _dg(dos, vall, NT)
        dsum = jnp.sum(p * dp, axis=-1, keepdims=True)
        dsc = (p * (dp - dsum) * (HD ** -0.5)).astype(CDT)
        dq = _dg(dsc, kall, NN)
        dkall = _dg(dsc, qs, TN)
        dvall = _dg(p.astype(CDT), dos, TN)
        for g in range(ATT_G):
            dq_ref[:, g * HD:(g + 1) * HD] = dq[g * BLK:(g + 1) * BLK, :].astype(dq_ref.dtype)
        dkc_ref[...] += dkall[0:M]
        dvc_ref[...] += dvall[0:M]
        dkw_ref[...] = dkall[M:]
        dvw_ref[...] = dvall[M:]
        ds_ref[0] += -(p_sink * dsum)

    qspec, kvs, sspec = _attn_specs(M, HD, nblk)
    ctx_out = pl.BlockSpec((M, HD), lambda h, i: (0, h))
    win_out = pl.BlockSpec((3 * BLK, HD), lambda h, i: (i, h))
    return _pcall(
        body, name=name, grid=(ATT_KV, nblk),
        in_specs=[qspec] + kvs + kvs + [sspec, qspec],
        out_specs=(qspec, ctx_out, ctx_out, win_out, win_out, sspec),
        out_shape=(jax.ShapeDtypeStruct((T, QW), CDT), jax.ShapeDtypeStruct((M, KW), F32), jax.ShapeDtypeStruct((M, KW), F32),
                   jax.ShapeDtypeStruct((nblk * 3 * BLK, KW), F32), jax.ShapeDtypeStruct((nblk * 3 * BLK, KW), F32),
                   jax.ShapeDtypeStruct((ATT_KV, ATT_G * BLK, 1), F32)),
        sem=("parallel", "arbitrary"), ride=ride,
    )(q, k, k, k, k, v, v, v, v, sink_col, do)


def _window_combine(part, *, nbc, name):
    rows, KW = part.shape
    nblk = rows // (3 * BLK)
    nbl = nblk - nbc

    def body(a_ref, b_ref, c_ref, o_ref):
        j = pl.program_id(0)
        o_ref[...] = (a_ref[...] * jnp.where(j + 1 < nbl, 1.0, 0.0) + b_ref[...]
                      + c_ref[...] * jnp.where(j >= 1, 1.0, 0.0))

    return pl.pallas_call(
        body, name=name, grid=(nbl,),
        in_specs=[pl.BlockSpec((BLK, KW), lambda j: (3 * jnp.minimum(nbc + j + 1, nblk - 1), 0)),
                  pl.BlockSpec((BLK, KW), lambda j: (3 * (nbc + j) + 1, 0)),
                  pl.BlockSpec((BLK, KW), lambda j: (3 * jnp.maximum(nbc + j - 1, 0) + 2, 0))],
        out_specs=pl.BlockSpec((BLK, KW), lambda j: (j, 0)),
        out_shape=jax.ShapeDtypeStruct((nbl * BLK, KW), F32),
        compiler_params=_cparams(("parallel",)),
    )(part, part, part)


def _split3(x):
    hi = x.astype(CDT)
    r1 = x - hi.astype(F32)
    mid = r1.astype(CDT)
    lo = (r1 - mid.astype(F32)).astype(CDT)
    return hi, mid, lo


def _tri_sum(tri, x, terms):
    parts = _split3(x)[:terms]
    out = _dg(tri, parts[0], NN)
    for p in parts[1:]:
        out = out + _dg(tri, p, NN)
    return out


def _gla_dims(D):
    dk = D // 2 // GLA_H
    dv = D // GLA_H
    return dk, dv


def _chunk_of(s, rev, ncc, ns):
    if not rev:
        return s
    return jnp.where(s < ncc, ncc - 1 - s, ns - 1 - (s - ncc))


def _gla_chunk(q, k, g, rev, dk):
    C = q.shape[0]
    r = lax.broadcasted_iota(jnp.int32, (C, C), 0)
    c = lax.broadcasted_iota(jnp.int32, (C, C), 1)
    causal = (r <= c) if rev else (r >= c)
    b = _tri_sum(causal.astype(CDT), g, 3)
    B = b[0:1, :] if rev else b[C - 1:C, :]
    q = q.astype(F32) * (dk ** -0.5)
    k = k.astype(F32)
    return causal, b, B, q * jnp.exp(b), k * jnp.exp(-b), k * jnp.exp(B - b)


def _gla_scan_fwd(proj, g, *, rev, M, D, name):
    T = proj.shape[0]
    dk, dv = _gla_dims(D)
    C, H, Dh = CHUNK, GLA_H, D // 2
    ns = T // C
    ncc = M // C
    cm = lambda s: _chunk_of(s, rev, ncc, ns)

    def body(q_ref, k_ref, v_ref, g_ref, o_ref, st_ref, S):
        @pl.when(pl.program_id(0) == 0)
        def _():
            S[...] = jnp.zeros_like(S)

        for h in range(H):
            ks, vs = slice(h * dk, (h + 1) * dk), slice(h * dv, (h + 1) * dv)
            causal, b, B, qt, kt, kh = _gla_chunk(q_ref[:, ks], k_ref[:, ks], g_ref[:, ks], rev, dk)
            v = v_ref[:, vs]
            A = jnp.where(causal, _dg(qt.astype(CDT), kt.astype(CDT), NT), 0.0)
            Sin = S[h]
            st_ref[h] = Sin
            o_ref[:, vs] = _dg(A.astype(CDT), v, NN) + _dg(qt.astype(CDT), Sin.astype(CDT), NT)
            S[h] = Sin * jnp.exp(B) + _dg(v, kh.astype(CDT), TN)

    return pl.pallas_call(
        body, name=name, grid=(ns,),
        in_specs=[pl.BlockSpec((C, Dh), lambda s: (cm(s), 0)), pl.BlockSpec((C, Dh), lambda s: (cm(s), 1)),
                  pl.BlockSpec((C, D), lambda s: (cm(s), 1)), pl.BlockSpec((C, Dh), lambda s: (cm(s), 0))],
        out_specs=(pl.BlockSpec((C, D), lambda s: (cm(s), 0)), pl.BlockSpec((H, dv, dk), lambda s: (s, 0, 0))),
        out_shape=(jax.ShapeDtypeStruct((T, D), F32), jax.ShapeDtypeStruct((ns * H, dv, dk), F32)),
        scratch_shapes=[pltpu.VMEM((H, dv, dk), F32)],
        compiler_params=_cparams(("arbitrary",)),
    )(proj, proj, proj, g)


def _gla_scan_bwd(proj, g, st, do, *, rev, M, D, name):
    T = proj.shape[0]
    dk, dv = _gla_dims(D)
    C, H, Dh = CHUNK, GLA_H, D // 2
    ns = T // C
    ncc = M // C
    cm = lambda j: _chunk_of(ns - 1 - j, rev, ncc, ns)

    def body(q_ref, k_ref, v_ref, g_ref, st_ref, do_ref, dq_ref, dk_ref, dv_ref, dg_ref, dS):
        @pl.when(pl.program_id(0) == 0)
        def _():
            dS[...] = jnp.zeros_like(dS)

        rows = lax.broadcasted_iota(jnp.int32, (C, dk), 0)
        eye = lax.broadcasted_iota(jnp.int32, (C, C), 0) == lax.broadcasted_iota(jnp.int32, (C, C), 1)
        for h in range(H):
            ks, vs = slice(h * dk, (h + 1) * dk), slice(h * dv, (h + 1) * dv)
            causal, b, B, qt, kt, kh = _gla_chunk(q_ref[:, ks], k_ref[:, ks], g_ref[:, ks], rev, dk)
            v = v_ref[:, vs]
            dov = do_ref[:, vs]
            ST = st_ref[h]
            dSo = dS[h]
            qtb, ktb, khb = qt.astype(CDT), kt.astype(CDT), kh.astype(CDT)
            dSb = dSo.astype(CDT)
            A = jnp.where(causal, _dg(qtb, ktb, NT), 0.0).astype(CDT)
            dA = jnp.where(causal, _dg(dov, v, NT), 0.0).astype(CDT)
            dqt = _dg(dA, ktb, NN) + _dg(dov, ST.astype(CDT), NN)
            dkt = _dg(dA, qtb, TN)
            dvv = _dg(A, dov, TN) + _dg(khb, dSb, NT)
            dkh = _dg(v, dSb, NN)
            eB = jnp.exp(B)
            dB = eB * jnp.sum(ST * dSo, axis=0, keepdims=True) + jnp.sum(dkh * kh, axis=0, keepdims=True)
            db = dqt * qt - dkt * kt - dkh * kh + jnp.where(rows == (0 if rev else C - 1), dB, 0.0)
            anti = jnp.logical_not(causal) | eye
            dg_ref[:, ks] = _tri_sum(anti.astype(CDT), db, 2)
            dq_ref[:, ks] = dqt * jnp.exp(b) * (dk ** -0.5)
            dk_ref[:, ks] = dkt * jnp.exp(-b) + dkh * jnp.exp(B - b)
            dv_ref[:, vs] = dvv
            dS[h] = dSo * eB + _dg(dov, qtb, TN)

    half = pl.BlockSpec((C, Dh), lambda j: (cm(j), 0))
    full = pl.BlockSpec((C, D), lambda j: (cm(j), 0))
    return pl.pallas_call(
        body, name=name, grid=(ns,),
        in_specs=[half, pl.BlockSpec((C, Dh), lambda j: (cm(j), 1)), pl.BlockSpec((C, D), lambda j: (cm(j), 1)), half,
                  pl.BlockSpec((H, dv, dk), lambda j: (ns - 1 - j, 0, 0)), full],
        out_specs=(half, half, full, half),
        out_shape=(jax.ShapeDtypeStruct((T, Dh), F32), jax.ShapeDtypeStruct((T, Dh), F32),
                   jax.ShapeDtypeStruct((T, D), F32), jax.ShapeDtypeStruct((T, Dh), F32)),
        scratch_shapes=[pltpu.VMEM((H, dv, dk), F32)],
        compiler_params=_cparams(("arbitrary",)),
    )(proj, proj, proj, g, st, do)


def _log_sigmoid_parts(z):
    t = jnp.exp(-jnp.abs(z))
    return jnp.minimum(z, 0.0) - jnp.log(1.0 + t), jnp.where(z >= 0, t / (1.0 + t), 1.0 / (1.0 + t))


def _gla_gate_fwd(lr, w2, bias, *, D, name):
    T = lr.shape[0]
    tm = _pick(T, (256,))
    Dh = D // 2

    def body(lr_ref, w_ref, b_ref, gf_ref, gb_ref):
        z = _dg(lr_ref[...], w_ref[...], NN) + b_ref[...]
        g, _ = _log_sigmoid_parts(z)
        g = g * (1.0 / GATE_NORM)
        gf_ref[...] = g[:, 0:Dh]
        gb_ref[...] = g[:, Dh:D]

    half = pl.BlockSpec((tm, Dh), lambda i: (i, 0))
    return pl.pallas_call(
        body, name=name, grid=(T // tm,),
        in_specs=[pl.BlockSpec((tm, 128), lambda i: (i, 0)), pl.BlockSpec((128, D), lambda i: (0, 0)),
                  pl.BlockSpec((1, D), lambda i: (0, 0))],
        out_specs=(half, half),
        out_shape=(jax.ShapeDtypeStruct((T, Dh), F32), jax.ShapeDtypeStruct((T, Dh), F32)),
        compiler_params=_cparams(("parallel",)),
    )(lr, w2, bias)


def _gla_proj_bwd(lr, w2, bias, dqf, dkf, dvf, dgf, dqb, dkb, dvb, dgb, dr, *, D, name):
    T = lr.shape[0]
    tm = _pick(T, (256,))
    Dh = D // 2

    def body(lr_ref, w_ref, b_ref, dqf_r, dkf_r, dvf_r, dgf_r, dqb_r, dkb_r, dvb_r, dgb_r, dr_ref, dp_ref, dl_ref, dw_ref, db_ref):
        @pl.when(pl.program_id(0) == 0)
        def _():
            dw_ref[...] = jnp.zeros_like(dw_ref)
            db_ref[...] = jnp.zeros_like(db_ref)

        lr = lr_ref[...]
        z = _dg(lr, w_ref[...], NN) + b_ref[...]
        _, sneg = _log_sigmoid_parts(z)
        dz = jnp.concatenate([dgf_r[...], dgb_r[...]], axis=1) * sneg * (1.0 / GATE_NORM)
        dzb = dz.astype(CDT)
        dp_ref[:, 0:Dh] = (dqf_r[...] + dqb_r[...]).astype(dp_ref.dtype)
        dp_ref[:, Dh:D] = (dkf_r[...] + dkb_r[...]).astype(dp_ref.dtype)
        dp_ref[:, D:2 * D] = (dvf_r[...] + dvb_r[...]).astype(dp_ref.dtype)
        dp_ref[:, 2 * D:3 * D] = dr_ref[...]
        dl_ref[...] = _dg(dzb, w_ref[...], NT).astype(dl_ref.dtype)
        dw_ref[...] += _dg(lr, dzb, TN)
        db_ref[0:1, :] += jnp.sum(dz, axis=0, keepdims=True)

    half = pl.BlockSpec((tm, Dh), lambda i: (i, 0))
    full = pl.BlockSpec((tm, D), lambda i: (i, 0))
    return pl.pallas_call(
        body, name=name, grid=(T // tm,),
        in_specs=[pl.BlockSpec((tm, 128), lambda i: (i, 0)), pl.BlockSpec((128, D), lambda i: (0, 0)),
                  pl.BlockSpec((1, D), lambda i: (0, 0)), half, half, full, half, half, half, full, half, full],
        out_specs=(pl.BlockSpec((tm, 3 * D), lambda i: (i, 0)), pl.BlockSpec((tm, 128), lambda i: (i, 0)),
                   pl.BlockSpec((128, D), lambda i: (0, 0)), pl.BlockSpec((8, D), lambda i: (0, 0))),
        out_shape=(jax.ShapeDtypeStruct((T, 3 * D), CDT), jax.ShapeDtypeStruct((T, 128), CDT),
                   jax.ShapeDtypeStruct((128, D), F32), jax.ShapeDtypeStruct((8, D), F32)),
        compiler_params=_cparams(("arbitrary",)),
    )(lr, w2, bias, dqf, dkf, dvf, dgf, dqb, dkb, dvb, dgb, dr)


def _gla_out_fwd(of, ob, proj, gn, *, D, name):
    T = of.shape[0]
    tm = _pick(T, (256,))
    dv = D // GLA_H

    def body(of_ref, ob_ref, r_ref, g_ref, y_ref):
        for h in range(GLA_H):
            sl = slice(h * dv, (h + 1) * dv)
            o = of_ref[:, sl] + ob_ref[:, sl]
            rstd = lax.rsqrt(jnp.mean(o * o, axis=-1, keepdims=True) + EPS)
            y_ref[:, sl] = (o * rstd * g_ref[...] * _silu(r_ref[:, sl].astype(F32))).astype(y_ref.dtype)

    full = pl.BlockSpec((tm, D), lambda i: (i, 0))
    return pl.pallas_call(
        body, name=name, grid=(T // tm,),
        in_specs=[full, full, pl.BlockSpec((tm, D), lambda i: (i, 2)), pl.BlockSpec((1, dv), lambda i: (0, 0))],
        out_specs=full, out_shape=jax.ShapeDtypeStruct((T, D), CDT),
        compiler_params=_cparams(("parallel",)),
    )(of, ob, proj, gn)


def _gla_out_bwd(of, ob, proj, gn, dy, *, D, name):
    T = of.shape[0]
    tm = _pick(T, (256,))
    dv = D // GLA_H

    def body(of_ref, ob_ref, r_ref, g_ref, dy_ref, do_ref, dr_ref, dg_ref):
        @pl.when(pl.program_id(0) == 0)
        def _():
            dg_ref[...] = jnp.zeros_like(dg_ref)

        gv = g_ref[...]
        for h in range(GLA_H):
            sl = slice(h * dv, (h + 1) * dv)
            o = of_ref[:, sl] + ob_ref[:, sl]
            rstd = lax.rsqrt(jnp.mean(o * o, axis=-1, keepdims=True) + EPS)
            oh = o * rstd
            r = r_ref[:, sl].astype(F32)
            dyv = dy_ref[:, sl].astype(F32)
            act, dact_dr = _silu_pair(r)
            don = dyv * act
            dr_ref[:, sl] = (dyv * oh * gv * dact_dr).astype(dr_ref.dtype)
            dg_ref[0:1, :] += jnp.sum(don * oh, axis=0, keepdims=True)
            dn = don * gv
            do_ref[:, sl] = (rstd * (dn - oh * jnp.mean(dn * oh, axis=-1, keepdims=True))).astype(do_ref.dtype)

    full = pl.BlockSpec((tm, D), lambda i: (i, 0))
    return pl.pallas_call(
        body, name=name, grid=(T // tm,),
        in_specs=[full, full, pl.BlockSpec((tm, D), lambda i: (i, 2)), pl.BlockSpec((1, dv), lambda i: (0, 0)), full],
        out_specs=(full, full, pl.BlockSpec((8, dv), lambda i: (0, 0))),
        out_shape=(jax.ShapeDtypeStruct((T, D), CDT), jax.ShapeDtypeStruct((T, D), CDT), jax.ShapeDtypeStruct((8, dv), F32)),
        compiler_params=_cparams(("arbitrary",)),
    )(of, ob, proj, gn, dy)


def _adamw(w, g, m, v, *, name, ride=None):
    R, Cc = w.shape
    tr = R
    for cand in (512, 256, 128, 64, 32, 16, 8):
        if R % cand == 0 and cand * Cc * 4 <= 2 * 1024 * 1024:
            tr = cand
            break

    def body(w_ref, g_ref, m_ref, v_ref, d_ref, mo_ref, vo_ref):
        gv = g_ref[...]
        mn = B1 * m_ref[...] + (1.0 - B1) * gv
        vn = B2 * v_ref[...] + (1.0 - B2) * (gv * gv)
        mh = mn / (1.0 - B1 ** STEP)
        vh = vn / (1.0 - B2 ** STEP)
        d_ref[...] = -LR * (mh / (jnp.sqrt(vh) + AEPS) + WD * w_ref[...])
        mo_ref[...] = mn
        vo_ref[...] = vn

    spec = pl.BlockSpec((tr, Cc), lambda i: (i, 0))
    sh = jax.ShapeDtypeStruct((R, Cc), F32)
    return _pcall(
        body, name=name, grid=(R // tr,), in_specs=[spec] * 4, out_specs=(spec,) * 3, out_shape=(sh,) * 3,
        sem=("parallel",), ride=ride,
    )(w, g, m, v)


def _attn_layer_fwd(h, w, tabs, M, tag, rides):
    cos, sin = tabs
    QW = 4 * w["w_o"].shape[1]
    HD = QW // ATT_HEADS
    KW = ATT_KV * HD
    qkv = _mm(h, w["w_qkv"], b_chip=1, out_dtype=F32, ride=rides.get("mix"), name=f"{tag}_qkv")
    q, k, v = _rope_fwd(qkv, cos, sin, QW=QW, KW=KW, HD=HD, name=f"{tag}_rope")
    sink_col = jnp.repeat(w["sink"].astype(F32), BLK).reshape(ATT_KV, ATT_G * BLK, 1)
    o = _attn_fwd(q, k, v, sink_col, M=M, ride=rides.get("attn"), name=f"{tag}_attn")
    y = _mm(o, w["w_o"], b_chip=0, out_dtype=CDT, name=f"{tag}_wo")
    return y, dict(h=h, q=q, k=k, v=v, o=o, sink_col=sink_col)


def _attn_layer_bwd(dy, sv, w, tabs, M, tag, ride=None):
    cos, sin = tabs
    QW = 4 * w["w_o"].shape[1]
    HD = QW // ATT_HEADS
    do = _mm(dy, w["w_o"], tb=True, b_chip=0, out_dtype=CDT, name=f"{tag}_dwo_x")
    g = {"w_o": _mm(sv["o"], dy, ta=True, out_chip=0, out_dtype=CDT, name=f"{tag}_dwo_w")}
    dq, dkc, dvc, dkw, dvw, dsink = _attn_bwd(sv["q"], sv["k"], sv["v"], sv["sink_col"], do, M=M, ride=ride, name=f"{tag}_attn_bwd")
    nbc = M // BLK
    dk = jnp.concatenate([dkc, _window_combine(dkw, nbc=nbc, name=f"{tag}_dk_comb")], axis=0)
    dv = jnp.concatenate([dvc, _window_combine(dvw, nbc=nbc, name=f"{tag}_dv_comb")], axis=0)
    dqkv = _rope_bwd(dq, dk, dv, cos, sin, HD=HD, name=f"{tag}_rope_bwd")
    dh = _mm(dqkv, w["w_qkv"], tb=True, b_chip=1, out_dtype=CDT, name=f"{tag}_dqkv_x")
    g["w_qkv"] = _mm(sv["h"], dqkv, ta=True, out_chip=1, out_dtype=CDT, name=f"{tag}_dqkv_w")
    g["sink"] = jnp.sum(dsink.reshape(ATT_HEADS, BLK), axis=1)
    return dh, g


def _gla_layer_fwd(h, w, M, tag, rides):
    D = h.shape[1]
    proj = _mm(h, w["w_in"], b_chip=1, out_dtype=CDT, ride=rides.get("mix"), name=f"{tag}_in")
    lr = _mm(h, w["w1x"], out_dtype=CDT, name=f"{tag}_in_gate")
    gf, gb = _gla_gate_fwd(lr, w["w2"], w["gbias"], D=D, name=f"{tag}_gate")
    of, stf = _gla_scan_fwd(proj, gf, rev=False, M=M, D=D, name=f"{tag}_scan_f")
    ob, stb = _gla_scan_fwd(proj, gb, rev=True, M=M, D=D, name=f"{tag}_scan_b")
    yg = _gla_out_fwd(of, ob, proj, w["onorm"], D=D, name=f"{tag}_out")
    y = _mm(yg, w["w_o"], b_chip=0, out_dtype=CDT, name=f"{tag}_wo")
    return y, dict(h=h, proj=proj, lr=lr, gf=gf, gb=gb, of=of, ob=ob, stf=stf, stb=stb, yg=yg)


def _gla_layer_bwd(dy, sv, w, M, tag):
    D = dy.shape[1]
    dyg = _mm(dy, w["w_o"], tb=True, b_chip=0, out_dtype=CDT, name=f"{tag}_dwo_x")
    g = {"w_o": _mm(sv["yg"], dy, ta=True, out_chip=0, out_dtype=CDT, name=f"{tag}_dwo_w")}
    do, dr, dgn = _gla_out_bwd(sv["of"], sv["ob"], sv["proj"], w["onorm"], dyg, D=D, name=f"{tag}_out_bwd")
    df = _gla_scan_bwd(sv["proj"], sv["gf"], sv["stf"], do, rev=False, M=M, D=D, name=f"{tag}_scan_f_bwd")
    db = _gla_scan_bwd(sv["proj"], sv["gb"], sv["stb"], do, rev=True, M=M, D=D, name=f"{tag}_scan_b_bwd")
    dproj, dlr, dw2, dbias = _gla_proj_bwd(sv["lr"], w["w2"], w["gbias"], *df, *db, dr, D=D, name=f"{tag}_proj_bwd")
    dh = _mm(dproj, w["w_in"], tb=True, b_chip=1, extra=(dlr, w["w1x"]), out_dtype=CDT, name=f"{tag}_din_x")
    g["w_in"] = _mm(sv["h"], dproj, ta=True, out_chip=1, out_dtype=CDT, name=f"{tag}_din_w")
    g["w1x"] = _mm(sv["h"], dlr, ta=True, out_dtype=F32, name=f"{tag}_din_gate_w")
    g["w2"] = dw2
    g["gbias"] = dbias[0]
    g["onorm"] = dgn[0]
    return dh, g


def _ffn_fwd(h2, w, M, tag, rides):
    u = _mm(h2, w["w_up"], b_chip=1, out_dtype=CDT, ride=rides.get("up"), name=f"{tag}_up")
    act = _conv_gate_fwd(u, w["conv_w"], w["conv_b"], M=M, ride=rides.get("conv"), name=f"{tag}_conv")
    f = _mm(act, w["w_down"], b_chip=0, out_dtype=CDT, ride=rides.get("down"), name=f"{tag}_down")
    return f, dict(h2=h2, u=u, act=act)


def _ffn_bwd(dyf, sv, w, M, tag, rides):
    dact = _mm(dyf, w["w_down"], tb=True, b_chip=0, out_dtype=CDT, ride=rides.get("ddown_x"), name=f"{tag}_ddown_x")
    g = {"w_down": _mm(sv["act"], dyf, ta=True, out_chip=0, out_dtype=CDT, ride=rides.get("ddown_w"), name=f"{tag}_ddown_w")}
    duc, cacc = _conv_gate_bwd(sv["u"], dact, w["conv_w"], w["conv_b"], M=M, name=f"{tag}_conv_bwd")
    du = _conv_t(duc, w["conv_w"], M=M, name=f"{tag}_conv_t")
    dh2 = _mm(du, w["w_up"], tb=True, a_split=True, b_chip=1, out_dtype=CDT, ride=rides.get("dup_x"), name=f"{tag}_dup_x")
    g["w_up"] = _mm(sv["h2"], du, ta=True, b_chip=1, out_chip=1, out_dtype=CDT, name=f"{tag}_dup_w")
    g["conv_w"] = jnp.concatenate([cacc[0, 0:3], cacc[1, 0:3]], axis=1)
    g["conv_b"] = jnp.concatenate([cacc[0, 3], cacc[1, 3]], axis=0)
    return dh2, g


def _norm_grads(acc, gain, scale):
    p = acc[:, 1]
    return acc[:, 0], p * gain, jnp.sum(p * (1.0 + scale[:, 0]), axis=0)


def _local_step(x, tgt, mods, weights_of, final_g, *, M, fwd_rides=None, bwd_rides=None, on_ffn_grads=None, on_grads=None):
    T, D = x.shape
    L = mods.shape[0]
    HD = D // ATT_HEADS
    tabs = _rope_tables(T - M, M, HD)
    sel = lambda i, k: mods[i][:, k:k + 1, :]
    saved = []
    xs, y_prev, gate_prev = x, None, None
    for i in range(L):
        w = weights_of(i, "mixer")
        rides = {} if fwd_rides is None else fwd_rides(i)
        x_in, h = _norm_fwd(xs, y_prev, gate_prev, w["g_mix"], sel(i, 0), sel(i, 1), M=M, name=f"l{i}_norm_mix")
        if "w_qkv" in w:
            y_mix, sm = _attn_layer_fwd(h, w, tabs, M, f"l{i}", rides)
        else:
            y_mix, sm = _gla_layer_fwd(h, w, M, f"l{i}", rides)
        w = {**w, **weights_of(i, "ffn")}
        x_mid, h2 = _norm_fwd(x_in, y_mix, sel(i, 2), w["g_ffn"], sel(i, 3), sel(i, 4), M=M, name=f"l{i}_norm_ffn")
        f, sf = _ffn_fwd(h2, w, M, f"l{i}", rides)
        saved.append(dict(x_in=x_in, x_mid=x_mid, y_mix=y_mix, f=f, sm=sm, sf=sf, w=w))
        xs, y_prev, gate_prev = x_mid, f, sel(i, 5)

    loss_parts, dx, dyf, acc = _final_loss(xs, y_prev, gate_prev, tgt, final_g, M=M, name="final_loss")
    loss = jnp.sum(loss_parts[:, 0, 0])
    d_final_g = acc[0, 0] + acc[1, 0]
    dmods = [None] * L
    grads = [None] * L
    dgate_ffn = acc[:, 2]
    for i in reversed(range(L)):
        sv = saved[i]
        w = sv["w"]
        dh2, g = _ffn_bwd(dyf, sv["sf"], w, M, f"l{i}", {} if bwd_rides is None else bwd_rides(i))
        dx, dy_mix, acc = _norm_bwd(sv["x_mid"], dh2, dx, w["g_ffn"], sel(i, 4), sv["y_mix"], sel(i, 2), M=M, name=f"l{i}_norm_ffn_bwd")
        dsh_f, dsc_f, g["g_ffn"] = _norm_grads(acc, w["g_ffn"], sel(i, 4))
        dgate_mix = acc[:, 2]
        if "w_qkv" in w:
            dh, gm = _attn_layer_bwd(dy_mix, sv["sm"], w, tabs, M, f"l{i}", None if on_ffn_grads is None else on_ffn_grads(i, g))
        else:
            dh, gm = _gla_layer_bwd(dy_mix, sv["sm"], w, M, f"l{i}")
        g.update(gm)
        if i > 0:
            dx, dyf, acc = _norm_bwd(sv["x_in"], dh, dx, w["g_mix"], sel(i, 1), saved[i - 1]["f"], sel(i - 1, 5), M=M, name=f"l{i}_norm_mix_bwd")
        else:
            dx, dyf, acc = _norm_bwd(sv["x_in"], dh, dx, w["g_mix"], sel(i, 1), None, None, M=M, name=f"l{i}_norm_mix_bwd")
        dsh_m, dsc_m, g["g_mix"] = _norm_grads(acc, w["g_mix"], sel(i, 1))
        dmods[i] = jnp.stack([dsh_m, dsc_m, dgate_mix, dsh_f, dsc_f, dgate_ffn], axis=1)
        dgate_ffn = acc[:, 2]
        grads[i] = g if on_grads is None else on_grads(i, g)
    return loss, dx, jnp.stack(dmods, axis=0), grads, d_final_g


ANY = pl.BlockSpec(memory_space=pl.ANY)


def _me():
    return lax.axis_index("x"), lax.axis_index("y"), lax.axis_index("c")


def _other_chips(mx, my):
    return [(1 - mx, my), (mx, 1 - my), (1 - mx, 1 - my)]


def _rcopy(src, dst, sems, k, dev):
    send_sems, recv_sems = sems
    return pltpu.make_async_remote_copy(src_ref=src, dst_ref=dst, send_sem=send_sems.at[k], recv_sem=recv_sems.at[k],
                                        device_id=dev, device_id_type=MESH)


def _all_gather8(x, *, name):
    m, n = x.shape

    def body(x_ref, out_ref, send_sems, recv_sems, local_sem):
        mx, my, mc = _me()
        sems = (send_sems, recv_sems)
        me, sib = (mx, my, mc), (mx, my, 1 - mc)
        chips = _other_chips(mx, my)
        blk = lambda d: out_ref.at[4 * d[0] + 2 * d[1] + d[2]]
        mine = pltpu.make_async_copy(x_ref, blk(me), local_sem)
        mine.start()
        first = [_rcopy(x_ref, blk(me), sems, 0, sib)]
        first += [_rcopy(x_ref, blk(me), sems, 1 + j, (*ch, mc)) for j, ch in enumerate(chips)]
        for cp in first:
            cp.start()
        passed = [_rcopy(blk((*ch, mc)), blk((*ch, mc)), sems, 4 + j, sib) for j, ch in enumerate(chips)]
        for j, ch in enumerate(chips):
            _rcopy(x_ref, blk((*ch, mc)), sems, 1 + j, me).wait_recv()
            passed[j].start()
        _rcopy(x_ref, blk(sib), sems, 0, me).wait_recv()
        for j, ch in enumerate(chips):
            _rcopy(x_ref, blk((*ch, 1 - mc)), sems, 4 + j, me).wait_recv()
        for cp in first + passed:
            cp.wait_send()
        mine.wait()

    return pl.pallas_call(
        body, name=name, out_shape=jax.ShapeDtypeStruct((8, m, n), x.dtype), in_specs=[ANY], out_specs=ANY,
        scratch_shapes=[pltpu.SemaphoreType.DMA((7,)), pltpu.SemaphoreType.DMA((7,)), pltpu.SemaphoreType.DMA],
    )(x)


ROW_TILES = (512, 352, 256, 128)


def _sem_pairs(n):
    return [pltpu.SemaphoreType.DMA((n,)), pltpu.SemaphoreType.DMA((n,))]


def _place(w, layer, pos, *, name):
    _, a, b = w.shape
    tr = _pick(a, ROW_TILES)

    def body(pos_ref, w_ref, o_ref):
        o_ref[...] = w_ref[...].astype(o_ref.dtype)

    return pl.pallas_call(
        body, name=name, out_shape=jax.ShapeDtypeStruct((4, a, b), CDT),
        grid_spec=pltpu.PrefetchScalarGridSpec(
            num_scalar_prefetch=1, grid=(a // tr,),
            in_specs=[pl.BlockSpec((None, tr, b), lambda i, pos: (layer, i, 0))],
            out_specs=pl.BlockSpec((None, tr, b), lambda i, pos: (pos[0], i, 0))),
        compiler_params=_cparams(("parallel",)),
    )(pos, w)


def _gather_layer(bufs, *, name):
    n = len(bufs)

    def body(*refs):
        outs = refs[n:2 * n]
        sems = (refs[2 * n], refs[2 * n + 1])
        mx, my, mc = _me()
        me, sib = (mx, my, mc), (mx, my, 1 - mc)
        chips = _other_chips(mx, my)
        p = 2 * mx + my
        qs = [2 * ch[0] + ch[1] for ch in chips]
        halves = [(pl.ds(mc * (o.shape[1] // 2), o.shape[1] // 2), pl.ds((1 - mc) * (o.shape[1] // 2), o.shape[1] // 2)) for o in outs]
        first = []
        for t, o in enumerate(outs):
            mine = halves[t][0]
            first += [_rcopy(o.at[p, mine], o.at[p, mine], sems, 6 * t + j, (*ch, mc)) for j, ch in enumerate(chips)]
        for cp in first:
            cp.start()
        passed = []
        for j in range(3):
            for t, o in enumerate(outs):
                mine = halves[t][0]
                _rcopy(o.at[qs[j], mine], o.at[qs[j], mine], sems, 6 * t + j, me).wait_recv()
                fwd = _rcopy(o.at[qs[j], mine], o.at[qs[j], mine], sems, 6 * t + 3 + j, sib)
                fwd.start()
                passed.append(fwd)
        for j in range(3):
            for t, o in enumerate(outs):
                theirs = halves[t][1]
                _rcopy(o.at[qs[j], theirs], o.at[qs[j], theirs], sems, 6 * t + 3 + j, me).wait_recv()
        for cp in first + passed:
            cp.wait_send()

    return pl.pallas_call(
        body, name=name, out_shape=[jax.ShapeDtypeStruct(b.shape, b.dtype) for b in bufs],
        in_specs=[ANY] * n, out_specs=[ANY] * n, input_output_aliases={t: t for t in range(n)},
        scratch_shapes=_sem_pairs(6 * n),
    )(*bufs)


def _gather_ici_plan(outs):
    mx, my, mc = _me()
    p = 2 * mx + my
    for t, o in enumerate(outs):
        ah = o.shape[1] // 2
        mine = pl.ds(mc * ah, ah)
        for j, ch in enumerate(_other_chips(mx, my)):
            yield 3 * t + j, o.at[p, mine], o.at[2 * ch[0] + ch[1], mine], (*ch, mc)


def _ride_gather(bufs):
    def start(r_in, r_out, sems):
        for k, src, _, dev in _gather_ici_plan(r_out):
            _rcopy(src, src, sems, k, dev).start()

    def finish(r_in, r_out, sems):
        for k, _, land, _ in _gather_ici_plan(r_out):
            _rcopy(land, land, sems, k, _me()).wait_recv()
        for k, src, _, dev in _gather_ici_plan(r_out):
            _rcopy(src, src, sems, k, dev).wait_send()

    return dict(ins=list(bufs), outs=[jax.ShapeDtypeStruct(b.shape, b.dtype) for b in bufs],
                alias={t: t for t in range(len(bufs))}, nsem=3 * len(bufs), start=start, finish=finish)


def _gather_d2d(bufs, *, name):
    n = len(bufs)

    def body(*refs):
        outs = refs[n:2 * n]
        sems = (refs[2 * n], refs[2 * n + 1])
        mx, my, mc = _me()
        sib = (mx, my, 1 - mc)
        qs = [2 * ch[0] + ch[1] for ch in _other_chips(mx, my)]
        cps = []
        for t, o in enumerate(outs):
            ah = o.shape[1] // 2
            for j, q in enumerate(qs):
                mine = o.at[q, pl.ds(mc * ah, ah)]
                cps.append((_rcopy(mine, mine, sems, 3 * t + j, sib), o.at[q, pl.ds((1 - mc) * ah, ah)], 3 * t + j))
        for cp, _, _ in cps:
            cp.start()
        for _, theirs, k in cps:
            _rcopy(theirs, theirs, sems, k, sib).wait_recv()
        for cp, _, _ in cps:
            cp.wait_send()

    return pl.pallas_call(
        body, name=name, out_shape=[jax.ShapeDtypeStruct(b.shape, b.dtype) for b in bufs],
        in_specs=[ANY] * n, out_specs=[ANY] * n, input_output_aliases={t: t for t in range(n)},
        scratch_shapes=_sem_pairs(3 * n),
    )(*bufs)


def _scatter_plan(ins, outs):
    mx, my, mc = _me()
    p = 2 * mx + my
    for t, (s, o) in enumerate(zip(ins, outs)):
        for j, ch in enumerate(_other_chips(mx, my)):
            q = 2 * ch[0] + ch[1]
            yield 3 * t + j, s.at[q], o.at[p], o.at[q], (*ch, mc)


def _ride_scatter(parts):
    def start(r_in, r_out, sems):
        for k, src, dst, _, dev in _scatter_plan(r_in, r_out):
            _rcopy(src, dst, sems, k, dev).start()

    def finish(r_in, r_out, sems):
        for k, _, _, land, _ in _scatter_plan(r_in, r_out):
            _rcopy(land, land, sems, k, _me()).wait_recv()
        for k, src, dst, _, dev in _scatter_plan(r_in, r_out):
            _rcopy(src, dst, sems, k, dev).wait_send()

    return dict(ins=list(parts), outs=[jax.ShapeDtypeStruct(s.shape, s.dtype) for s in parts], alias={},
                nsem=3 * len(parts), start=start, finish=finish)


def _rs_split(gs, *, name):
    n = len(gs)

    def body(*refs):
        ins, outs = refs[:n], refs[n:2 * n]
        sems = (refs[2 * n], refs[2 * n + 1])
        mx, my, mc = _me()
        sib = (mx, my, 1 - mc)
        cps = []
        for t, (g, o) in enumerate(zip(ins, outs)):
            ah = g.shape[1] // 2
            cps.append(_rcopy(g.at[:, pl.ds((1 - mc) * ah, ah), :], o, sems, t, sib))
        for cp in cps:
            cp.start()
        for cp in cps:
            cp.wait_recv()
        for cp in cps:
            cp.wait_send()

    return pl.pallas_call(
        body, name=name, out_shape=[jax.ShapeDtypeStruct((4, g.shape[1] // 2, g.shape[2]), g.dtype) for g in gs],
        in_specs=[ANY] * n, out_specs=[ANY] * n, scratch_shapes=_sem_pairs(n),
    )(*gs)


def _rs_add(g, got, pos, *, name):
    _, a, b = g.shape
    ah = a // 2
    tr = _pick(ah, ROW_TILES)
    nb = ah // tr

    def body(pos_ref, g_ref, r_ref, o_ref):
        o_ref[...] = (g_ref[...].astype(F32) + r_ref[...].astype(F32)).astype(o_ref.dtype)

    blk = pl.BlockSpec((None, tr, b), lambda q, i, pos: (q, i, 0))
    return pl.pallas_call(
        body, name=name, out_shape=jax.ShapeDtypeStruct((4, ah, b), g.dtype),
        grid_spec=pltpu.PrefetchScalarGridSpec(
            num_scalar_prefetch=1, grid=(4, nb),
            in_specs=[pl.BlockSpec((None, tr, b), lambda q, i, pos: (q, pos[1] * nb + i, 0)), blk], out_specs=blk),
        compiler_params=_cparams(("parallel", "parallel")),
    )(pos, g, got)


def _rs_scatter(ps, *, name):
    n = len(ps)

    def body(*refs):
        ins, outs = refs[:n], refs[n:2 * n]
        sems = (refs[2 * n], refs[2 * n + 1])
        mx, my, mc = _me()
        me = (mx, my, mc)
        chips = _other_chips(mx, my)
        p = 2 * mx + my
        sends = []
        for t, (s, o) in enumerate(zip(ins, outs)):
            sends += [_rcopy(s.at[2 * ch[0] + ch[1]], o.at[p], sems, 3 * t + j, (*ch, mc)) for j, ch in enumerate(chips)]
        for cp in sends:
            cp.start()
        for t, (s, o) in enumerate(zip(ins, outs)):
            for j, ch in enumerate(chips):
                _rcopy(s.at[p], o.at[2 * ch[0] + ch[1]], sems, 3 * t + j, me).wait_recv()
        for cp in sends:
            cp.wait_send()

    return pl.pallas_call(
        body, name=name, out_shape=[jax.ShapeDtypeStruct(s.shape, s.dtype) for s in ps],
        in_specs=[ANY] * n, out_specs=[ANY] * n, scratch_shapes=_sem_pairs(3 * n),
    )(*ps)


def _rs_sum(part, recv, buf, layer, pos, *, name):
    _, ah, b = part.shape
    tr = _pick(ah, ROW_TILES)
    nb = ah // tr

    def body(pos_ref, p_ref, r0, r1, r2, buf_ref, o_ref):
        o_ref[...] = ((p_ref[...].astype(F32) + r0[...].astype(F32)) + r1[...].astype(F32)) + r2[...].astype(F32)

    other = lambda k: pl.BlockSpec((None, tr, b), lambda i, pos: (jnp.where(pos[0] <= k, k + 1, k), i, 0))
    return pl.pallas_call(
        body, name=name, out_shape=jax.ShapeDtypeStruct(buf.shape, buf.dtype),
        grid_spec=pltpu.PrefetchScalarGridSpec(
            num_scalar_prefetch=1, grid=(nb,),
            in_specs=[pl.BlockSpec((None, tr, b), lambda i, pos: (pos[0], i, 0)), other(0), other(1), other(2), ANY],
            out_specs=pl.BlockSpec((None, tr, b), lambda i, pos: (layer, pos[1] * nb + i, 0))),
        input_output_aliases={5: 0},
        compiler_params=_cparams(("parallel",)),
    )(pos, part, recv, recv, recv, buf)


def _rs_share(bufs, layers, *, name):
    n = len(bufs)

    def body(*refs):
        outs = refs[n:2 * n]
        sems = (refs[2 * n], refs[2 * n + 1])
        mx, my, mc = _me()
        sib = (mx, my, 1 - mc)
        cps = []
        for t, o in enumerate(outs):
            ah = o.shape[1] // 2
            mine = o.at[layers[t], pl.ds(mc * ah, ah)]
            cps.append((_rcopy(mine, mine, sems, t, sib), o.at[layers[t], pl.ds((1 - mc) * ah, ah)]))
        for cp, _ in cps:
            cp.start()
        for t, (cp, theirs) in enumerate(cps):
            _rcopy(theirs, theirs, sems, t, sib).wait_recv()
        for cp, _ in cps:
            cp.wait_send()

    return pl.pallas_call(
        body, name=name, out_shape=[jax.ShapeDtypeStruct(b.shape, b.dtype) for b in bufs],
        in_specs=[ANY] * n, out_specs=[ANY] * n, input_output_aliases={t: t for t in range(n)},
        scratch_shapes=_sem_pairs(n),
    )(*bufs)


def _sum_lead(a, *, name):
    n, R, W = a.shape
    tr = _pick(R, (PACK_ROWS,))
    specs = [pl.BlockSpec((1, tr, W), functools.partial(lambda i, q: (q, i, 0), q=q)) for q in range(n)]

    def body(*refs):
        acc = refs[0][0].astype(F32)
        for r in refs[1:n]:
            acc = acc + r[0].astype(F32)
        refs[n][...] = acc

    return pl.pallas_call(
        body, name=name, grid=(R // tr,), in_specs=specs, out_specs=pl.BlockSpec((tr, W), lambda i: (i, 0)),
        out_shape=jax.ShapeDtypeStruct((R, W), F32), compiler_params=_cparams(("parallel",)),
    )(*([a] * n))


SMALL_SHARDED = (("ffn_conv_w", 2), ("gla_gf_w1", 1), ("gla_gf_w2", 2), ("gla_gf_b", 1), ("gla_gb_w1", 1), ("gla_gb_w2", 2),
                 ("gla_gb_b", 1), ("gla_onorm_g", 1))


def _rows_of(flat, width):
    rows = -(-flat.shape[0] // (8 * width)) * 8
    return jnp.pad(flat, (0, rows * width - flat.shape[0])).reshape(rows, width)


def _size(shape):
    n = 1
    for s in shape:
        n *= s
    return n


def _gather_small(shards):
    flat = jnp.concatenate([shards[name].astype(F32).reshape(-1) for name, _ in SMALL_SHARDED])
    got = _all_gather8(_rows_of(flat, SMALL_W), name="gather_small_w")[0::2].reshape(4, -1)
    full, off = {}, 0
    for name, ax in SMALL_SHARDED:
        shape = shards[name].shape
        n = _size(shape)
        seg = jnp.moveaxis(got[:, off:off + n].reshape((4,) + shape), 0, ax)
        full[name] = seg.reshape(shape[:ax] + (4 * shape[ax],) + shape[ax + 1:])
        off += n
    return full


WEIGHTS = ("c_ctx", "ada_w", "ada_b", "norm_mix_g", "norm_ffn_g", "ffn_w_up", "ffn_conv_w", "ffn_conv_b", "ffn_w_down",
           "attn_w_qkv", "attn_sink", "attn_w_o", "gla_w_in", "gla_gf_w1", "gla_gf_w2", "gla_gf_b", "gla_gb_w1", "gla_gb_w2",
           "gla_gb_b", "gla_onorm_g", "gla_w_o", "final_norm_g")
REPLICATED = ("norm_mix_g", "norm_ffn_g", "ffn_conv_b", "attn_sink", "final_norm_g", "c_ctx")
SMALL_W = 2048
ROWS16 = 16


def _layer_big(i):
    j = i // 2
    mixer = [("w_qkv", "attn_w_qkv", j), ("w_o", "attn_w_o", j)] if i % 2 == 0 else [("w_in", "gla_w_in", j), ("w_o", "gla_w_o", j)]
    return [("w_up", "ffn_w_up", i), ("w_down", "ffn_w_down", i)] + mixer


def _layer_weights(i, big, small, rep):
    D = rep["norm_mix_g"].shape[1]
    j = i // 2
    w = dict(g_mix=rep["norm_mix_g"][i][None], g_ffn=rep["norm_ffn_g"][i][None], conv_w=small["ffn_conv_w"][i],
             conv_b=rep["ffn_conv_b"][i][None], **big)
    if i % 2 == 0:
        w["sink"] = rep["attn_sink"][j]
    else:
        r = GATE_RANK
        w2 = jnp.zeros((128, D), F32)
        w2 = w2.at[0:r, 0:D // 2].set(small["gla_gf_w2"][j]).at[r:2 * r, D // 2:].set(small["gla_gb_w2"][j])
        w1x = jnp.concatenate([small["gla_gf_w1"][j], small["gla_gb_w1"][j], jnp.zeros((D, 128 - 2 * r), F32)], axis=1)
        w.update(w1x=w1x.astype(CDT), w2=w2.astype(CDT), gbias=jnp.concatenate([small["gla_gf_b"][j], small["gla_gb_b"][j]])[None],
                 onorm=small["gla_onorm_g"][j][None])
    return w


def _small_grads(grads, D):
    att = [g for g in grads if "sink" in g]
    gla = [g for g in grads if "w1x" in g]
    st = lambda xs: jnp.stack(xs, axis=0)
    r = GATE_RANK
    return {
        "ffn_conv_w": st([g["conv_w"] for g in grads]),
        "gla_gf_w1": st([g["w1x"][:, 0:r] for g in gla]), "gla_gb_w1": st([g["w1x"][:, r:2 * r] for g in gla]),
        "gla_gf_w2": st([g["w2"][0:r, :D // 2] for g in gla]), "gla_gb_w2": st([g["w2"][r:2 * r, D // 2:] for g in gla]),
        "gla_gf_b": st([g["gbias"][:D // 2] for g in gla]), "gla_gb_b": st([g["gbias"][D // 2:] for g in gla]),
        "gla_onorm_g": st([g["onorm"] for g in gla]),
        "norm_mix_g": st([g["g_mix"] for g in grads]), "norm_ffn_g": st([g["g_ffn"] for g in grads]),
        "ffn_conv_b": st([g["conv_b"] for g in grads]), "attn_sink": st([g["sink"] for g in att]),
    }


def kernel(x, c, ctx, c_ctx, ada_w, ada_b, norm_mix_g, norm_ffn_g, ffn_w_up, ffn_conv_w, ffn_conv_b, ffn_w_down, attn_w_qkv, attn_sink, attn_w_o, gla_w_in, gla_gf_w1, gla_gf_w2, gla_gf_b, gla_gb_w1, gla_gb_w2, gla_gb_b, gla_onorm_g, gla_w_o, final_norm_g, loss_target, m_c_ctx, m_ada_w, m_ada_b, m_norm_mix_g, m_norm_ffn_g, m_ffn_w_up, m_ffn_conv_w, m_ffn_conv_b, m_ffn_w_down, m_attn_w_qkv, m_attn_sink, m_attn_w_o, m_gla_w_in, m_gla_gf_w1, m_gla_gf_w2, m_gla_gf_b, m_gla_gb_w1, m_gla_gb_w2, m_gla_gb_b, m_gla_onorm_g, m_gla_w_o, m_final_norm_g, v_c_ctx, v_ada_w, v_ada_b, v_norm_mix_g, v_norm_ffn_g, v_ffn_w_up, v_ffn_conv_w, v_ffn_conv_b, v_ffn_w_down, v_attn_w_qkv, v_attn_sink, v_attn_w_o, v_gla_w_in, v_gla_gf_w1, v_gla_gf_w2, v_gla_gf_b, v_gla_gb_w1, v_gla_gb_w2, v_gla_gb_b, v_gla_onorm_g, v_gla_w_o, v_final_norm_g):
    wts = dict(c_ctx=c_ctx, ada_w=ada_w, ada_b=ada_b, norm_mix_g=norm_mix_g, norm_ffn_g=norm_ffn_g, ffn_w_up=ffn_w_up,
               ffn_conv_w=ffn_conv_w, ffn_conv_b=ffn_conv_b, ffn_w_down=ffn_w_down, attn_w_qkv=attn_w_qkv, attn_sink=attn_sink,
               attn_w_o=attn_w_o, gla_w_in=gla_w_in, gla_gf_w1=gla_gf_w1, gla_gf_w2=gla_gf_w2, gla_gf_b=gla_gf_b,
               gla_gb_w1=gla_gb_w1, gla_gb_w2=gla_gb_w2, gla_gb_b=gla_gb_b, gla_onorm_g=gla_onorm_g, gla_w_o=gla_w_o,
               final_norm_g=final_norm_g)
    mom_m = dict(zip(WEIGHTS, (m_c_ctx, m_ada_w, m_ada_b, m_norm_mix_g, m_norm_ffn_g, m_ffn_w_up, m_ffn_conv_w, m_ffn_conv_b,
                               m_ffn_w_down, m_attn_w_qkv, m_attn_sink, m_attn_w_o, m_gla_w_in, m_gla_gf_w1, m_gla_gf_w2,
                               m_gla_gf_b, m_gla_gb_w1, m_gla_gb_w2, m_gla_gb_b, m_gla_onorm_g, m_gla_w_o, m_final_norm_g)))
    mom_v = dict(zip(WEIGHTS, (v_c_ctx, v_ada_w, v_ada_b, v_norm_mix_g, v_norm_ffn_g, v_ffn_w_up, v_ffn_conv_w, v_ffn_conv_b,
                               v_ffn_w_down, v_attn_w_qkv, v_attn_sink, v_attn_w_o, v_gla_w_in, v_gla_gf_w1, v_gla_gf_w2,
                               v_gla_gf_b, v_gla_gb_w1, v_gla_gb_w2, v_gla_gb_b, v_gla_onorm_g, v_gla_w_o, v_final_norm_g)))
    L, D, W6 = ada_w.shape[0], ada_w.shape[1], ada_w.shape[2]
    M = ctx.shape[1]
    mx, my, mc = _me()
    chip = 2 * mx + my
    batch = 4 * mx + 2 * my + mc

    crow = jnp.concatenate([c.astype(F32), jnp.zeros((7, D), F32)], axis=0)
    call = _all_gather8(crow, name="gather_c")[:, 0, :]
    s16 = jnp.concatenate([jax.nn.silu(call), jax.nn.silu(c_ctx)[None], jnp.zeros((ROWS16 - 9, D), F32)], axis=0)
    s16c = s16.astype(CDT)
    ada_c = ada_w.astype(CDT)
    mod_cols = jnp.concatenate([_mm(s16c, ada_c[i], out_dtype=F32, name=f"mods_l{i}") for i in range(L)], axis=0)
    mod_all = _all_gather8(mod_cols, name="gather_mods")
    mod_all = mod_all.reshape(4, 2, L, ROWS16, W6)[:, 0]
    mod_all = jnp.moveaxis(mod_all, 0, 2).reshape(L, ROWS16, 4 * W6) + ada_b[:, None, :]
    mod_mine = jnp.stack([mod_all[:, 8], lax.dynamic_index_in_dim(mod_all, batch, axis=1, keepdims=False)], axis=1)
    mods = mod_mine.reshape(L, 2, N_MOD, D)

    pos = jnp.stack([chip, mc]).astype(jnp.int32)
    small_w = _gather_small({name: wts[name] for name, _ in SMALL_SHARDED})
    def place(i):
        return [_place(wts[name], j, pos, name=f"l{i}_place_{key}") for key, name, j in _layer_big(i)]

    placed0 = place(0)
    gathered = {0: [None, None] + list(_gather_layer(placed0[2:4], name="l0_gather_mixer"))}
    carried = {0: dict(up=_ride_gather(placed0[0:1]), down=_ride_gather(placed0[1:2]))}

    def weights_of(i, part):
        keys = [key for key, _, _ in _layer_big(i)]
        if part == "mixer":
            if i not in gathered:
                r = carried.pop(i)
                gathered[i] = _gather_d2d(r["up"]["result"] + r["down"]["result"] + r["mixer"]["result"], name=f"l{i}_gather_d2d")
            return _layer_weights(i, dict(zip(keys[2:4], gathered[i][2:4])), small_w, wts)
        if i == 0:
            r = carried.pop(0)
            gathered[0][0:2] = _gather_d2d(r["up"]["result"] + r["down"]["result"], name="l0_gather_d2d")
        return dict(zip(keys[0:2], gathered[i][0:2]))

    def fwd_rides(i):
        hosts = dict(attn=carried[0]["up"], mix=carried[0]["down"]) if i == 0 else {}
        if i + 1 < L:
            bufs = place(i + 1)
            nxt = dict(up=_ride_gather(bufs[0:1]), down=_ride_gather(bufs[1:2]), mixer=_ride_gather(bufs[2:4]))
            carried[i + 1] = nxt
            hosts.update(up=nxt["up"], down=nxt["down"])
            hosts["mix" if i % 2 == 1 else ("conv" if i == 0 else "attn")] = nxt["mixer"]
        return hosts

    red = {name: jnp.zeros(wts[name].shape, F32) for name in ("ffn_w_up", "ffn_w_down", "attn_w_qkv", "attn_w_o", "gla_w_in", "gla_w_o")}
    pending = {}

    def pair_sums(i, keys, g, tag):
        gs = [g.pop(key) for key, _, _ in keys]
        gots = _rs_split(gs, name=f"l{i}_rs_split{tag}")
        return [_rs_add(gv, got, pos, name=f"l{i}_rs_add_{key}") for (key, _, _), gv, got in zip(keys, gs, gots)]

    def bwd_rides(i):
        if i + 1 not in pending:
            return {}
        parts = pending[i + 1]["parts"]
        r = dict(dup_x=_ride_scatter(parts[0:1]), ddown_x=_ride_scatter(parts[1:2]), ddown_w=_ride_scatter(parts[2:4]))
        pending[i + 1]["rides"] = [r["dup_x"], r["ddown_x"], r["ddown_w"]]
        return r

    def early_ffn(i, g):
        if i != 0:
            return None
        ride = _ride_scatter(pair_sums(0, _layer_big(0)[0:2], g, "_ffn"))
        pending["ffn0"] = ride
        return ride

    def finish_reduce(i):
        keys = _layer_big(i)
        p = pending.pop(i)
        if i == 0:
            early = pending.pop("ffn0")
            parts = early["ins"] + p["parts"]
            recvs = early["result"] + list(_rs_scatter(p["parts"], name="l0_rs_scatter_mixer"))
        else:
            parts = p["parts"]
            recvs = [buf for r in p["rides"] for buf in r["result"]]
        outs = [_rs_sum(part, recv, red[name], j, pos, name=f"l{i}_rs_sum_{key}") for (key, name, j), part, recv in zip(keys, parts, recvs)]
        outs = _rs_share(outs, [j for _, _, j in keys], name=f"l{i}_rs_share")
        for (_, name, _), out in zip(keys, outs):
            red[name] = out

    def reduce_layer(i, g):
        if i + 1 in pending:
            finish_reduce(i + 1)
        keys = [k for k in _layer_big(i) if k[0] in g]
        pending[i] = dict(parts=pair_sums(i, keys, g, ""))
        return g

    xcat = jnp.concatenate([ctx[0], x[0]], axis=0)
    loss, dx, dmods, grads, d_final_g = _local_step(xcat, loss_target[0], mods, weights_of, final_norm_g[None], M=M,
                                                    fwd_rides=fwd_rides, bwd_rides=bwd_rides, on_ffn_grads=early_ffn,
                                                    on_grads=reduce_layer)
    finish_reduce(0)
    loss = lax.psum(loss, ("x", "y", "c"))
    grad_x = dx[M:][None]

    dm_all = _all_gather8(dmods.reshape(L * 2, N_MOD * D), name="gather_dmods")
    dm_sum = _sum_lead(dm_all, name="dmods_sum").reshape(L, 2, N_MOD * D)
    dm_rows = dm_all.reshape(8, L, 2, N_MOD * D)[:, :, 1]
    dm16 = jnp.concatenate([jnp.moveaxis(dm_rows, 0, 1), dm_sum[:, 0:1], jnp.zeros((L, ROWS16 - 9, N_MOD * D), F32)], axis=1)
    dm16 = lax.dynamic_slice_in_dim(dm16, chip * W6, W6, axis=2).astype(CDT)
    g_ada_w = jnp.stack([_mm(s16c, dm16[i], ta=True, out_dtype=F32, name=f"dada_w_l{i}") for i in range(L)], axis=0)
    ds16 = _mm(dm16[0], ada_c[0], tb=True, out_dtype=F32, name="dcond_l0")
    for i in range(1, L):
        ds16 = ds16 + _mm(dm16[i], ada_c[i], tb=True, out_dtype=F32, name=f"dcond_l{i}")
    d_sctx = ds16[8] * jnp.where(mc == 0, 1.0, 0.0)

    gfull = _small_grads(grads, D)
    gfull["final_norm_g"] = d_final_g
    gfull["c_ctx"] = d_sctx

    small_names = list(REPLICATED) + [name for name, _ in SMALL_SHARDED]
    flat = jnp.concatenate([gfull[name].astype(F32).reshape(-1) for name in small_names])
    small = _sum_lead(_all_gather8(_rows_of(flat, SMALL_W), name="gather_small_g"), name="small_sum").reshape(-1)
    off = 0
    for name in small_names:
        shape = gfull[name].shape
        red[name] = small[off:off + _size(shape)].reshape(shape)
        off += _size(shape)
    for name, ax in SMALL_SHARDED:
        shape = wts[name].shape
        g4 = red[name].reshape(shape[:ax] + (4, shape[ax]) + shape[ax + 1:])
        red[name] = lax.dynamic_index_in_dim(g4, chip, axis=ax, keepdims=False)
    sig = jax.nn.sigmoid(c_ctx)
    red["c_ctx"] = red["c_ctx"] * (sig * (1.0 + c_ctx * (1.0 - sig)))
    red["ada_w"] = g_ada_w
    red["ada_b"] = dm_sum[:, 0] + dm_sum[:, 1]

    deltas, new_m, new_v = {}, {}, {}
    for name in WEIGHTS:
        w = wts[name]
        view = (lambda a: a.reshape(-1, a.shape[-1])) if w.ndim > 1 else (lambda a: a.reshape(1, -1))
        d, m2, v2 = _adamw(view(w), view(red[name]), view(mom_m[name]), view(mom_v[name]), name=f"adamw_{name}")
        deltas[name], new_m[name], new_v[name] = d.reshape(w.shape), m2.reshape(w.shape), v2.reshape(w.shape)
    return (loss, grad_x, *[red[n] for n in WEIGHTS], *[deltas[n] for n in WEIGHTS], *[new_m[n] for n in WEIGHTS],
            *[new_v[n] for n in WEIGHTS])
```

```python
import functools

import jax
import jax.numpy as jnp
from jax import lax
from jax.experimental import pallas as pl
from jax.experimental.pallas import tpu as pltpu

F32 = jnp.float32
CDT = jnp.bfloat16
VMEM_LIMIT = 56 * 1024 * 1024
MESH = pl.DeviceIdType.MESH

ATT_HEADS = 16
ATT_KV = 4
ATT_G = ATT_HEADS // ATT_KV
BLK = 128
GRID_W = 64
ROPE_BASE = 10000.0
GLA_H = 4
GATE_RANK = 16
GATE_NORM = 16.0
CHUNK = 64
EPS = 1e-6
N_MOD = 6
LR, B1, B2, AEPS, WD, STEP = 0.001, 0.9, 0.999, 1e-08, 0.01, 10
PACK_ROWS = 512

NN = (((1,), (0,)), ((), ()))
NT = (((1,), (1,)), ((), ()))
TN = (((0,), (0,)), ((), ()))


def _dg(a, b, dims):
    return lax.dot_general(a, b, dims, preferred_element_type=F32)


def _pick(dim, cands):
    for c in cands:
        if dim % c == 0:
            return c
    return dim


def _cparams(sem):
    return pltpu.CompilerParams(dimension_semantics=sem, vmem_limit_bytes=VMEM_LIMIT)


def _pcall(body, *, name, grid, in_specs, out_specs, out_shape, scratch_shapes=(), sem, ride=None):
    if ride is None:
        return pl.pallas_call(body, name=name, grid=grid, in_specs=list(in_specs), out_specs=out_specs, out_shape=out_shape,
                              scratch_shapes=list(scratch_shapes), compiler_params=_cparams(sem))
    multi = isinstance(out_shape, (tuple, list))
    o_specs = list(out_specs) if multi else [out_specs]
    o_shapes = list(out_shape) if multi else [out_shape]
    n_in, n_out, n_scr = len(in_specs), len(o_specs), len(scratch_shapes)
    n_rin, n_rout = len(ride["ins"]), len(ride["outs"])
    total = 1
    for extent in grid:
        total *= extent

    def carrying(*refs):
        o0 = n_in + n_rin
        s0 = o0 + n_out + n_rout
        r_in, r_out, sems = refs[n_in:o0], refs[o0 + n_out:s0], (refs[-2], refs[-1])
        step = 0
        for axis, extent in enumerate(grid):
            step = step * extent + pl.program_id(axis)

        @pl.when(step == 0)
        def _():
            ride["start"](r_in, r_out, sems)

        body(*refs[:n_in], *refs[o0:o0 + n_out], *refs[s0:s0 + n_scr])

        @pl.when(step == total - 1)
        def _():
            ride["finish"](r_in, r_out, sems)

    hbm = pl.BlockSpec(memory_space=pl.ANY)
    call = pl.pallas_call(
        carrying, name=name, grid=grid, in_specs=list(in_specs) + [hbm] * n_rin, out_specs=o_specs + [hbm] * n_rout,
        out_shape=o_shapes + list(ride["outs"]), input_output_aliases={n_in + i: n_out + o for i, o in ride["alias"].items()},
        scratch_shapes=list(scratch_shapes) + [pltpu.SemaphoreType.DMA((ride["nsem"],)), pltpu.SemaphoreType.DMA((ride["nsem"],))],
        compiler_params=_cparams(("arbitrary",) * len(grid)))

    def run(*args):
        res = call(*args, *ride["ins"])
        ride["result"] = list(res[n_out:])
        return tuple(res[:n_out]) if multi else res[0]

    return run


def _sigmoid(x):
    return 1.0 / (1.0 + jnp.exp(-x))


def _silu(x):
    return x * _sigmoid(x)


def _silu_pair(x):
    s = _sigmoid(x)
    return x * s, s * (1.0 + x * (1.0 - s))


MM_VMEM_BUDGET = 40 * 1024 * 1024
TILE_M = (2048, 1408, 1088, 1024, 544, 512, 256, 128)
TILE_N = (2048, 1536, 1408, 1024, 768, 512, 256, 128)
TILE_K = (2176, 2048, 1408, 1088, 1024, 768, 512)


def _mm_tiles(m_unit, n_unit, k_unit, out_bytes):
    best = None
    for tm in [c for c in TILE_M if m_unit % c == 0] or [m_unit]:
        for tn in [c for c in TILE_N if n_unit % c == 0] or [n_unit]:
            for tk in [c for c in TILE_K if k_unit % c == 0] or [k_unit]:
                vmem = 4 * tk * (tm + tn) + tm * tn * (2 * out_bytes + 4)
                if vmem > MM_VMEM_BUDGET:
                    continue
                key = (tm * tn / (tm + tn), tk)
                if best is None or key > best[0]:
                    best = (key, (tm, tn, tk))
    assert best is not None, (m_unit, n_unit, k_unit)
    return best[1]


def _mm(a, b, *, ta=False, tb=False, out_dtype=F32, name, a_split=False, b_chip=None, out_chip=None, extra=None, ride=None):
    if a_split:
        assert not ta
        M, K = a.shape[1], a.shape[0] * a.shape[2]
    elif ta:
        K, M = a.shape
    else:
        M, K = a.shape
    bs = list(b.shape) if b_chip is None else list(b.shape[1:])
    if b_chip is not None:
        bs[b_chip] *= b.shape[0]
    N, K2 = bs if tb else bs[::-1]
    assert K == K2, (a.shape, b.shape, ta, tb, b_chip)
    m_unit, n_unit, k_unit = M, N, K
    if a_split:
        k_unit = a.shape[2]
    if b_chip is not None:
        if (b_chip == 0) == tb:
            n_unit = N // b.shape[0]
        else:
            k_unit = min(k_unit, K // b.shape[0])
    if out_chip == 0:
        m_unit = M // 4
    elif out_chip == 1:
        n_unit = min(n_unit, N // 4)
    tm, tn, tk = _mm_tiles(m_unit, n_unit, k_unit, jnp.dtype(out_dtype).itemsize)
    nk = K // tk
    dims = TN if ta else (NT if tb else NN)

    def body(*refs):
        a_ref, b_ref, o_ref, acc_ref = refs[0], refs[1], refs[-2], refs[-1]
        k = pl.program_id(2)

        @pl.when(k == 0)
        def _():
            if extra is None:
                acc_ref[...] = jnp.zeros_like(acc_ref)
            else:
                acc_ref[...] = _dg(refs[2][...], refs[3][...], dims)

        acc_ref[...] += _dg(a_ref[...], b_ref[...], dims)

        @pl.when(k == nk - 1)
        def _():
            o_ref[...] = acc_ref[...].astype(o_ref.dtype)

    def b_index(n, m, k):
        i0, i1 = (n, k) if tb else (k, n)
        if b_chip is None:
            return (i0, i1)
        if b_chip == 0:
            nb = b.shape[1] // b_block[0]
            return (i0 // nb, i0 % nb, i1)
        nb = b.shape[2] // b_block[1]
        return (i1 // nb, i0, i1 % nb)

    def o_index(n, m, k):
        if out_chip is None:
            return (m, n)
        if out_chip == 0:
            mb = m_unit // tm
            return (m // mb, m % mb, n)
        nb = n_unit // tn
        return (n // nb, m, n % nb)

    b_block = (tn, tk) if tb else (tk, tn)
    lead = lambda blk, on: ((None,) + blk) if on else blk
    if a_split:
        kb = a.shape[2] // tk
        a_spec = pl.BlockSpec((None, tm, tk), lambda n, m, k: (k // kb, m, k % kb))
    elif ta:
        a_spec = pl.BlockSpec((tk, tm), lambda n, m, k: (k, m))
    else:
        a_spec = pl.BlockSpec((tm, tk), lambda n, m, k: (m, k))
    in_specs = [a_spec, pl.BlockSpec(lead(b_block, b_chip is not None), b_index)]
    args = [a, b]
    if extra is not None:
        assert not ta
        a2, b2 = extra
        E = a2.shape[1]
        in_specs += [pl.BlockSpec((tm, E), lambda n, m, k: (m, 0)),
                     pl.BlockSpec((tn, E), lambda n, m, k: (n, 0)) if tb else pl.BlockSpec((E, tn), lambda n, m, k: (0, n))]
        args += [a2, b2]
    out_full = (M, N) if out_chip is None else ((4, M // 4, N) if out_chip == 0 else (4, M, N // 4))
    return _pcall(
        body, name=name, grid=(N // tn, M // tm, nk), in_specs=in_specs,
        out_specs=pl.BlockSpec(lead((tm, tn), out_chip is not None), o_index),
        out_shape=jax.ShapeDtypeStruct(out_full, out_dtype), scratch_shapes=[pltpu.VMEM((tm, tn), F32)],
        sem=("parallel", "parallel", "arbitrary"), ride=ride,
    )(*args)


def _seg_spec(D, first_lat):
    return pl.BlockSpec((1, 1, D), lambda i: (jnp.where(i >= first_lat, 1, 0), 0, 0))


def _norm_fwd(x, y, gate, g, shift, scale, *, M, name):
    T, D = x.shape
    tm = _pick(T, (256,))
    first_lat = M // tm
    has_res = y is not None
    seg = _seg_spec(D, first_lat)
    row = pl.BlockSpec((tm, D), lambda i: (i, 0))

    def body(*refs):
        if has_res:
            x_ref, y_ref, gate_ref, g_ref, sh_ref, sc_ref, xo_ref, h_ref = refs
            xv = x_ref[...] + gate_ref[0] * y_ref[...].astype(F32)
            xo_ref[...] = xv
        else:
            x_ref, g_ref, sh_ref, sc_ref, h_ref = refs
            xv = x_ref[...]
        rstd = lax.rsqrt(jnp.mean(xv * xv, axis=-1, keepdims=True) + EPS)
        h = xv * rstd * g_ref[...] * (1.0 + sc_ref[0]) + sh_ref[0]
        h_ref[...] = h.astype(h_ref.dtype)

    gspec = pl.BlockSpec((1, D), lambda i: (0, 0))
    if has_res:
        ins = [x, y, gate, g, shift, scale]
        in_specs = [row, row, seg, gspec, seg, seg]
        out_shape = (jax.ShapeDtypeStruct((T, D), F32), jax.ShapeDtypeStruct((T, D), CDT))
        out_specs = (row, row)
    else:
        ins = [x, g, shift, scale]
        in_specs = [row, gspec, seg, seg]
        out_shape = jax.ShapeDtypeStruct((T, D), CDT)
        out_specs = row
    out = pl.pallas_call(
        body, name=name, grid=(T // tm,), in_specs=in_specs, out_specs=out_specs, out_shape=out_shape,
        compiler_params=_cparams(("parallel",)),
    )(*ins)
    return out if has_res else (x, out)


def _norm_bwd(x, dh, dx_in, g, scale, y_prev, gate_prev, *, M, name):
    T, D = x.shape
    tm = _pick(T, (256,))
    first_lat = M // tm
    has_prev = y_prev is not None
    seg = _seg_spec(D, first_lat)
    row = pl.BlockSpec((tm, D), lambda i: (i, 0))
    gspec = pl.BlockSpec((1, D), lambda i: (0, 0))

    def body(*refs):
        if has_prev:
            x_ref, dh_ref, dxi_ref, g_ref, sc_ref, yp_ref, gp_ref, dx_ref, dy_ref, acc_ref = refs
        else:
            x_ref, dh_ref, dxi_ref, g_ref, sc_ref, dx_ref, acc_ref = refs
        i = pl.program_id(0)

        @pl.when(jnp.logical_or(i == 0, i == first_lat))
        def _():
            acc_ref[...] = jnp.zeros_like(acc_ref)

        xv = x_ref[...]
        rstd = lax.rsqrt(jnp.mean(xv * xv, axis=-1, keepdims=True) + EPS)
        xn = xv * rstd
        dh = dh_ref[...].astype(F32)
        dxn = dh * (g_ref[...] * (1.0 + sc_ref[0]))
        dx = dxi_ref[...] + rstd * (dxn - xn * jnp.mean(dxn * xn, axis=-1, keepdims=True))
        dx_ref[...] = dx
        acc_ref[0, 0:1, :] += jnp.sum(dh, axis=0, keepdims=True)
        acc_ref[0, 1:2, :] += jnp.sum(dh * xn, axis=0, keepdims=True)
        if has_prev:
            dy_ref[...] = (dx * gp_ref[0]).astype(dy_ref.dtype)
            acc_ref[0, 2:3, :] += jnp.sum(dx * yp_ref[...].astype(F32), axis=0, keepdims=True)

    acc_spec = pl.BlockSpec((1, 8, D), lambda i: (jnp.where(i >= first_lat, 1, 0), 0, 0))
    acc_shape = jax.ShapeDtypeStruct((2, 8, D), F32)
    if has_prev:
        ins = [x, dh, dx_in, g, scale, y_prev, gate_prev]
        in_specs = [row, row, row, gspec, seg, row, seg]
        out_shape = (jax.ShapeDtypeStruct((T, D), F32), jax.ShapeDtypeStruct((T, D), CDT), acc_shape)
        out_specs = (row, row, acc_spec)
    else:
        ins = [x, dh, dx_in, g, scale]
        in_specs = [row, row, row, gspec, seg]
        out_shape = (jax.ShapeDtypeStruct((T, D), F32), acc_shape)
        out_specs = (row, acc_spec)
    out = pl.pallas_call(
        body, name=name, grid=(T // tm,), in_specs=in_specs, out_specs=out_specs, out_shape=out_shape,
        compiler_params=_cparams(("arbitrary",)),
    )(*ins)
    if has_prev:
        return out
    return out[0], None, out[1]


def _final_loss(x, y_prev, gate_prev, tgt, g, *, M, name):
    T, D = x.shape
    tm = _pick(T, (256,))
    first_lat = M // tm
    nt = T // tm
    seg = _seg_spec(D, first_lat)
    row = pl.BlockSpec((tm, D), lambda i: (i, 0))
    gspec = pl.BlockSpec((1, D), lambda i: (0, 0))
    tspec = pl.BlockSpec((tm, D), lambda i: (jnp.maximum(i - first_lat, 0), 0))

    def body(x_ref, yp_ref, gp_ref, t_ref, g_ref, loss_ref, dx_ref, dy_ref, acc_ref):
        i = pl.program_id(0)

        @pl.when(jnp.logical_or(i == 0, i == first_lat))
        def _():
            acc_ref[...] = jnp.zeros_like(acc_ref)

        lat = jnp.where(i >= first_lat, 1.0, 0.0)
        yp = yp_ref[...].astype(F32)
        xv = x_ref[...] + gp_ref[0] * yp
        rstd = lax.rsqrt(jnp.mean(xv * xv, axis=-1, keepdims=True) + EPS)
        xn = xv * rstd
        diff = (xn * g_ref[...] - t_ref[...]) * lat
        part = 0.5 * jnp.sum(jnp.sum(diff * diff, axis=-1, keepdims=True), axis=0, keepdims=True) * (1.0 / D)
        loss_ref[0] = jnp.broadcast_to(part, (8, 128))
        dyv = diff * (1.0 / D)
        dxn = dyv * g_ref[...]
        dx = rstd * (dxn - xn * jnp.mean(dxn * xn, axis=-1, keepdims=True))
        dx_ref[...] = dx
        dy_ref[...] = (dx * gp_ref[0]).astype(dy_ref.dtype)
        acc_ref[0, 0:1, :] += jnp.sum(dyv * xn, axis=0, keepdims=True)
        acc_ref[0, 2:3, :] += jnp.sum(dx * yp, axis=0, keepdims=True)

    return pl.pallas_call(
        body, name=name, grid=(nt,),
        in_specs=[row, row, seg, tspec, gspec],
        out_specs=(pl.BlockSpec((1, 8, 128), lambda i: (i, 0, 0)), row, row,
                   pl.BlockSpec((1, 8, D), lambda i: (jnp.where(i >= first_lat, 1, 0), 0, 0))),
        out_shape=(jax.ShapeDtypeStruct((nt, 8, 128), F32), jax.ShapeDtypeStruct((T, D), F32),
                   jax.ShapeDtypeStruct((T, D), CDT), jax.ShapeDtypeStruct((2, 8, D), F32)),
        compiler_params=_cparams(("arbitrary",)),
    )(x, y_prev, gate_prev, tgt, g)


HALO = 16
CONV_TC = (1408, 512)


def _taps(uc, prev16, next16, keep_prev, keep_next):
    tm = uc.shape[0]
    u = uc.astype(F32)
    rows = lax.broadcasted_iota(jnp.int32, u.shape, 0)
    pr = prev16[HALO - 1:HALO, :].astype(F32) * keep_prev
    nx = next16[0:1, :].astype(F32) * keep_next
    um = jnp.where(rows == 0, pr, pltpu.roll(u, 1, 0))
    up = jnp.where(rows == tm - 1, nx, pltpu.roll(u, tm - 1, 0))
    return um, u, up


def _conv3(uc, prev16, next16, w, bias, keep_prev, keep_next):
    um, u, up = _taps(uc, prev16, next16, keep_prev, keep_next)
    out = w[0:1, :] * um + w[1:2, :] * u + w[2:3, :] * up
    return out if bias is None else out + bias


def _conv_specs(tm, tc, T, col):
    hb = tm // HALO
    last = T // HALO - 1
    return [
        pl.BlockSpec((tm, tc), lambda j, i: (i, col(j))),
        pl.BlockSpec((HALO, tc), lambda j, i: (jnp.maximum(i * hb - 1, 0), col(j))),
        pl.BlockSpec((HALO, tc), lambda j, i: (jnp.minimum((i + 1) * hb, last), col(j))),
    ]


def _seg_keep(i, first_lat, nt):
    keep_prev = jnp.where(jnp.logical_or(i == 0, i == first_lat), 0.0, 1.0)
    keep_next = jnp.where(jnp.logical_or(i == first_lat - 1, i == nt - 1), 0.0, 1.0)
    return keep_prev, keep_next


def _conv_gate_fwd(u, cw, cb, *, M, name, ride=None):
    T, F2 = u.shape
    Fh = F2 // 2
    tm = _pick(T, (256,))
    tc = _pick(Fh, CONV_TC)
    nf = Fh // tc
    nt = T // tm
    first_lat = M // tm

    def body(ug, ugp, ugn, uv, uvp, uvn, wg, wv, bg, bv, o_ref):
        kp, kn = _seg_keep(pl.program_id(1), first_lat, nt)
        gc = _conv3(ug[...], ugp[...], ugn[...], wg[...], bg[...], kp, kn)
        vc = _conv3(uv[...], uvp[...], uvn[...], wv[...], bv[...], kp, kn)
        o_ref[...] = (_silu(gc) * vc).astype(o_ref.dtype)

    wspec = lambda off: pl.BlockSpec((3, tc), lambda j, i: (0, j + off))
    bspec = lambda off: pl.BlockSpec((1, tc), lambda j, i: (0, j + off))
    return _pcall(
        body, name=name, grid=(nf, nt),
        in_specs=_conv_specs(tm, tc, T, lambda j: j) + _conv_specs(tm, tc, T, lambda j: j + nf)
        + [wspec(0), wspec(nf), bspec(0), bspec(nf)],
        out_specs=pl.BlockSpec((tm, tc), lambda j, i: (i, j)),
        out_shape=jax.ShapeDtypeStruct((T, Fh), CDT), sem=("parallel", "parallel"), ride=ride,
    )(u, u, u, u, u, u, cw, cw, cb, cb)


def _conv_gate_bwd(u, dact, cw, cb, *, M, name):
    T, F2 = u.shape
    Fh = F2 // 2
    tm = _pick(T, (256,))
    tc = _pick(Fh, CONV_TC)
    nf = Fh // tc
    nt = T // tm
    first_lat = M // tm

    def body(ug, ugp, ugn, uv, uvp, uvn, da, wg, wv, bg, bv, d_ref, acc_ref):
        i = pl.program_id(1)

        @pl.when(i == 0)
        def _():
            acc_ref[...] = jnp.zeros_like(acc_ref)

        kp, kn = _seg_keep(i, first_lat, nt)
        tg = _taps(ug[...], ugp[...], ugn[...], kp, kn)
        tv = _taps(uv[...], uvp[...], uvn[...], kp, kn)
        w = wg[...]
        gc = w[0:1, :] * tg[0] + w[1:2, :] * tg[1] + w[2:3, :] * tg[2] + bg[...]
        w = wv[...]
        vc = w[0:1, :] * tv[0] + w[1:2, :] * tv[1] + w[2:3, :] * tv[2] + bv[...]
        dav = da[...].astype(F32)
        act, dact_dg = _silu_pair(gc)
        for half, d, taps in ((0, dav * vc * dact_dg, tg), (1, dav * act, tv)):
            d_ref[half] = d.astype(d_ref.dtype)
            acc_ref[half, 0:1, :] += jnp.sum(d * taps[0], axis=0, keepdims=True)
            acc_ref[half, 1:2, :] += jnp.sum(d * taps[1], axis=0, keepdims=True)
            acc_ref[half, 2:3, :] += jnp.sum(d * taps[2], axis=0, keepdims=True)
            acc_ref[half, 3:4, :] += jnp.sum(d, axis=0, keepdims=True)

    wspec = lambda off: pl.BlockSpec((3, tc), lambda j, i: (0, j + off))
    bspec = lambda off: pl.BlockSpec((1, tc), lambda j, i: (0, j + off))
    return pl.pallas_call(
        body, name=name, grid=(nf, nt),
        in_specs=_conv_specs(tm, tc, T, lambda j: j) + _conv_specs(tm, tc, T, lambda j: j + nf)
        + [pl.BlockSpec((tm, tc), lambda j, i: (i, j)), wspec(0), wspec(nf), bspec(0), bspec(nf)],
        out_specs=(pl.BlockSpec((2, tm, tc), lambda j, i: (0, i, j)), pl.BlockSpec((2, 8, tc), lambda j, i: (0, 0, j))),
        out_shape=(jax.ShapeDtypeStruct((2, T, Fh), CDT), jax.ShapeDtypeStruct((2, 8, Fh), F32)),
        compiler_params=_cparams(("parallel", "arbitrary")),
    )(u, u, u, u, u, u, dact, cw, cw, cb, cb)


def _conv_t(d, cw, *, M, name):
    _, T, Fh = d.shape
    tm = _pick(T, (256,))
    tc = _pick(Fh, CONV_TC)
    nf = Fh // tc
    nt = T // tm
    first_lat = M // tm
    hb = tm // HALO
    last = T // HALO - 1

    def body(dc, dp, dn, w, o_ref):
        kp, kn = _seg_keep(pl.program_id(2), first_lat, nt)
        dm, d0, dp1 = _taps(dc[...], dp[...], dn[...], kp, kn)
        wv = w[...]
        o_ref[...] = (wv[2:3, :] * dm + wv[1:2, :] * d0 + wv[0:1, :] * dp1).astype(o_ref.dtype)

    return pl.pallas_call(
        body, name=name, grid=(2, nf, nt),
        in_specs=[pl.BlockSpec((None, tm, tc), lambda g, j, i: (g, i, j)),
                  pl.BlockSpec((None, HALO, tc), lambda g, j, i: (g, jnp.maximum(i * hb - 1, 0), j)),
                  pl.BlockSpec((None, HALO, tc), lambda g, j, i: (g, jnp.minimum((i + 1) * hb, last), j)),
                  pl.BlockSpec((3, tc), lambda g, j, i: (0, g * nf + j))],
        out_specs=pl.BlockSpec((None, tm, tc), lambda g, j, i: (g, i, j)),
        out_shape=jax.ShapeDtypeStruct((2, T, Fh), CDT),
        compiler_params=_cparams(("parallel", "parallel", "parallel")),
    )(d, d, d, cw)


def _rope_tables(N, M, HD):
    ax = HD // 2
    pos = jnp.arange(N, dtype=jnp.int32)
    row = (pos // GRID_W).astype(F32)
    col = (pos % GRID_W).astype(F32)
    inv = ROPE_BASE ** (-jnp.arange(0, ax, 2, dtype=F32) / ax)
    ar = row[:, None] * inv[None, :]
    ac = col[:, None] * inv[None, :]
    cos = jnp.concatenate([jnp.cos(ar), jnp.cos(ar), jnp.cos(ac), jnp.cos(ac)], axis=1)
    sin = jnp.concatenate([-jnp.sin(ar), jnp.sin(ar), -jnp.sin(ac), jnp.sin(ac)], axis=1)
    cos = jnp.concatenate([jnp.ones((M, HD), F32), cos], axis=0)
    sin = jnp.concatenate([jnp.zeros((M, HD), F32), sin], axis=0)
    return cos, sin


def _pair_swap(x, nf):
    w = x.shape[1]
    lane = lax.broadcasted_iota(jnp.int32, x.shape, 1)
    first = (lane % (2 * nf)) < nf
    return jnp.where(first, pltpu.roll(x, w - nf, 1), pltpu.roll(x, nf, 1))


def _rope_fwd(qkv, cos, sin, *, QW, KW, HD, name):
    T = qkv.shape[0]
    tm = _pick(T, (256,))
    nf = HD // 4

    def body(qkv_ref, c_ref, s_ref, q_ref, k_ref, v_ref):
        c = c_ref[...]
        s = s_ref[...]
        for ref, off, w in ((q_ref, 0, QW), (k_ref, QW, KW)):
            xv = qkv_ref[:, off:off + w]
            ct = jnp.tile(c, (1, w // HD))
            st = jnp.tile(s, (1, w // HD))
            ref[...] = (xv * ct + _pair_swap(xv, nf) * st).astype(ref.dtype)
        v_ref[...] = qkv_ref[:, QW + KW:QW + 2 * KW].astype(v_ref.dtype)

    tspec = pl.BlockSpec((tm, HD), lambda i: (i, 0))
    return pl.pallas_call(
        body, name=name, grid=(T // tm,),
        in_specs=[pl.BlockSpec((tm, QW + 2 * KW), lambda i: (i, 0)), tspec, tspec],
        out_specs=(pl.BlockSpec((tm, QW), lambda i: (i, 0)), pl.BlockSpec((tm, KW), lambda i: (i, 0)),
                   pl.BlockSpec((tm, KW), lambda i: (i, 0))),
        out_shape=(jax.ShapeDtypeStruct((T, QW), CDT), jax.ShapeDtypeStruct((T, KW), CDT),
                   jax.ShapeDtypeStruct((T, KW), CDT)),
        compiler_params=_cparams(("parallel",)),
    )(qkv, cos, sin)


def _rope_bwd(dq, dk, dv, cos, sin, *, HD, name):
    T, QW = dq.shape
    KW = dk.shape[1]
    tm = _pick(T, (256,))
    nf = HD // 4

    def body(dq_ref, dk_ref, dv_ref, c_ref, s_ref, o_ref):
        c = c_ref[...]
        s = s_ref[...]
        for ref, off, w in ((dq_ref, 0, QW), (dk_ref, QW, KW)):
            g = ref[...].astype(F32)
            ct = jnp.tile(c, (1, w // HD))
            st = jnp.tile(s, (1, w // HD))
            o_ref[:, off:off + w] = (g * ct + _pair_swap(g * st, nf)).astype(o_ref.dtype)
        o_ref[:, QW + KW:QW + 2 * KW] = dv_ref[...].astype(o_ref.dtype)

    tspec = pl.BlockSpec((tm, HD), lambda i: (i, 0))
    return pl.pallas_call(
        body, name=name, grid=(T // tm,),
        in_specs=[pl.BlockSpec((tm, QW), lambda i: (i, 0)), pl.BlockSpec((tm, KW), lambda i: (i, 0)),
                  pl.BlockSpec((tm, KW), lambda i: (i, 0)), tspec, tspec],
        out_specs=pl.BlockSpec((tm, QW + 2 * KW), lambda i: (i, 0)),
        out_shape=jax.ShapeDtypeStruct((T, QW + 2 * KW), CDT),
        compiler_params=_cparams(("parallel",)),
    )(dq, dk, dv, cos, sin)


def _attn_scores(q_ref, kc_ref, kp_ref, kn_ref, kx_ref, sink_ref, i, *, M, HD, nblk, nbc):
    qs = jnp.concatenate([q_ref[:, g * HD:(g + 1) * HD] for g in range(ATT_G)], axis=0)
    kall = jnp.concatenate([kc_ref[...], kp_ref[...], kn_ref[...], kx_ref[...]], axis=0)
    s = _dg(qs, kall, NT) * (HD ** -0.5)
    shape = s.shape
    r = lax.broadcasted_iota(jnp.int32, shape, 0) % BLK
    c = lax.broadcasted_iota(jnp.int32, shape, 1) - M
    far = 4 * BLK
    lat_off = jnp.where(i >= nbc, 0, far)
    lo = jnp.maximum(r, jnp.where(i - 1 >= nbc, 0, BLK)) + lat_off
    hi = jnp.minimum(r + 2 * BLK, jnp.where(i + 1 < nblk, 3 * BLK - 1, 2 * BLK - 1))
    allowed = jnp.logical_or(c < 0, jnp.logical_and(c >= lo, c <= hi))
    s = jnp.where(allowed, s, -1e30)
    sink = sink_ref[0]
    m = jnp.maximum(jnp.max(s, axis=-1, keepdims=True), sink)
    e = jnp.exp(s - m)
    es = jnp.exp(sink - m)
    inv = 1.0 / (jnp.sum(e, axis=-1, keepdims=True) + es)
    return qs, kall, e * inv, es * inv


def _attn_specs(M, HD, nblk):
    kv_blk = lambda f: pl.BlockSpec((BLK, HD), lambda h, i: (f(i), h))
    ctx = pl.BlockSpec((M, HD), lambda h, i: (0, h))
    win = [kv_blk(lambda i: jnp.maximum(i - 1, 0)), kv_blk(lambda i: i), kv_blk(lambda i: jnp.minimum(i + 1, nblk - 1))]
    qspec = pl.BlockSpec((BLK, ATT_G * HD), lambda h, i: (i, h))
    sspec = pl.BlockSpec((1, ATT_G * BLK, 1), lambda h, i: (h, 0, 0))
    return qspec, [ctx] + win, sspec


def _attn_fwd(q, k, v, sink_col, *, M, name, ride=None):
    T, QW = q.shape
    HD = QW // ATT_HEADS
    nblk = T // BLK
    nbc = M // BLK

    def body(q_ref, kc, kp, kn, kx, vc, vp, vn, vx, sink_ref, o_ref):
        i = pl.program_id(1)
        _, _, p, _ = _attn_scores(q_ref, kc, kp, kn, kx, sink_ref, i, M=M, HD=HD, nblk=nblk, nbc=nbc)
        vall = jnp.concatenate([vc[...], vp[...], vn[...], vx[...]], axis=0)
        o = _dg(p.astype(CDT), vall, NN)
        for g in range(ATT_G):
            o_ref[:, g * HD:(g + 1) * HD] = o[g * BLK:(g + 1) * BLK, :].astype(o_ref.dtype)

    qspec, kvs, sspec = _attn_specs(M, HD, nblk)
    return _pcall(
        body, name=name, grid=(ATT_KV, nblk), in_specs=[qspec] + kvs + kvs + [sspec], out_specs=qspec,
        out_shape=jax.ShapeDtypeStruct((T, QW), CDT), sem=("parallel", "parallel"), ride=ride,
    )(q, k, k, k, k, v, v, v, v, sink_col)


def _attn_bwd(q, k, v, sink_col, do, *, M, name, ride=None):
    T, QW = q.shape
    HD = QW // ATT_HEADS
    KW = ATT_KV * HD
    nblk = T // BLK
    nbc = M // BLK

    def body(q_ref, kc, kp, kn, kx, vc, vp, vn, vx, sink_ref, do_ref, dq_ref, dkc_ref, dvc_ref, dkw_ref, dvw_ref, ds_ref):
        i = pl.program_id(1)

        @pl.when(i == 0)
        def _():
            dkc_ref[...] = jnp.zeros_like(dkc_ref)
            dvc_ref[...] = jnp.zeros_like(dvc_ref)
            ds_ref[...] = jnp.zeros_like(ds_ref)

        qs, kall, p, p_sink = _attn_scores(q_ref, kc, kp, kn, kx, sink_ref, i, M=M, HD=HD, nblk=nblk, nbc=nbc)
        vall = jnp.concatenate([vc[...], vp[...], vn[...], vx[...]], axis=0)
        dos = jnp.concatenate([do_ref[:, g * HD:(g + 1) * HD] for g in range(ATT_G)], axis=0)
        dp = _dg(dos, vall, NT)
        dsum = jnp.sum(p * dp, axis=-1, keepdims=True)
        dsc = (p * (dp - dsum) * (HD ** -0.5)).astype(CDT)
        dq = _dg(dsc, kall, NN)
        dkall = _dg(dsc, qs, TN)
        dvall = _dg(p.astype(CDT), dos, TN)
        for g in range(ATT_G):
            dq_ref[:, g * HD:(g + 1) * HD] = dq[g * BLK:(g + 1) * BLK, :].astype(dq_ref.dtype)
        dkc_ref[...] += dkall[0:M]
        dvc_ref[...] += dvall[0:M]
        dkw_ref[...] = dkall[M:]
        dvw_ref[...] = dvall[M:]
        ds_ref[0] += -(p_sink * dsum)

    qspec, kvs, sspec = _attn_specs(M, HD, nblk)
    ctx_out = pl.BlockSpec((M, HD), lambda h, i: (0, h))
    win_out = pl.BlockSpec((3 * BLK, HD), lambda h, i: (i, h))
    return _pcall(
        body, name=name, grid=(ATT_KV, nblk),
        in_specs=[qspec] + kvs + kvs + [sspec, qspec],
        out_specs=(qspec, ctx_out, ctx_out, win_out, win_out, sspec),
        out_shape=(jax.ShapeDtypeStruct((T, QW), CDT), jax.ShapeDtypeStruct((M, KW), F32), jax.ShapeDtypeStruct((M, KW), F32),
                   jax.ShapeDtypeStruct((nblk * 3 * BLK, KW), F32), jax.ShapeDtypeStruct((nblk * 3 * BLK, KW), F32),
                   jax.ShapeDtypeStruct((ATT_KV, ATT_G * BLK, 1), F32)),
        sem=("parallel", "arbitrary"), ride=ride,
    )(q, k, k, k, k, v, v, v, v, sink_col, do)


def _window_combine(part, *, nbc, name):
    rows, KW = part.shape
    nblk = rows // (3 * BLK)
    nbl = nblk - nbc

    def body(a_ref, b_ref, c_ref, o_ref):
        j = pl.program_id(0)
        o_ref[...] = (a_ref[...] * jnp.where(j + 1 < nbl, 1.0, 0.0) + b_ref[...]
                      + c_ref[...] * jnp.where(j >= 1, 1.0, 0.0))

    return pl.pallas_call(
        body, name=name, grid=(nbl,),
        in_specs=[pl.BlockSpec((BLK, KW), lambda j: (3 * jnp.minimum(nbc + j + 1, nblk - 1), 0)),
                  pl.BlockSpec((BLK, KW), lambda j: (3 * (nbc + j) + 1, 0)),
                  pl.BlockSpec((BLK, KW), lambda j: (3 * jnp.maximum(nbc + j - 1, 0) + 2, 0))],
        out_specs=pl.BlockSpec((BLK, KW), lambda j: (j, 0)),
        out_shape=jax.ShapeDtypeStruct((nbl * BLK, KW), F32),
        compiler_params=_cparams(("parallel",)),
    )(part, part, part)


def _split3(x):
    hi = x.astype(CDT)
    r1 = x - hi.astype(F32)
    mid = r1.astype(CDT)
    lo = (r1 - mid.astype(F32)).astype(CDT)
    return hi, mid, lo


def _tri_sum(tri, x, terms):
    parts = _split3(x)[:terms]
    out = _dg(tri, parts[0], NN)
    for p in parts[1:]:
        out = out + _dg(tri, p, NN)
    return out


def _gla_dims(D):
    dk = D // 2 // GLA_H
    dv = D // GLA_H
    return dk, dv


def _chunk_of(s, rev, ncc, ns):
    if not rev:
        return s
    return jnp.where(s < ncc, ncc - 1 - s, ns - 1 - (s - ncc))


def _gla_chunk(q, k, g, rev, dk):
    C = q.shape[0]
    r = lax.broadcasted_iota(jnp.int32, (C, C), 0)
    c = lax.broadcasted_iota(jnp.int32, (C, C), 1)
    causal = (r <= c) if rev else (r >= c)
    b = _tri_sum(causal.astype(CDT), g, 3)
    B = b[0:1, :] if rev else b[C - 1:C, :]
    q = q.astype(F32) * (dk ** -0.5)
    k = k.astype(F32)
    return causal, b, B, q * jnp.exp(b), k * jnp.exp(-b), k * jnp.exp(B - b)


def _gla_scan_fwd(proj, g, *, rev, M, D, name):
    T = proj.shape[0]
    dk, dv = _gla_dims(D)
    C, H, Dh = CHUNK, GLA_H, D // 2
    ns = T // C
    ncc = M // C
    cm = lambda s: _chunk_of(s, rev, ncc, ns)

    def body(q_ref, k_ref, v_ref, g_ref, o_ref, st_ref, S):
        @pl.when(pl.program_id(0) == 0)
        def _():
            S[...] = jnp.zeros_like(S)

        for h in range(H):
            ks, vs = slice(h * dk, (h + 1) * dk), slice(h * dv, (h + 1) * dv)
            causal, b, B, qt, kt, kh = _gla_chunk(q_ref[:, ks], k_ref[:, ks], g_ref[:, ks], rev, dk)
            v = v_ref[:, vs]
            A = jnp.where(causal, _dg(qt.astype(CDT), kt.astype(CDT), NT), 0.0)
            Sin = S[h]
            st_ref[h] = Sin
            o_ref[:, vs] = _dg(A.astype(CDT), v, NN) + _dg(qt.astype(CDT), Sin.astype(CDT), NT)
            S[h] = Sin * jnp.exp(B) + _dg(v, kh.astype(CDT), TN)

    return pl.pallas_call(
        body, name=name, grid=(ns,),
        in_specs=[pl.BlockSpec((C, Dh), lambda s: (cm(s), 0)), pl.BlockSpec((C, Dh), lambda s: (cm(s), 1)),
                  pl.BlockSpec((C, D), lambda s: (cm(s), 1)), pl.BlockSpec((C, Dh), lambda s: (cm(s), 0))],
        out_specs=(pl.BlockSpec((C, D), lambda s: (cm(s), 0)), pl.BlockSpec((H, dv, dk), lambda s: (s, 0, 0))),
        out_shape=(jax.ShapeDtypeStruct((T, D), F32), jax.ShapeDtypeStruct((ns * H, dv, dk), F32)),
        scratch_shapes=[pltpu.VMEM((H, dv, dk), F32)],
        compiler_params=_cparams(("arbitrary",)),
    )(proj, proj, proj, g)


def _gla_scan_bwd(proj, g, st, do, *, rev, M, D, name):
    T = proj.shape[0]
    dk, dv = _gla_dims(D)
    C, H, Dh = CHUNK, GLA_H, D // 2
    ns = T // C
    ncc = M // C
    cm = lambda j: _chunk_of(ns - 1 - j, rev, ncc, ns)

    def body(q_ref, k_ref, v_ref, g_ref, st_ref, do_ref, dq_ref, dk_ref, dv_ref, dg_ref, dS):
        @pl.when(pl.program_id(0) == 0)
        def _():
            dS[...] = jnp.zeros_like(dS)

        rows = lax.broadcasted_iota(jnp.int32, (C, dk), 0)
        eye = lax.broadcasted_iota(jnp.int32, (C, C), 0) == lax.broadcasted_iota(jnp.int32, (C, C), 1)
        for h in range(H):
            ks, vs = slice(h * dk, (h + 1) * dk), slice(h * dv, (h + 1) * dv)
            causal, b, B, qt, kt, kh = _gla_chunk(q_ref[:, ks], k_ref[:, ks], g_ref[:, ks], rev, dk)
            v = v_ref[:, vs]
            dov = do_ref[:, vs]
            ST = st_ref[h]
            dSo = dS[h]
            qtb, ktb, khb = qt.astype(CDT), kt.astype(CDT), kh.astype(CDT)
            dSb = dSo.astype(CDT)
            A = jnp.where(causal, _dg(qtb, ktb, NT), 0.0).astype(CDT)
            dA = jnp.where(causal, _dg(dov, v, NT), 0.0).astype(CDT)
            dqt = _dg(dA, ktb, NN) + _dg(dov, ST.astype(CDT), NN)
            dkt = _dg(dA, qtb, TN)
            dvv = _dg(A, dov, TN) + _dg(khb, dSb, NT)
            dkh = _dg(v, dSb, NN)
            eB = jnp.exp(B)
            dB = eB * jnp.sum(ST * dSo, axis=0, keepdims=True) + jnp.sum(dkh * kh, axis=0, keepdims=True)
            db = dqt * qt - dkt * kt - dkh * kh + jnp.where(rows == (0 if rev else C - 1), dB, 0.0)
            anti = jnp.logical_not(causal) | eye
            dg_ref[:, ks] = _tri_sum(anti.astype(CDT), db, 2)
            dq_ref[:, ks] = dqt * jnp.exp(b) * (dk ** -0.5)
            dk_ref[:, ks] = dkt * jnp.exp(-b) + dkh * jnp.exp(B - b)
            dv_ref[:, vs] = dvv
            dS[h] = dSo * eB + _dg(dov, qtb, TN)

    half = pl.BlockSpec((C, Dh), lambda j: (cm(j), 0))
    full = pl.BlockSpec((C, D), lambda j: (cm(j), 0))
    return pl.pallas_call(
        body, name=name, grid=(ns,),
        in_specs=[half, pl.BlockSpec((C, Dh), lambda j: (cm(j), 1)), pl.BlockSpec((C, D), lambda j: (cm(j), 1)), half,
                  pl.BlockSpec((H, dv, dk), lambda j: (ns - 1 - j, 0, 0)), full],
        out_specs=(half, half, full, half),
        out_shape=(jax.ShapeDtypeStruct((T, Dh), F32), jax.ShapeDtypeStruct((T, Dh), F32),
                   jax.ShapeDtypeStruct((T, D), F32), jax.ShapeDtypeStruct((T, Dh), F32)),
        scratch_shapes=[pltpu.VMEM((H, dv, dk), F32)],
        compiler_params=_cparams(("arbitrary",)),
    )(proj, proj, proj, g, st, do)


def _log_sigmoid_parts(z):
    t = jnp.exp(-jnp.abs(z))
    return jnp.minimum(z, 0.0) - jnp.log(1.0 + t), jnp.where(z >= 0, t / (1.0 + t), 1.0 / (1.0 + t))


def _gla_gate_fwd(lr, w2, bias, *, D, name):
    T = lr.shape[0]
    tm = _pick(T, (256,))
    Dh = D // 2

    def body(lr_ref, w_ref, b_ref, gf_ref, gb_ref):
        z = _dg(lr_ref[...], w_ref[...], NN) + b_ref[...]
        g, _ = _log_sigmoid_parts(z)
        g = g * (1.0 / GATE_NORM)
        gf_ref[...] = g[:, 0:Dh]
        gb_ref[...] = g[:, Dh:D]

    half = pl.BlockSpec((tm, Dh), lambda i: (i, 0))
    return pl.pallas_call(
        body, name=name, grid=(T // tm,),
        in_specs=[pl.BlockSpec((tm, 128), lambda i: (i, 0)), pl.BlockSpec((128, D), lambda i: (0, 0)),
                  pl.BlockSpec((1, D), lambda i: (0, 0))],
        out_specs=(half, half),
        out_shape=(jax.ShapeDtypeStruct((T, Dh), F32), jax.ShapeDtypeStruct((T, Dh), F32)),
        compiler_params=_cparams(("parallel",)),
    )(lr, w2, bias)


def _gla_proj_bwd(lr, w2, bias, dqf, dkf, dvf, dgf, dqb, dkb, dvb, dgb, dr, *, D, name):
    T = lr.shape[0]
    tm = _pick(T, (256,))
    Dh = D // 2

    def body(lr_ref, w_ref, b_ref, dqf_r, dkf_r, dvf_r, dgf_r, dqb_r, dkb_r, dvb_r, dgb_r, dr_ref, dp_ref, dl_ref, dw_ref, db_ref):
        @pl.when(pl.program_id(0) == 0)
        def _():
            dw_ref[...] = jnp.zeros_like(dw_ref)
            db_ref[...] = jnp.zeros_like(db_ref)

        lr = lr_ref[...]
        z = _dg(lr, w_ref[...], NN) + b_ref[...]
        _, sneg = _log_sigmoid_parts(z)
        dz = jnp.concatenate([dgf_r[...], dgb_r[...]], axis=1) * sneg * (1.0 / GATE_NORM)
        dzb = dz.astype(CDT)
        dp_ref[:, 0:Dh] = (dqf_r[...] + dqb_r[...]).astype(dp_ref.dtype)
        dp_ref[:, Dh:D] = (dkf_r[...] + dkb_r[...]).astype(dp_ref.dtype)
        dp_ref[:, D:2 * D] = (dvf_r[...] + dvb_r[...]).astype(dp_ref.dtype)
        dp_ref[:, 2 * D:3 * D] = dr_ref[...]
        dl_ref[...] = _dg(dzb, w_ref[...], NT).astype(dl_ref.dtype)
        dw_ref[...] += _dg(lr, dzb, TN)
        db_ref[0:1, :] += jnp.sum(dz, axis=0, keepdims=True)

    half = pl.BlockSpec((tm, Dh), lambda i: (i, 0))
    full = pl.BlockSpec((tm, D), lambda i: (i, 0))
    return pl.pallas_call(
        body, name=name, grid=(T // tm,),
        in_specs=[pl.BlockSpec((tm, 128), lambda i: (i, 0)), pl.BlockSpec((128, D), lambda i: (0, 0)),
                  pl.BlockSpec((1, D), lambda i: (0, 0)), half, half, full, half, half, half, full, half, full],
        out_specs=(pl.BlockSpec((tm, 3 * D), lambda i: (i, 0)), pl.BlockSpec((tm, 128), lambda i: (i, 0)),
                   pl.BlockSpec((128, D), lambda i: (0, 0)), pl.BlockSpec((8, D), lambda i: (0, 0))),
        out_shape=(jax.ShapeDtypeStruct((T, 3 * D), CDT), jax.ShapeDtypeStruct((T, 128), CDT),
                   jax.ShapeDtypeStruct((128, D), F32), jax.ShapeDtypeStruct((8, D), F32)),
        compiler_params=_cparams(("arbitrary",)),
    )(lr, w2, bias, dqf, dkf, dvf, dgf, dqb, dkb, dvb, dgb, dr)


def _gla_out_fwd(of, ob, proj, gn, *, D, name):
    T = of.shape[0]
    tm = _pick(T, (256,))
    dv = D // GLA_H

    def body(of_ref, ob_ref, r_ref, g_ref, y_ref):
        for h in range(GLA_H):
            sl = slice(h * dv, (h + 1) * dv)
            o = of_ref[:, sl] + ob_ref[:, sl]
            rstd = lax.rsqrt(jnp.mean(o * o, axis=-1, keepdims=True) + EPS)
            y_ref[:, sl] = (o * rstd * g_ref[...] * _silu(r_ref[:, sl].astype(F32))).astype(y_ref.dtype)

    full = pl.BlockSpec((tm, D), lambda i: (i, 0))
    return pl.pallas_call(
        body, name=name, grid=(T // tm,),
        in_specs=[full, full, pl.BlockSpec((tm, D), lambda i: (i, 2)), pl.BlockSpec((1, dv), lambda i: (0, 0))],
        out_specs=full, out_shape=jax.ShapeDtypeStruct((T, D), CDT),
        compiler_params=_cparams(("parallel",)),
    )(of, ob, proj, gn)


def _gla_out_bwd(of, ob, proj, gn, dy, *, D, name):
    T = of.shape[0]
    tm = _pick(T, (256,))
    dv = D // GLA_H

    def body(of_ref, ob_ref, r_ref, g_ref, dy_ref, do_ref, dr_ref, dg_ref):
        @pl.when(pl.program_id(0) == 0)
        def _():
            dg_ref[...] = jnp.zeros_like(dg_ref)

        gv = g_ref[...]
        for h in range(GLA_H):
            sl = slice(h * dv, (h + 1) * dv)
            o = of_ref[:, sl] + ob_ref[:, sl]
            rstd = lax.rsqrt(jnp.mean(o * o, axis=-1, keepdims=True) + EPS)
            oh = o * rstd
            r = r_ref[:, sl].astype(F32)
            dyv = dy_ref[:, sl].astype(F32)
            act, dact_dr = _silu_pair(r)
            don = dyv * act
            dr_ref[:, sl] = (dyv * oh * gv * dact_dr).astype(dr_ref.dtype)
            dg_ref[0:1, :] += jnp.sum(don * oh, axis=0, keepdims=True)
            dn = don * gv
            do_ref[:, sl] = (rstd * (dn - oh * jnp.mean(dn * oh, axis=-1, keepdims=True))).astype(do_ref.dtype)

    full = pl.BlockSpec((tm, D), lambda i: (i, 0))
    return pl.pallas_call(
        body, name=name, grid=(T // tm,),
        in_specs=[full, full, pl.BlockSpec((tm, D), lambda i: (i, 2)), pl.BlockSpec((1, dv), lambda i: (0, 0)), full],
        out_specs=(full, full, pl.BlockSpec((8, dv), lambda i: (0, 0))),
        out_shape=(jax.ShapeDtypeStruct((T, D), CDT), jax.ShapeDtypeStruct((T, D), CDT), jax.ShapeDtypeStruct((8, dv), F32)),
        compiler_params=_cparams(("arbitrary",)),
    )(of, ob, proj, gn, dy)


def _adamw(w, g, m, v, *, name, ride=None):
    R, Cc = w.shape
    tr = R
    for cand in (512, 256, 128, 64, 32, 16, 8):
        if R % cand == 0 and cand * Cc * 4 <= 2 * 1024 * 1024:
            tr = cand
            break

    def body(w_ref, g_ref, m_ref, v_ref, d_ref, mo_ref, vo_ref):
        gv = g_ref[...]
        mn = B1 * m_ref[...] + (1.0 - B1) * gv
        vn = B2 * v_ref[...] + (1.0 - B2) * (gv * gv)
        mh = mn / (1.0 - B1 ** STEP)
        vh = vn / (1.0 - B2 ** STEP)
        d_ref[...] = -LR * (mh / (jnp.sqrt(vh) + AEPS) + WD * w_ref[...])
        mo_ref[...] = mn
        vo_ref[...] = vn

    spec = pl.BlockSpec((tr, Cc), lambda i: (i, 0))
    sh = jax.ShapeDtypeStruct((R, Cc), F32)
    return _pcall(
        body, name=name, grid=(R // tr,), in_specs=[spec] * 4, out_specs=(spec,) * 3, out_shape=(sh,) * 3,
        sem=("parallel",), ride=ride,
    )(w, g, m, v)


def _attn_layer_fwd(h, w, tabs, M, tag, rides):
    cos, sin = tabs
    QW = 4 * w["w_o"].shape[1]
    HD = QW // ATT_HEADS
    KW = ATT_KV * HD
    qkv = _mm(h, w["w_qkv"], b_chip=1, out_dtype=F32, ride=rides.get("mix"), name=f"{tag}_qkv")
    q, k, v = _rope_fwd(qkv, cos, sin, QW=QW, KW=KW, HD=HD, name=f"{tag}_rope")
    sink_col = jnp.repeat(w["sink"].astype(F32), BLK).reshape(ATT_KV, ATT_G * BLK, 1)
    o = _attn_fwd(q, k, v, sink_col, M=M, ride=rides.get("attn"), name=f"{tag}_attn")
    y = _mm(o, w["w_o"], b_chip=0, out_dtype=CDT, name=f"{tag}_wo")
    return y, dict(h=h, q=q, k=k, v=v, o=o, sink_col=sink_col)


def _attn_layer_bwd(dy, sv, w, tabs, M, tag, ride=None):
    cos, sin = tabs
    QW = 4 * w["w_o"].shape[1]
    HD = QW // ATT_HEADS
    do = _mm(dy, w["w_o"], tb=True, b_chip=0, out_dtype=CDT, name=f"{tag}_dwo_x")
    g = {"w_o": _mm(sv["o"], dy, ta=True, out_chip=0, out_dtype=CDT, name=f"{tag}_dwo_w")}
    dq, dkc, dvc, dkw, dvw, dsink = _attn_bwd(sv["q"], sv["k"], sv["v"], sv["sink_col"], do, M=M, ride=ride, name=f"{tag}_attn_bwd")
    nbc = M // BLK
    dk = jnp.concatenate([dkc, _window_combine(dkw, nbc=nbc, name=f"{tag}_dk_comb")], axis=0)
    dv = jnp.concatenate([dvc, _window_combine(dvw, nbc=nbc, name=f"{tag}_dv_comb")], axis=0)
    dqkv = _rope_bwd(dq, dk, dv, cos, sin, HD=HD, name=f"{tag}_rope_bwd")
    dh = _mm(dqkv, w["w_qkv"], tb=True, b_chip=1, out_dtype=CDT, name=f"{tag}_dqkv_x")
    g["w_qkv"] = _mm(sv["h"], dqkv, ta=True, out_chip=1, out_dtype=CDT, name=f"{tag}_dqkv_w")
    g["sink"] = jnp.sum(dsink.reshape(ATT_HEADS, BLK), axis=1)
    return dh, g


def _gla_layer_fwd(h, w, M, tag, rides):
    D = h.shape[1]
    proj = _mm(h, w["w_in"], b_chip=1, out_dtype=CDT, ride=rides.get("mix"), name=f"{tag}_in")
    lr = _mm(h, w["w1x"], out_dtype=CDT, name=f"{tag}_in_gate")
    gf, gb = _gla_gate_fwd(lr, w["w2"], w["gbias"], D=D, name=f"{tag}_gate")
    of, stf = _gla_scan_fwd(proj, gf, rev=False, M=M, D=D, name=f"{tag}_scan_f")
    ob, stb = _gla_scan_fwd(proj, gb, rev=True, M=M, D=D, name=f"{tag}_scan_b")
    yg = _gla_out_fwd(of, ob, proj, w["onorm"], D=D, name=f"{tag}_out")
    y = _mm(yg, w["w_o"], b_chip=0, out_dtype=CDT, name=f"{tag}_wo")
    return y, dict(h=h, proj=proj, lr=lr, gf=gf, gb=gb, of=of, ob=ob, stf=stf, stb=stb, yg=yg)


def _gla_layer_bwd(dy, sv, w, M, tag):
    D = dy.shape[1]
    dyg = _mm(dy, w["w_o"], tb=True, b_chip=0, out_dtype=CDT, name=f"{tag}_dwo_x")
    g = {"w_o": _mm(sv["yg"], dy, ta=True, out_chip=0, out_dtype=CDT, name=f"{tag}_dwo_w")}
    do, dr, dgn = _gla_out_bwd(sv["of"], sv["ob"], sv["proj"], w["onorm"], dyg, D=D, name=f"{tag}_out_bwd")
    df = _gla_scan_bwd(sv["proj"], sv["gf"], sv["stf"], do, rev=False, M=M, D=D, name=f"{tag}_scan_f_bwd")
    db = _gla_scan_bwd(sv["proj"], sv["gb"], sv["stb"], do, rev=True, M=M, D=D, name=f"{tag}_scan_b_bwd")
    dproj, dlr, dw2, dbias = _gla_proj_bwd(sv["lr"], w["w2"], w["gbias"], *df, *db, dr, D=D, name=f"{tag}_proj_bwd")
    dh = _mm(dproj, w["w_in"], tb=True, b_chip=1, extra=(dlr, w["w1x"]), out_dtype=CDT, name=f"{tag}_din_x")
    g["w_in"] = _mm(sv["h"], dproj, ta=True, out_chip=1, out_dtype=CDT, name=f"{tag}_din_w")
    g["w1x"] = _mm(sv["h"], dlr, ta=True, out_dtype=F32, name=f"{tag}_din_gate_w")
    g["w2"] = dw2
    g["gbias"] = dbias[0]
    g["onorm"] = dgn[0]
    return dh, g


def _ffn_fwd(h2, w, M, tag, rides):
    u = _mm(h2, w["w_up"], b_chip=1, out_dtype=CDT, ride=rides.get("up"), name=f"{tag}_up")
    act = _conv_gate_fwd(u, w["conv_w"], w["conv_b"], M=M, ride=rides.get("conv"), name=f"{tag}_conv")
    f = _mm(act, w["w_down"], b_chip=0, out_dtype=CDT, ride=rides.get("down"), name=f"{tag}_down")
    return f, dict(h2=h2, u=u, act=act)


def _ffn_bwd(dyf, sv, w, M, tag, rides):
    dact = _mm(dyf, w["w_down"], tb=True, b_chip=0, out_dtype=CDT, ride=rides.get("ddown_x"), name=f"{tag}_ddown_x")
    g = {"w_down": _mm(sv["act"], dyf, ta=True, out_chip=0, out_dtype=CDT, ride=rides.get("ddown_w"), name=f"{tag}_ddown_w")}
    duc, cacc = _conv_gate_bwd(sv["u"], dact, w["conv_w"], w["conv_b"], M=M, name=f"{tag}_conv_bwd")
    du = _conv_t(duc, w["conv_w"], M=M, name=f"{tag}_conv_t")
    dh2 = _mm(du, w["w_up"], tb=True, a_split=True, b_chip=1, out_dtype=CDT, ride=rides.get("dup_x"), name=f"{tag}_dup_x")
    g["w_up"] = _mm(sv["h2"], du, ta=True, b_chip=1, out_chip=1, out_dtype=CDT, name=f"{tag}_dup_w")
    g["conv_w"] = jnp.concatenate([cacc[0, 0:3], cacc[1, 0:3]], axis=1)
    g["conv_b"] = jnp.concatenate([cacc[0, 3], cacc[1, 3]], axis=0)
    return dh2, g


def _norm_grads(acc, gain, scale):
    p = acc[:, 1]
    return acc[:, 0], p * gain, jnp.sum(p * (1.0 + scale[:, 0]), axis=0)


def _local_step(x, tgt, mods, weights_of, final_g, *, M, fwd_rides=None, bwd_rides=None, on_ffn_grads=None, on_grads=None):
    T, D = x.shape
    L = mods.shape[0]
    HD = D // ATT_HEADS
    tabs = _rope_tables(T - M, M, HD)
    sel = lambda i, k: mods[i][:, k:k + 1, :]
    saved = []
    xs, y_prev, gate_prev = x, None, None
    for i in range(L):
        w = weights_of(i, "mixer")
        rides = {} if fwd_rides is None else fwd_rides(i)
        x_in, h = _norm_fwd(xs, y_prev, gate_prev, w["g_mix"], sel(i, 0), sel(i, 1), M=M, name=f"l{i}_norm_mix")
        if "w_qkv" in w:
            y_mix, sm = _attn_layer_fwd(h, w, tabs, M, f"l{i}", rides)
        else:
            y_mix, sm = _gla_layer_fwd(h, w, M, f"l{i}", rides)
        w = {**w, **weights_of(i, "ffn")}
        x_mid, h2 = _norm_fwd(x_in, y_mix, sel(i, 2), w["g_ffn"], sel(i, 3), sel(i, 4), M=M, name=f"l{i}_norm_ffn")
        f, sf = _ffn_fwd(h2, w, M, f"l{i}", rides)
        saved.append(dict(x_in=x_in, x_mid=x_mid, y_mix=y_mix, f=f, sm=sm, sf=sf, w=w))
        xs, y_prev, gate_prev = x_mid, f, sel(i, 5)

    loss_parts, dx, dyf, acc = _final_loss(xs, y_prev, gate_prev, tgt, final_g, M=M, name="final_loss")
    loss = jnp.sum(loss_parts[:, 0, 0])
    d_final_g = acc[0, 0] + acc[1, 0]
    dmods = [None] * L
    grads = [None] * L
    dgate_ffn = acc[:, 2]
    for i in reversed(range(L)):
        sv = saved[i]
        w = sv["w"]
        dh2, g = _ffn_bwd(dyf, sv["sf"], w, M, f"l{i}", {} if bwd_rides is None else bwd_rides(i))
        dx, dy_mix, acc = _norm_bwd(sv["x_mid"], dh2, dx, w["g_ffn"], sel(i, 4), sv["y_mix"], sel(i, 2), M=M, name=f"l{i}_norm_ffn_bwd")
        dsh_f, dsc_f, g["g_ffn"] = _norm_grads(acc, w["g_ffn"], sel(i, 4))
        dgate_mix = acc[:, 2]
        if "w_qkv" in w:
            dh, gm = _attn_layer_bwd(dy_mix, sv["sm"], w, tabs, M, f"l{i}", None if on_ffn_grads is None else on_ffn_grads(i, g))
        else:
            dh, gm = _gla_layer_bwd(dy_mix, sv["sm"], w, M, f"l{i}")
        g.update(gm)
        if i > 0:
            dx, dyf, acc = _norm_bwd(sv["x_in"], dh, dx, w["g_mix"], sel(i, 1), saved[i - 1]["f"], sel(i - 1, 5), M=M, name=f"l{i}_norm_mix_bwd")
        else:
            dx, dyf, acc = _norm_bwd(sv["x_in"], dh, dx, w["g_mix"], sel(i, 1), None, None, M=M, name=f"l{i}_norm_mix_bwd")
        dsh_m, dsc_m, g["g_mix"] = _norm_grads(acc, w["g_mix"], sel(i, 1))
        dmods[i] = jnp.stack([dsh_m, dsc_m, dgate_mix, dsh_f, dsc_f, dgate_ffn], axis=1)
        dgate_ffn = acc[:, 2]
        grads[i] = g if on_grads is None else on_grads(i, g)
    return loss, dx, jnp.stack(dmods, axis=0), grads, d_final_g


ANY = pl.BlockSpec(memory_space=pl.ANY)


def _me():
    return lax.axis_index("x"), lax.axis_index("y"), lax.axis_index("c")


def _other_chips(mx, my):
    return [(1 - mx, my), (mx, 1 - my), (1 - mx, 1 - my)]


def _rcopy(src, dst, sems, k, dev):
    send_sems, recv_sems = sems
    return pltpu.make_async_remote_copy(src_ref=src, dst_ref=dst, send_sem=send_sems.at[k], recv_sem=recv_sems.at[k],
                                        device_id=dev, device_id_type=MESH)


def _all_gather8(x, *, name):
    m, n = x.shape

    def body(x_ref, out_ref, send_sems, recv_sems, local_sem):
        mx, my, mc = _me()
        sems = (send_sems, recv_sems)
        me, sib = (mx, my, mc), (mx, my, 1 - mc)
        chips = _other_chips(mx, my)
        blk = lambda d: out_ref.at[4 * d[0] + 2 * d[1] + d[2]]
        mine = pltpu.make_async_copy(x_ref, blk(me), local_sem)
        mine.start()
        first = [_rcopy(x_ref, blk(me), sems, 0, sib)]
        first += [_rcopy(x_ref, blk(me), sems, 1 + j, (*ch, mc)) for j, ch in enumerate(chips)]
        for cp in first:
            cp.start()
        passed = [_rcopy(blk((*ch, mc)), blk((*ch, mc)), sems, 4 + j, sib) for j, ch in enumerate(chips)]
        for j, ch in enumerate(chips):
            _rcopy(x_ref, blk((*ch, mc)), sems, 1 + j, me).wait_recv()
            passed[j].start()
        _rcopy(x_ref, blk(sib), sems, 0, me).wait_recv()
        for j, ch in enumerate(chips):
            _rcopy(x_ref, blk((*ch, 1 - mc)), sems, 4 + j, me).wait_recv()
        for cp in first + passed:
            cp.wait_send()
        mine.wait()

    return pl.pallas_call(
        body, name=name, out_shape=jax.ShapeDtypeStruct((8, m, n), x.dtype), in_specs=[ANY], out_specs=ANY,
        scratch_shapes=[pltpu.SemaphoreType.DMA((7,)), pltpu.SemaphoreType.DMA((7,)), pltpu.SemaphoreType.DMA],
    )(x)


ROW_TILES = (512, 352, 256, 128)


def _sem_pairs(n):
    return [pltpu.SemaphoreType.DMA((n,)), pltpu.SemaphoreType.DMA((n,))]


def _place(w, layer, pos, *, name):
    _, a, b = w.shape
    tr = _pick(a, ROW_TILES)

    def body(pos_ref, w_ref, o_ref):
        o_ref[...] = w_ref[...].astype(o_ref.dtype)

    return pl.pallas_call(
        body, name=name, out_shape=jax.ShapeDtypeStruct((4, a, b), CDT),
        grid_spec=pltpu.PrefetchScalarGridSpec(
            num_scalar_prefetch=1, grid=(a // tr,),
            in_specs=[pl.BlockSpec((None, tr, b), lambda i, pos: (layer, i, 0))],
            out_specs=pl.BlockSpec((None, tr, b), lambda i, pos: (pos[0], i, 0))),
        compiler_params=_cparams(("parallel",)),
    )(pos, w)


def _gather_layer(bufs, *, name):
    n = len(bufs)

    def body(*refs):
        outs = refs[n:2 * n]
        sems = (refs[2 * n], refs[2 * n + 1])
        mx, my, mc = _me()
        me, sib = (mx, my, mc), (mx, my, 1 - mc)
        chips = _other_chips(mx, my)
        p = 2 * mx + my
        qs = [2 * ch[0] + ch[1] for ch in chips]
        halves = [(pl.ds(mc * (o.shape[1] // 2), o.shape[1] // 2), pl.ds((1 - mc) * (o.shape[1] // 2), o.shape[1] // 2)) for o in outs]
        first = []
        for t, o in enumerate(outs):
            mine = halves[t][0]
            first += [_rcopy(o.at[p, mine], o.at[p, mine], sems, 6 * t + j, (*ch, mc)) for j, ch in enumerate(chips)]
        for cp in first:
            cp.start()
        passed = []
        for j in range(3):
            for t, o in enumerate(outs):
                mine = halves[t][0]
                _rcopy(o.at[qs[j], mine], o.at[qs[j], mine], sems, 6 * t + j, me).wait_recv()
                fwd = _rcopy(o.at[qs[j], mine], o.at[qs[j], mine], sems, 6 * t + 3 + j, sib)
                fwd.start()
                passed.append(fwd)
        for j in range(3):
            for t, o in enumerate(outs):
                theirs = halves[t][1]
                _rcopy(o.at[qs[j], theirs], o.at[qs[j], theirs], sems, 6 * t + 3 + j, me).wait_recv()
        for cp in first + passed:
            cp.wait_send()

    return pl.pallas_call(
        body, name=name, out_shape=[jax.ShapeDtypeStruct(b.shape, b.dtype) for b in bufs],
        in_specs=[ANY] * n, out_specs=[ANY] * n, input_output_aliases={t: t for t in range(n)},
        scratch_shapes=_sem_pairs(6 * n),
    )(*bufs)


def _gather_ici_plan(outs):
    mx, my, mc = _me()
    p = 2 * mx + my
    for t, o in enumerate(outs):
        ah = o.shape[1] // 2
        mine = pl.ds(mc * ah, ah)
        for j, ch in enumerate(_other_chips(mx, my)):
            yield 3 * t + j, o.at[p, mine], o.at[2 * ch[0] + ch[1], mine], (*ch, mc)


def _ride_gather(bufs):
    def start(r_in, r_out, sems):
        for k, src, _, dev in _gather_ici_plan(r_out):
            _rcopy(src, src, sems, k, dev).start()

    def finish(r_in, r_out, sems):
        for k, _, land, _ in _gather_ici_plan(r_out):
            _rcopy(land, land, sems, k, _me()).wait_recv()
        for k, src, _, dev in _gather_ici_plan(r_out):
            _rcopy(src, src, sems, k, dev).wait_send()

    return dict(ins=list(bufs), outs=[jax.ShapeDtypeStruct(b.shape, b.dtype) for b in bufs],
                alias={t: t for t in range(len(bufs))}, nsem=3 * len(bufs), start=start, finish=finish)


def _gather_d2d(bufs, *, name):
    n = len(bufs)

    def body(*refs):
        outs = refs[n:2 * n]
        sems = (refs[2 * n], refs[2 * n + 1])
        mx, my, mc = _me()
        sib = (mx, my, 1 - mc)
        qs = [2 * ch[0] + ch[1] for ch in _other_chips(mx, my)]
        cps = []
        for t, o in enumerate(outs):
            ah = o.shape[1] // 2
            for j, q in enumerate(qs):
                mine = o.at[q, pl.ds(mc * ah, ah)]
                cps.append((_rcopy(mine, mine, sems, 3 * t + j, sib), o.at[q, pl.ds((1 - mc) * ah, ah)], 3 * t + j))
        for cp, _, _ in cps:
            cp.start()
        for _, theirs, k in cps:
            _rcopy(theirs, theirs, sems, k, sib).wait_recv()
        for cp, _, _ in cps:
            cp.wait_send()

    return pl.pallas_call(
        body, name=name, out_shape=[jax.ShapeDtypeStruct(b.shape, b.dtype) for b in bufs],
        in_specs=[ANY] * n, out_specs=[ANY] * n, input_output_aliases={t: t for t in range(n)},
        scratch_shapes=_sem_pairs(3 * n),
    )(*bufs)


def _scatter_plan(ins, outs):
    mx, my, mc = _me()
    p = 2 * mx + my
    for t, (s, o) in enumerate(zip(ins, outs)):
        for j, ch in enumerate(_other_chips(mx, my)):
            q = 2 * ch[0] + ch[1]
            yield 3 * t + j, s.at[q], o.at[p], o.at[q], (*ch, mc)


def _ride_scatter(parts):
    def start(r_in, r_out, sems):
        for k, src, dst, _, dev in _scatter_plan(r_in, r_out):
            _rcopy(src, dst, sems, k, dev).start()

    def finish(r_in, r_out, sems):
        for k, _, _, land, _ in _scatter_plan(r_in, r_out):
            _rcopy(land, land, sems, k, _me()).wait_recv()
        for k, src, dst, _, dev in _scatter_plan(r_in, r_out):
            _rcopy(src, dst, sems, k, dev).wait_send()

    return dict(ins=list(parts), outs=[jax.ShapeDtypeStruct(s.shape, s.dtype) for s in parts], alias={},
                nsem=3 * len(parts), start=start, finish=finish)


def _rs_split(gs, *, name):
    n = len(gs)

    def body(*refs):
        ins, outs = refs[:n], refs[n:2 * n]
        sems = (refs[2 * n], refs[2 * n + 1])
        mx, my, mc = _me()
        sib = (mx, my, 1 - mc)
        cps = []
        for t, (g, o) in enumerate(zip(ins, outs)):
            ah = g.shape[1] // 2
            cps.append(_rcopy(g.at[:, pl.ds((1 - mc) * ah, ah), :], o, sems, t, sib))
        for cp in cps:
            cp.start()
        for cp in cps:
            cp.wait_recv()
        for cp in cps:
            cp.wait_send()

    return pl.pallas_call(
        body, name=name, out_shape=[jax.ShapeDtypeStruct((4, g.shape[1] // 2, g.shape[2]), g.dtype) for g in gs],
        in_specs=[ANY] * n, out_specs=[ANY] * n, scratch_shapes=_sem_pairs(n),
    )(*gs)


def _rs_add(g, got, pos, *, name):
    _, a, b = g.shape
    ah = a // 2
    tr = _pick(ah, ROW_TILES)
    nb = ah // tr

    def body(pos_ref, g_ref, r_ref, o_ref):
        o_ref[...] = (g_ref[...].astype(F32) + r_ref[...].astype(F32)).astype(o_ref.dtype)

    blk = pl.BlockSpec((None, tr, b), lambda q, i, pos: (q, i, 0))
    return pl.pallas_call(
        body, name=name, out_shape=jax.ShapeDtypeStruct((4, ah, b), g.dtype),
        grid_spec=pltpu.PrefetchScalarGridSpec(
            num_scalar_prefetch=1, grid=(4, nb),
            in_specs=[pl.BlockSpec((None, tr, b), lambda q, i, pos: (q, pos[1] * nb + i, 0)), blk], out_specs=blk),
        compiler_params=_cparams(("parallel", "parallel")),
    )(pos, g, got)


def _rs_scatter(ps, *, name):
    n = len(ps)

    def body(*refs):
        ins, outs = refs[:n], refs[n:2 * n]
        sems = (refs[2 * n], refs[2 * n + 1])
        mx, my, mc = _me()
        me = (mx, my, mc)
        chips = _other_chips(mx, my)
        p = 2 * mx + my
        sends = []
        for t, (s, o) in enumerate(zip(ins, outs)):
            sends += [_rcopy(s.at[2 * ch[0] + ch[1]], o.at[p], sems, 3 * t + j, (*ch, mc)) for j, ch in enumerate(chips)]
        for cp in sends:
            cp.start()
        for t, (s, o) in enumerate(zip(ins, outs)):
            for j, ch in enumerate(chips):
                _rcopy(s.at[p], o.at[2 * ch[0] + ch[1]], sems, 3 * t + j, me).wait_recv()
        for cp in sends:
            cp.wait_send()

    return pl.pallas_call(
        body, name=name, out_shape=[jax.ShapeDtypeStruct(s.shape, s.dtype) for s in ps],
        in_specs=[ANY] * n, out_specs=[ANY] * n, scratch_shapes=_sem_pairs(3 * n),
    )(*ps)


def _rs_sum(part, recv, buf, layer, pos, *, name):
    _, ah, b = part.shape
    tr = _pick(ah, ROW_TILES)
    nb = ah // tr

    def body(pos_ref, p_ref, r0, r1, r2, buf_ref, o_ref):
        o_ref[...] = ((p_ref[...].astype(F32) + r0[...].astype(F32)) + r1[...].astype(F32)) + r2[...].astype(F32)

    other = lambda k: pl.BlockSpec((None, tr, b), lambda i, pos: (jnp.where(pos[0] <= k, k + 1, k), i, 0))
    return pl.pallas_call(
        body, name=name, out_shape=jax.ShapeDtypeStruct(buf.shape, buf.dtype),
        grid_spec=pltpu.PrefetchScalarGridSpec(
            num_scalar_prefetch=1, grid=(nb,),
            in_specs=[pl.BlockSpec((None, tr, b), lambda i, pos: (pos[0], i, 0)), other(0), other(1), other(2), ANY],
            out_specs=pl.BlockSpec((None, tr, b), lambda i, pos: (layer, pos[1] * nb + i, 0))),
        input_output_aliases={5: 0},
        compiler_params=_cparams(("parallel",)),
    )(pos, part, recv, recv, recv, buf)


def _rs_share(bufs, layers, *, name):
    n = len(bufs)

    def body(*refs):
        outs = refs[n:2 * n]
        sems = (refs[2 * n], refs[2 * n + 1])
        mx, my, mc = _me()
        sib = (mx, my, 1 - mc)
        cps = []
        for t, o in enumerate(outs):
            ah = o.shape[1] // 2
            mine = o.at[layers[t], pl.ds(mc * ah, ah)]
            cps.append((_rcopy(mine, mine, sems, t, sib), o.at[layers[t], pl.ds((1 - mc) * ah, ah)]))
        for cp, _ in cps:
            cp.start()
        for t, (cp, theirs) in enumerate(cps):
            _rcopy(theirs, theirs, sems, t, sib).wait_recv()
        for cp, _ in cps:
            cp.wait_send()

    return pl.pallas_call(
        body, name=name, out_shape=[jax.ShapeDtypeStruct(b.shape, b.dtype) for b in bufs],
        in_specs=[ANY] * n, out_specs=[ANY] * n, input_output_aliases={t: t for t in range(n)},
        scratch_shapes=_sem_pairs(n),
    )(*bufs)


def _sum_lead(a, *, name):
    n, R, W = a.shape
    tr = _pick(R, (PACK_ROWS,))
    specs = [pl.BlockSpec((1, tr, W), functools.partial(lambda i, q: (q, i, 0), q=q)) for q in range(n)]

    def body(*refs):
        acc = refs[0][0].astype(F32)
        for r in refs[1:n]:
            acc = acc + r[0].astype(F32)
        refs[n][...] = acc

    return pl.pallas_call(
        body, name=name, grid=(R // tr,), in_specs=specs, out_specs=pl.BlockSpec((tr, W), lambda i: (i, 0)),
        out_shape=jax.ShapeDtypeStruct((R, W), F32), compiler_params=_cparams(("parallel",)),
    )(*([a] * n))


SMALL_SHARDED = (("ffn_conv_w", 2), ("gla_gf_w1", 1), ("gla_gf_w2", 2), ("gla_gf_b", 1), ("gla_gb_w1", 1), ("gla_gb_w2", 2),
                 ("gla_gb_b", 1), ("gla_onorm_g", 1))


def _rows_of(flat, width):
    rows = -(-flat.shape[0] // (8 * width)) * 8
    return jnp.pad(flat, (0, rows * width - flat.shape[0])).reshape(rows, width)


def _size(shape):
    n = 1
    for s in shape:
        n *= s
    return n


def _gather_small(shards):
    flat = jnp.concatenate([shards[name].astype(F32).reshape(-1) for name, _ in SMALL_SHARDED])
    got = _all_gather8(_rows_of(flat, SMALL_W), name="gather_small_w")[0::2].reshape(4, -1)
    full, off = {}, 0
    for name, ax in SMALL_SHARDED:
        shape = shards[name].shape
        n = _size(shape)
        seg = jnp.moveaxis(got[:, off:off + n].reshape((4,) + shape), 0, ax)
        full[name] = seg.reshape(shape[:ax] + (4 * shape[ax],) + shape[ax + 1:])
        off += n
    return full


WEIGHTS = ("c_ctx", "ada_w", "ada_b", "norm_mix_g", "norm_ffn_g", "ffn_w_up", "ffn_conv_w", "ffn_conv_b", "ffn_w_down",
           "attn_w_qkv", "attn_sink", "attn_w_o", "gla_w_in", "gla_gf_w1", "gla_gf_w2", "gla_gf_b", "gla_gb_w1", "gla_gb_w2",
           "gla_gb_b", "gla_onorm_g", "gla_w_o", "final_norm_g")
REPLICATED = ("norm_mix_g", "norm_ffn_g", "ffn_conv_b", "attn_sink", "final_norm_g", "c_ctx")
SMALL_W = 2048
ROWS16 = 16


def _layer_big(i):
    j = i // 2
    mixer = [("w_qkv", "attn_w_qkv", j), ("w_o", "attn_w_o", j)] if i % 2 == 0 else [("w_in", "gla_w_in", j), ("w_o", "gla_w_o", j)]
    return [("w_up", "ffn_w_up", i), ("w_down", "ffn_w_down", i)] + mixer


def _layer_weights(i, big, small, rep):
    D = rep["norm_mix_g"].shape[1]
    j = i // 2
    w = dict(g_mix=rep["norm_mix_g"][i][None], g_ffn=rep["norm_ffn_g"][i][None], conv_w=small["ffn_conv_w"][i],
             conv_b=rep["ffn_conv_b"][i][None], **big)
    if i % 2 == 0:
        w["sink"] = rep["attn_sink"][j]
    else:
        r = GATE_RANK
        w2 = jnp.zeros((128, D), F32)
        w2 = w2.at[0:r, 0:D // 2].set(small["gla_gf_w2"][j]).at[r:2 * r, D // 2:].set(small["gla_gb_w2"][j])
        w1x = jnp.concatenate([small["gla_gf_w1"][j], small["gla_gb_w1"][j], jnp.zeros((D, 128 - 2 * r), F32)], axis=1)
        w.update(w1x=w1x.astype(CDT), w2=w2.astype(CDT), gbias=jnp.concatenate([small["gla_gf_b"][j], small["gla_gb_b"][j]])[None],
                 onorm=small["gla_onorm_g"][j][None])
    return w


def _small_grads(grads, D):
    att = [g for g in grads if "sink" in g]
    gla = [g for g in grads if "w1x" in g]
    st = lambda xs: jnp.stack(xs, axis=0)
    r = GATE_RANK
    return {
        "ffn_conv_w": st([g["conv_w"] for g in grads]),
        "gla_gf_w1": st([g["w1x"][:, 0:r] for g in gla]), "gla_gb_w1": st([g["w1x"][:, r:2 * r] for g in gla]),
        "gla_gf_w2": st([g["w2"][0:r, :D // 2] for g in gla]), "gla_gb_w2": st([g["w2"][r:2 * r, D // 2:] for g in gla]),
        "gla_gf_b": st([g["gbias"][:D // 2] for g in gla]), "gla_gb_b": st([g["gbias"][D // 2:] for g in gla]),
        "gla_onorm_g": st([g["onorm"] for g in gla]),
        "norm_mix_g": st([g["g_mix"] for g in grads]), "norm_ffn_g": st([g["g_ffn"] for g in grads]),
        "ffn_conv_b": st([g["conv_b"] for g in grads]), "attn_sink": st([g["sink"] for g in att]),
    }


def kernel(x, c, ctx, c_ctx, ada_w, ada_b, norm_mix_g, norm_ffn_g, ffn_w_up, ffn_conv_w, ffn_conv_b, ffn_w_down, attn_w_qkv, attn_sink, attn_w_o, gla_w_in, gla_gf_w1, gla_gf_w2, gla_gf_b, gla_gb_w1, gla_gb_w2, gla_gb_b, gla_onorm_g, gla_w_o, final_norm_g, loss_target, m_c_ctx, m_ada_w, m_ada_b, m_norm_mix_g, m_norm_ffn_g, m_ffn_w_up, m_ffn_conv_w, m_ffn_conv_b, m_ffn_w_down, m_attn_w_qkv, m_attn_sink, m_attn_w_o, m_gla_w_in, m_gla_gf_w1, m_gla_gf_w2, m_gla_gf_b, m_gla_gb_w1, m_gla_gb_w2, m_gla_gb_b, m_gla_onorm_g, m_gla_w_o, m_final_norm_g, v_c_ctx, v_ada_w, v_ada_b, v_norm_mix_g, v_norm_ffn_g, v_ffn_w_up, v_ffn_conv_w, v_ffn_conv_b, v_ffn_w_down, v_attn_w_qkv, v_attn_sink, v_attn_w_o, v_gla_w_in, v_gla_gf_w1, v_gla_gf_w2, v_gla_gf_b, v_gla_gb_w1, v_gla_gb_w2, v_gla_gb_b, v_gla_onorm_g, v_gla_w_o, v_final_norm_g):
    wts = dict(c_ctx=c_ctx, ada_w=ada_w, ada_b=ada_b, norm_mix_g=norm_mix_g, norm_ffn_g=norm_ffn_g, ffn_w_up=ffn_w_up,
               ffn_conv_w=ffn_conv_w, ffn_conv_b=ffn_conv_b, ffn_w_down=ffn_w_down, attn_w_qkv=attn_w_qkv, attn_sink=attn_sink,
               attn_w_o=attn_w_o, gla_w_in=gla_w_in, gla_gf_w1=gla_gf_w1, gla_gf_w2=gla_gf_w2, gla_gf_b=gla_gf_b,
               gla_gb_w1=gla_gb_w1, gla_gb_w2=gla_gb_w2, gla_gb_b=gla_gb_b, gla_onorm_g=gla_onorm_g, gla_w_o=gla_w_o,
               final_norm_g=final_norm_g)
    mom_m = dict(zip(WEIGHTS, (m_c_ctx, m_ada_w, m_ada_b, m_norm_mix_g, m_norm_ffn_g, m_ffn_w_up, m_ffn_conv_w, m_ffn_conv_b,
                               m_ffn_w_down, m_attn_w_qkv, m_attn_sink, m_attn_w_o, m_gla_w_in, m_gla_gf_w1, m_gla_gf_w2,
                               m_gla_gf_b, m_gla_gb_w1, m_gla_gb_w2, m_gla_gb_b, m_gla_onorm_g, m_gla_w_o, m_final_norm_g)))
    mom_v = dict(zip(WEIGHTS, (v_c_ctx, v_ada_w, v_ada_b, v_norm_mix_g, v_norm_ffn_g, v_ffn_w_up, v_ffn_conv_w, v_ffn_conv_b,
                               v_ffn_w_down, v_attn_w_qkv, v_attn_sink, v_attn_w_o, v_gla_w_in, v_gla_gf_w1, v_gla_gf_w2,
                               v_gla_gf_b, v_gla_gb_w1, v_gla_gb_w2, v_gla_gb_b, v_gla_onorm_g, v_gla_w_o, v_final_norm_g)))
    L, D, W6 = ada_w.shape[0], ada_w.shape[1], ada_w.shape[2]
    M = ctx.shape[1]
    mx, my, mc = _me()
    chip = 2 * mx + my
    batch = 4 * mx + 2 * my + mc

    crow = jnp.concatenate([c.astype(F32), jnp.zeros((7, D), F32)], axis=0)
    call = _all_gather8(crow, name="gather_c")[:, 0, :]
    s16 = jnp.concatenate([jax.nn.silu(call), jax.nn.silu(c_ctx)[None], jnp.zeros((ROWS16 - 9, D), F32)], axis=0)
    s16c = s16.astype(CDT)
    ada_c = ada_w.astype(CDT)
    mod_cols = jnp.concatenate([_mm(s16c, ada_c[i], out_dtype=F32, name=f"mods_l{i}") for i in range(L)], axis=0)
    mod_all = _all_gather8(mod_cols, name="gather_mods")
    mod_all = mod_all.reshape(4, 2, L, ROWS16, W6)[:, 0]
    mod_all = jnp.moveaxis(mod_all, 0, 2).reshape(L, ROWS16, 4 * W6) + ada_b[:, None, :]
    mod_mine = jnp.stack([mod_all[:, 8], lax.dynamic_index_in_dim(mod_all, batch, axis=1, keepdims=False)], axis=1)
    mods = mod_mine.reshape(L, 2, N_MOD, D)

    pos = jnp.stack([chip, mc]).astype(jnp.int32)
    small_w = _gather_small({name: wts[name] for name, _ in SMALL_SHARDED})
    def place(i):
        return [_place(wts[name], j, pos, name=f"l{i}_place_{key}") for key, name, j in _layer_big(i)]

    placed0 = place(0)
    gathered = {0: [None, None] + list(_gather_layer(placed0[2:4], name="l0_gather_mixer"))}
    carried = {0: dict(up=_ride_gather(placed0[0:1]), down=_ride_gather(placed0[1:2]))}

    def weights_of(i, part):
        keys = [key for key, _, _ in _layer_big(i)]
        if part == "mixer":
            if i not in gathered:
                r = carried.pop(i)
                gathered[i] = _gather_d2d(r["up"]["result"] + r["down"]["result"] + r["mixer"]["result"], name=f"l{i}_gather_d2d")
            return _layer_weights(i, dict(zip(keys[2:4], gathered[i][2:4])), small_w, wts)
        if i == 0:
            r = carried.pop(0)
            gathered[0][0:2] = _gather_d2d(r["up"]["result"] + r["down"]["result"], name="l0_gather_d2d")
        return dict(zip(keys[0:2], gathered[i][0:2]))

    def fwd_rides(i):
        hosts = dict(attn=carried[0]["up"], mix=carried[0]["down"]) if i == 0 else {}
        if i + 1 < L:
            bufs = place(i + 1)
            nxt = dict(up=_ride_gather(bufs[0:1]), down=_ride_gather(bufs[1:2]), mixer=_ride_gather(bufs[2:4]))
            carried[i + 1] = nxt
            hosts.update(up=nxt["up"], down=nxt["down"])
            hosts["mix" if i % 2 == 1 else ("conv" if i == 0 else "attn")] = nxt["mixer"]
        return hosts

    red = {name: jnp.zeros(wts[name].shape, F32) for name in ("ffn_w_up", "ffn_w_down", "attn_w_qkv", "attn_w_o", "gla_w_in", "gla_w_o")}
    pending = {}

    def pair_sums(i, keys, g, tag):
        gs = [g.pop(key) for key, _, _ in keys]
        gots = _rs_split(gs, name=f"l{i}_rs_split{tag}")
        return [_rs_add(gv, got, pos, name=f"l{i}_rs_add_{key}") for (key, _, _), gv, got in zip(keys, gs, gots)]

    def bwd_rides(i):
        if i + 1 not in pending:
            return {}
        parts = pending[i + 1]["parts"]
        r = dict(dup_x=_ride_scatter(parts[0:1]), ddown_x=_ride_scatter(parts[1:2]), ddown_w=_ride_scatter(parts[2:4]))
        pending[i + 1]["rides"] = [r["dup_x"], r["ddown_x"], r["ddown_w"]]
        return r

    def early_ffn(i, g):
        if i != 0:
            return None
        ride = _ride_scatter(pair_sums(0, _layer_big(0)[0:2], g, "_ffn"))
        pending["ffn0"] = ride
        return ride

    def finish_reduce(i):
        keys = _layer_big(i)
        p = pending.pop(i)
        if i == 0:
            early = pending.pop("ffn0")
            parts = early["ins"] + p["parts"]
            recvs = early["result"] + list(_rs_scatter(p["parts"], name="l0_rs_scatter_mixer"))
        else:
            parts = p["parts"]
            recvs = [buf for r in p["rides"] for buf in r["result"]]
        outs = [_rs_sum(part, recv, red[name], j, pos, name=f"l{i}_rs_sum_{key}") for (key, name, j), part, recv in zip(keys, parts, recvs)]
        outs = _rs_share(outs, [j for _, _, j in keys], name=f"l{i}_rs_share")
        for (_, name, _), out in zip(keys, outs):
            red[name] = out

    def reduce_layer(i, g):
        if i + 1 in pending:
            finish_reduce(i + 1)
        keys = [k for k in _layer_big(i) if k[0] in g]
        pending[i] = dict(parts=pair_sums(i, keys, g, ""))
        return g

    xcat = jnp.concatenate([ctx[0], x[0]], axis=0)
    loss, dx, dmods, grads, d_final_g = _local_step(xcat, loss_target[0], mods, weights_of, final_norm_g[None], M=M,
                                                    fwd_rides=fwd_rides, bwd_rides=bwd_rides, on_ffn_grads=early_ffn,
                                                    on_grads=reduce_layer)
    finish_reduce(0)
    loss = lax.psum(loss, ("x", "y", "c"))
    grad_x = dx[M:][None]

    dm_all = _all_gather8(dmods.reshape(L * 2, N_MOD * D), name="gather_dmods")
    dm_sum = _sum_lead(dm_all, name="dmods_sum").reshape(L, 2, N_MOD * D)
    dm_rows = dm_all.reshape(8, L, 2, N_MOD * D)[:, :, 1]
    dm16 = jnp.concatenate([jnp.moveaxis(dm_rows, 0, 1), dm_sum[:, 0:1], jnp.zeros((L, ROWS16 - 9, N_MOD * D), F32)], axis=1)
    dm16 = lax.dynamic_slice_in_dim(dm16, chip * W6, W6, axis=2).astype(CDT)
    g_ada_w = jnp.stack([_mm(s16c, dm16[i], ta=True, out_dtype=F32, name=f"dada_w_l{i}") for i in range(L)], axis=0)
    ds16 = _mm(dm16[0], ada_c[0], tb=True, out_dtype=F32, name="dcond_l0")
    for i in range(1, L):
        ds16 = ds16 + _mm(dm16[i], ada_c[i], tb=True, out_dtype=F32, name=f"dcond_l{i}")
    d_sctx = ds16[8] * jnp.where(mc == 0, 1.0, 0.0)

    gfull = _small_grads(grads, D)
    gfull["final_norm_g"] = d_final_g
    gfull["c_ctx"] = d_sctx

    small_names = list(REPLICATED) + [name for name, _ in SMALL_SHARDED]
    flat = jnp.concatenate([gfull[name].astype(F32).reshape(-1) for name in small_names])
    small = _sum_lead(_all_gather8(_rows_of(flat, SMALL_W), name="gather_small_g"), name="small_sum").reshape(-1)
    off = 0
    for name in small_names:
        shape = gfull[name].shape
        red[name] = small[off:off + _size(shape)].reshape(shape)
        off += _size(shape)
    for name, ax in SMALL_SHARDED:
        shape = wts[name].shape
        g4 = red[name].reshape(shape[:ax] + (4, shape[ax]) + shape[ax + 1:])
        red[name] = lax.dynamic_index_in_dim(g4, chip, axis=ax, keepdims=False)
    sig = jax.nn.sigmoid(c_ctx)
    red["c_ctx"] = red["c_ctx"] * (sig * (1.0 + c_ctx * (1.0 - sig)))
    red["ada_w"] = g_ada_w
    red["ada_b"] = dm_sum[:, 0] + dm_sum[:, 1]

    deltas, new_m, new_v = {}, {}, {}
    for name in WEIGHTS:
        w = wts[name]
        view = (lambda a: a.reshape(-1, a.shape[-1])) if w.ndim > 1 else (lambda a: a.reshape(1, -1))
        d, m2, v2 = _adamw(view(w), view(red[name]), view(mom_m[name]), view(mom_v[name]), name=f"adamw_{name}")
        deltas[name], new_m[name], new_v[name] = d.reshape(w.shape), m2.reshape(w.shape), v2.reshape(w.shape)
    return (loss, grad_x, *[red[n] for n in WEIGHTS], *[deltas[n] for n in WEIGHTS], *[new_m[n] for n in WEIGHTS],
            *[new_v[n] for n in WEIGHTS])
```
